```python
import jax, jax.numpy as jnp
from jax import lax
import numpy as np

D_MODEL = 2048
BATCH = 8
SEQ = 8192
DEPTH = 4

F32 = jnp.float32
NORM_EPS = 1e-6
HALF = D_MODEL // 2
DN_DK = 128
DN_DV = 128
DN_HEADS = HALF // DN_DV
DN_QK = DN_HEADS * DN_DK
DN_V = DN_HEADS * DN_DV
DN_CONV = 4
DN_CHUNK = 64
LRU_WIDTH = HALF
LRU_BLOCKS = 8
LRU_BLK = LRU_WIDTH // LRU_BLOCKS
LRU_CONV = 4
LRU_C = 8.0
SWA_DIM = 128
SWA_HEADS = HALF // SWA_DIM
SWA_W = SWA_HEADS * SWA_DIM
SWA_BRANCHES = ((128, 1), (512, 4), (2048, 16))
SWA_BLOCK = 128
RET_DK = 128
RET_DV = 256
RET_HEADS = HALF // RET_DV
RET_QK = RET_HEADS * RET_DK
RET_V = RET_HEADS * RET_DV
RET_CHUNK = 64
GN_EPS = 1e-5
D_FF = 4 * D_MODEL
PLE_DIM = 256
N_EVEN = (DEPTH + 1) // 2
N_ODD = DEPTH // 2
EV_SPLITS = (2 * DN_QK + DN_V, DN_V, DN_HEADS, DN_HEADS, LRU_WIDTH, LRU_WIDTH)
EV_IN = sum(EV_SPLITS)
EV_OUT = DN_V + LRU_WIDTH
OD_SPLITS = (SWA_W, SWA_W, SWA_W, RET_QK, RET_QK, RET_V, RET_V)
OD_IN = sum(OD_SPLITS)
OD_OUT = SWA_W + RET_V

kernel_name = "hybrid_deltanet_rglru_dilated_retention_trunk"


def rms_norm(x, w, eps=NORM_EPS):
    xf = x.astype(F32)
    y = xf * lax.rsqrt(jnp.mean(xf * xf, axis=-1, keepdims=True) + eps)
    return (y * w.astype(F32)).astype(x.dtype)


def l2_normalize(x, eps=1e-6):
    return x * lax.rsqrt(jnp.sum(x * x, axis=-1, keepdims=True) + eps)


def head_group_norm(x, eps=GN_EPS):
    mu = jnp.mean(x, axis=-1, keepdims=True)
    xc = x - mu
    return xc * lax.rsqrt(jnp.mean(xc * xc, axis=-1, keepdims=True) + eps)


def split_cols(x, sizes):
    return jnp.split(x, np.cumsum(sizes)[:-1].tolist(), axis=-1)


def to_heads(x, n_heads):
    b, t, _ = x.shape
    return x.reshape(b, t, n_heads, -1).transpose(0, 2, 1, 3)


def merge_heads(x):
    b, h, t, d = x.shape
    return x.transpose(0, 2, 1, 3).reshape(b, t, h * d)


def to_chunks(x, c):
    b, h, t = x.shape[:3]
    return jnp.moveaxis(x.reshape(b, h, t // c, c, *x.shape[3:]), 2, 0)


def from_chunks(x):
    n, b, h, c, d = x.shape
    return jnp.moveaxis(x, 0, 2).reshape(b, h, n * c, d)


def causal_dwconv(x, w):
    width, t = w.shape[0], x.shape[1]
    xp = jnp.pad(x, ((0, 0), (width - 1, 0), (0, 0)))
    y = xp[:, 0:t] * w[0]
    for j in range(1, width):
        y = y + xp[:, j:j + t] * w[j]
    return y


def gated_delta_rule(q, k, v, g, beta):
    b, h, _, dk = q.shape
    dv = v.shape[-1]
    c = DN_CHUNK
    q, k, v = to_chunks(q, c), to_chunks(k, c), to_chunks(v, c)
    g, beta = to_chunks(g, c), to_chunks(beta, c)
    gc = jnp.cumsum(g, axis=-1)
    causal = jnp.tril(jnp.ones((c, c), bool))
    strict = jnp.tril(jnp.ones((c, c), bool), -1)
    decay = jnp.exp(jnp.where(causal, gc[..., :, None] - gc[..., None, :], -jnp.inf))
    kb = k * beta[..., None]
    m = jnp.where(strict, jnp.einsum('nbhid,nbhjd->nbhij', kb, k) * decay, 0.0)
    rhs = jnp.concatenate([v * beta[..., None], kb * jnp.exp(gc)[..., None]], axis=-1)
    sol = lax.linalg.triangular_solve(jnp.eye(c, dtype=F32) + m, rhs, left_side=True, lower=True, unit_diagonal=True)
    u, w = sol[..., :dv], sol[..., dv:]
    qk = jnp.einsum('nbhid,nbhjd->nbhij', q, k) * decay
    q_dec = q * jnp.exp(gc)[..., None]
    k_dec = k * jnp.exp(gc[..., -1:] - gc)[..., None]
    g_tot = jnp.exp(gc[..., -1])[..., None, None]

    def step(state, xs):
        u_n, w_n, qk_n, qd_n, kd_n, gt_n = xs
        v_new = u_n - jnp.einsum('bhcd,bhde->bhce', w_n, state)
        o = jnp.einsum('bhcd,bhde->bhce', qd_n, state) + jnp.einsum('bhij,bhje->bhie', qk_n, v_new)
        state = state * gt_n + jnp.einsum('bhcd,bhce->bhde', kd_n, v_new)
        return state, o

    _, o = lax.scan(step, jnp.zeros((b, h, dk, dv), F32), (u, w, qk, q_dec, k_dec, g_tot))
    return from_chunks(o)


def rg_lru_branch(xr, yr, conv_w, conv_b, wa, ba, wx, bx, lam):
    b, t, _ = xr.shape
    xc = (causal_dwconv(xr, conv_w) + conv_b).astype(F32)
    xb = xc.reshape(b, t, LRU_BLOCKS, LRU_BLK)
    r = jax.nn.sigmoid(jnp.einsum('btgi,gij->btgj', xb, wa.astype(F32)).reshape(b, t, -1) + ba.astype(F32))
    i = jax.nn.sigmoid(jnp.einsum('btgi,gij->btgj', xb, wx.astype(F32)).reshape(b, t, -1) + bx.astype(F32))
    log_a = -LRU_C * r * jax.nn.softplus(-lam.astype(F32))
    a = jnp.exp(log_a)
    u = jnp.sqrt(-jnp.expm1(2.0 * log_a)) * (i * xc)

    def combine(e1, e2):
        a1, b1 = e1
        a2, b2 = e2
        return a1 * a2, a2 * b1 + b2

    _, hs = lax.associative_scan(combine, (a, u), axis=1)
    return hs * jax.nn.gelu(yr.astype(F32))


def even_mixer(hn, w_in, w_out, dn_conv_w, dn_a_log, dn_dt_bias, dn_norm_w,
               lru_conv_w, lru_conv_b, lru_wa, lru_ba, lru_wx, lru_bx, lru_lambda):
    proj = hn @ w_in
    qkv, z, b_raw, a_raw, xr, yr = split_cols(proj, EV_SPLITS)
    qkv = jax.nn.silu(causal_dwconv(qkv, dn_conv_w)).astype(F32)
    q, k, v = split_cols(qkv, (DN_QK, DN_QK, DN_V))
    q = l2_normalize(to_heads(q, DN_HEADS)) * (DN_DK ** -0.5)
    k = l2_normalize(to_heads(k, DN_HEADS))
    v = to_heads(v, DN_HEADS)
    beta = jax.nn.sigmoid(b_raw.astype(F32)).transpose(0, 2, 1)
    g = (-jnp.exp(dn_a_log.astype(F32)) * jax.nn.softplus(a_raw.astype(F32) + dn_dt_bias.astype(F32))).transpose(0, 2, 1)
    o = gated_delta_rule(q, k, v, g, beta)
    o = rms_norm(o, dn_norm_w) * jax.nn.silu(to_heads(z.astype(F32), DN_HEADS))
    y_a = merge_heads(o)
    y_b = rg_lru_branch(xr, yr, lru_conv_w, lru_conv_b, lru_wa, lru_ba, lru_wx, lru_bx, lru_lambda)
    return jnp.concatenate([y_a, y_b], axis=-1).astype(hn.dtype) @ w_out


def dilated_branch(q, k, v, slopes, window, dilation):
    b, h, t, dh = q.shape
    d = dilation
    span = window // dilation
    n_len = t // d
    nb = -(-n_len // SWA_BLOCK)
    lp = nb * SWA_BLOCK

    def to_res(x):
        x = x.reshape(b, h, n_len, d, dh).transpose(0, 1, 3, 2, 4)
        x = jnp.pad(x, ((0, 0), (0, 0), (0, 0), (0, lp - n_len), (0, 0)))
        return x.reshape(b, h, d, nb, SWA_BLOCK, dh)

    def with_prev(x):
        prev = jnp.pad(x[:, :, :, :-1], ((0, 0), (0, 0), (0, 0), (1, 0), (0, 0), (0, 0)))
        return jnp.concatenate([prev, x], axis=4)

    def from_res(x):
        x = x.reshape(b, h, d, lp, *x.shape[5:])[:, :, :, :n_len]
        x = jnp.moveaxis(x, 2, 3)
        return x.reshape(b, h, t, *x.shape[4:])

    qr = to_res(q)
    kk, vv = with_prev(to_res(k)), with_prev(to_res(v))
    iq = jnp.arange(SWA_BLOCK)
    ik = jnp.arange(2 * SWA_BLOCK)
    rel = SWA_BLOCK + iq[:, None] - ik[None, :]
    blk = jnp.arange(nb)
    valid = (rel >= 0) & (rel <= span) & ((blk[:, None, None] > 0) | (ik >= SWA_BLOCK)[None, None, :])
    s = jnp.einsum('bhrnqd,bhrnkd->bhrnqk', qr, kk)
    s = s - slopes[:, None, None, None, None] * (rel * d).astype(F32)
    s = jnp.where(valid, s, -jnp.inf)
    mx = jnp.max(s, axis=-1, keepdims=True)
    pr = jnp.exp(s - mx)
    den = jnp.sum(pr, axis=-1, keepdims=True)
    o = jnp.einsum('bhrnqk,bhrnkd->bhrnqd', pr, vv) / den
    lse = (mx + jnp.log(den))[..., 0]
    return from_res(o), from_res(lse)


def retention(q, k, v, log_gamma):
    b, h, _, dk = q.shape
    dv = v.shape[-1]
    c = RET_CHUNK
    idx = jnp.arange(c, dtype=F32)
    rel = idx[:, None] - idx[None, :]
    dmask = jnp.where(rel >= 0, jnp.exp(jnp.maximum(rel, 0.0)[None] * log_gamma[:, None, None]), 0.0)
    qc, kc, vc = to_chunks(q, c), to_chunks(k, c), to_chunks(v, c)
    intra = jnp.einsum('nbhij,nbhje->nbhie', jnp.einsum('nbhid,nbhjd->nbhij', qc, kc) * dmask, vc)
    q_dec = qc * jnp.exp((idx + 1.0)[None, :] * log_gamma[:, None])[:, :, None]
    k_dec = kc * jnp.exp((c - 1.0 - idx)[None, :] * log_gamma[:, None])[:, :, None]
    chunk_decay = jnp.exp(c * log_gamma)[:, None, None]

    def step(state, xs):
        qd, kd, vn = xs
        o = jnp.einsum('bhcd,bhde->bhce', qd, state)
        state = state * chunk_decay + jnp.einsum('bhcd,bhce->bhde', kd, vn)
        return state, o

    _, inter = lax.scan(step, jnp.zeros((b, h, dk, dv), F32), (q_dec, k_dec, vc))
    return from_chunks(intra + inter)


def odd_mixer(hn, w_in, w_out):
    proj = hn @ w_in
    cq, ck, cv, rq, rk, rv, rg = split_cols(proj.astype(F32), OD_SPLITS)
    q = to_heads(cq, SWA_HEADS) * (SWA_DIM ** -0.5)
    k = to_heads(ck, SWA_HEADS)
    v = to_heads(cv, SWA_HEADS)
    slopes = jnp.exp2(-8.0 * jnp.arange(1, SWA_HEADS + 1, dtype=F32) / SWA_HEADS)
    outs, lses = [], []
    for window, dilation in SWA_BRANCHES:
        o_i, lse_i = dilated_branch(q, k, v, slopes, window, dilation)
        outs.append(o_i)
        lses.append(lse_i)
    wts = jax.nn.softmax(jnp.stack(lses), axis=0)
    y_c = merge_heads(jnp.sum(jnp.stack(outs) * wts[..., None], axis=0))
    log_gamma = jnp.log1p(-jnp.exp2(-5.0 - jnp.arange(RET_HEADS, dtype=F32)))
    o_r = retention(to_heads(rq, RET_HEADS), to_heads(rk, RET_HEADS) * (RET_DK ** -0.5), to_heads(rv, RET_HEADS), log_gamma)
    o_r = head_group_norm(o_r) * jax.nn.silu(to_heads(rg, RET_HEADS))
    y_d = merge_heads(o_r)
    return jnp.concatenate([y_c, y_d], axis=-1).astype(hn.dtype) @ w_out


def _fwd_setup_inputs(seed: int = 0) -> dict:
    key = jax.random.key(seed)
    ks = iter(jax.random.split(key, 32))

    def normal(shape, fan_in):
        return jax.random.normal(next(ks), shape, F32) * (fan_in ** -0.5)

    def gain(shape):
        return 1.0 + 0.02 * jax.random.normal(next(ks), shape, F32)

    def small(shape):
        return 0.02 * jax.random.normal(next(ks), shape, F32)

    x = jax.random.normal(next(ks), (BATCH, SEQ, D_MODEL), F32)
    p = jax.random.normal(next(ks), (DEPTH, BATCH, SEQ, PLE_DIM), F32)
    ln_mix_w = gain((DEPTH, D_MODEL))
    ln_mlp_w = gain((DEPTH, D_MODEL))
    ln_ple_w = gain((DEPTH, D_MODEL))
    w_up = normal((DEPTH, D_MODEL, D_FF), D_MODEL)
    w_down = normal((DEPTH, D_FF, D_MODEL), D_FF)
    w_ple_proj = normal((DEPTH, PLE_DIM, D_MODEL), PLE_DIM)
    w_ple_gate = normal((DEPTH, D_MODEL, D_MODEL), D_MODEL)
    ln_final_w = gain((D_MODEL,))
    ev_w_in = normal((N_EVEN, D_MODEL, EV_IN), D_MODEL)
    ev_w_out = normal((N_EVEN, EV_OUT, D_MODEL), EV_OUT)
    dn_conv_w = normal((N_EVEN, DN_CONV, 2 * DN_QK + DN_V), DN_CONV)
    dn_a_log = jnp.log(jax.random.uniform(next(ks), (N_EVEN, DN_HEADS), F32, 1.0, 16.0))
    dt = jnp.exp(jax.random.uniform(next(ks), (N_EVEN, DN_HEADS), F32, float(np.log(1e-3)), float(np.log(1e-1))))
    dn_dt_bias = dt + jnp.log(-jnp.expm1(-dt))
    dn_norm_w = gain((N_EVEN, DN_DV))
    lru_conv_w = normal((N_EVEN, LRU_CONV, LRU_WIDTH), LRU_CONV)
    lru_conv_b = small((N_EVEN, LRU_WIDTH))
    lru_wa = normal((N_EVEN, LRU_BLOCKS, LRU_BLK, LRU_BLK), LRU_BLK)
    lru_ba = small((N_EVEN, LRU_WIDTH))
    lru_wx = normal((N_EVEN, LRU_BLOCKS, LRU_BLK, LRU_BLK), LRU_BLK)
    lru_bx = small((N_EVEN, LRU_WIDTH))
    a_pow_c = jax.random.uniform(next(ks), (N_EVEN, LRU_WIDTH), F32, 0.9, 0.999)
    log_a = jnp.log(a_pow_c) / LRU_C
    lru_lambda = log_a - jnp.log(-jnp.expm1(log_a))
    od_w_in = normal((N_ODD, D_MODEL, OD_IN), D_MODEL)
    od_w_out = normal((N_ODD, OD_OUT, D_MODEL), OD_OUT)
    return {"x": x, "p": p, "ln_mix_w": ln_mix_w, "ln_mlp_w": ln_mlp_w, "ln_ple_w": ln_ple_w,
            "w_up": w_up, "w_down": w_down, "w_ple_proj": w_ple_proj, "w_ple_gate": w_ple_gate,
            "ln_final_w": ln_final_w, "ev_w_in": ev_w_in, "ev_w_out": ev_w_out, "dn_conv_w": dn_conv_w,
            "dn_a_log": dn_a_log, "dn_dt_bias": dn_dt_bias, "dn_norm_w": dn_norm_w,
            "lru_conv_w": lru_conv_w, "lru_conv_b": lru_conv_b, "lru_wa": lru_wa, "lru_ba": lru_ba,
            "lru_wx": lru_wx, "lru_bx": lru_bx, "lru_lambda": lru_lambda,
            "od_w_in": od_w_in, "od_w_out": od_w_out}


def _fwd_reference(x, p, ln_mix_w, ln_mlp_w, ln_ple_w, w_up, w_down, w_ple_proj, w_ple_gate, ln_final_w,
              ev_w_in, ev_w_out, dn_conv_w, dn_a_log, dn_dt_bias, dn_norm_w,
              lru_conv_w, lru_conv_b, lru_wa, lru_ba, lru_wx, lru_bx, lru_lambda,
              od_w_in, od_w_out):
    h = x
    for i in range(DEPTH):
        j = i // 2
        hn = rms_norm(h, ln_mix_w[i])
        if i % 2 == 0:
            mix = even_mixer(hn, ev_w_in[j], ev_w_out[j], dn_conv_w[j], dn_a_log[j], dn_dt_bias[j], dn_norm_w[j],
                             lru_conv_w[j], lru_conv_b[j], lru_wa[j], lru_ba[j], lru_wx[j], lru_bx[j], lru_lambda[j])
        else:
            mix = odd_mixer(hn, od_w_in[j], od_w_out[j])
        h = h + mix
        hn = rms_norm(h, ln_mlp_w[i])
        h = h + jnp.square(jax.nn.relu(hn @ w_up[i])) @ w_down[i]
        hn = rms_norm(h, ln_ple_w[i])
        h = h + jax.nn.sigmoid(hn @ w_ple_gate[i]) * (p[i] @ w_ple_proj[i])
    return rms_norm(h, ln_final_w)


import jax as _jax
import jax.numpy as _jnp

TWIN_FORMAT = 'train_step'
FWD_PARAMS = ['x', 'p', 'ln_mix_w', 'ln_mlp_w', 'ln_ple_w', 'w_up', 'w_down', 'w_ple_proj', 'w_ple_gate', 'ln_final_w', 'ev_w_in', 'ev_w_out', 'dn_conv_w', 'dn_a_log', 'dn_dt_bias', 'dn_norm_w', 'lru_conv_w', 'lru_conv_b', 'lru_wa', 'lru_ba', 'lru_wx', 'lru_bx', 'lru_lambda', 'od_w_in', 'od_w_out']
TWIN_WEIGHTS = ['ln_mix_w', 'ln_mlp_w', 'ln_ple_w', 'w_up', 'w_down', 'w_ple_proj', 'w_ple_gate', 'ln_final_w', 'ev_w_in', 'ev_w_out', 'dn_conv_w', 'dn_a_log', 'dn_dt_bias', 'dn_norm_w', 'lru_conv_w', 'lru_conv_b', 'lru_wa', 'lru_ba', 'lru_wx', 'lru_bx', 'lru_lambda', 'od_w_in', 'od_w_out']
TWIN_DIFF_INPUT = 'x'
TWIN_INPUTS = ['x', 'p', 'ln_mix_w', 'ln_mlp_w', 'ln_ple_w', 'w_up', 'w_down', 'w_ple_proj', 'w_ple_gate', 'ln_final_w', 'ev_w_in', 'ev_w_out', 'dn_conv_w', 'dn_a_log', 'dn_dt_bias', 'dn_norm_w', 'lru_conv_w', 'lru_conv_b', 'lru_wa', 'lru_ba', 'lru_wx', 'lru_bx', 'lru_lambda', 'od_w_in', 'od_w_out', 'loss_target', 'm_ln_mix_w', 'm_ln_mlp_w', 'm_ln_ple_w', 'm_w_up', 'm_w_down', 'm_w_ple_proj', 'm_w_ple_gate', 'm_ln_final_w', 'm_ev_w_in', 'm_ev_w_out', 'm_dn_conv_w', 'm_dn_a_log', 'm_dn_dt_bias', 'm_dn_norm_w', 'm_lru_conv_w', 'm_lru_conv_b', 'm_lru_wa', 'm_lru_ba', 'm_lru_wx', 'm_lru_bx', 'm_lru_lambda', 'm_od_w_in', 'm_od_w_out', 'v_ln_mix_w', 'v_ln_mlp_w', 'v_ln_ple_w', 'v_w_up', 'v_w_down', 'v_w_ple_proj', 'v_w_ple_gate', 'v_ln_final_w', 'v_ev_w_in', 'v_ev_w_out', 'v_dn_conv_w', 'v_dn_a_log', 'v_dn_dt_bias', 'v_dn_norm_w', 'v_lru_conv_w', 'v_lru_conv_b', 'v_lru_wa', 'v_lru_ba', 'v_lru_wx', 'v_lru_bx', 'v_lru_lambda', 'v_od_w_in', 'v_od_w_out']
TWIN_OUTPUTS = ['loss', 'grad_x', 'grad_ln_mix_w', 'grad_ln_mlp_w', 'grad_ln_ple_w', 'grad_w_up', 'grad_w_down', 'grad_w_ple_proj', 'grad_w_ple_gate', 'grad_ln_final_w', 'grad_ev_w_in', 'grad_ev_w_out', 'grad_dn_conv_w', 'grad_dn_a_log', 'grad_dn_dt_bias', 'grad_dn_norm_w', 'grad_lru_conv_w', 'grad_lru_conv_b', 'grad_lru_wa', 'grad_lru_ba', 'grad_lru_wx', 'grad_lru_bx', 'grad_lru_lambda', 'grad_od_w_in', 'grad_od_w_out', 'delta_ln_mix_w', 'delta_ln_mlp_w', 'delta_ln_ple_w', 'delta_w_up', 'delta_w_down', 'delta_w_ple_proj', 'delta_w_ple_gate', 'delta_ln_final_w', 'delta_ev_w_in', 'delta_ev_w_out', 'delta_dn_conv_w', 'delta_dn_a_log', 'delta_dn_dt_bias', 'delta_dn_norm_w', 'delta_lru_conv_w', 'delta_lru_conv_b', 'delta_lru_wa', 'delta_lru_ba', 'delta_lru_wx', 'delta_lru_bx', 'delta_lru_lambda', 'delta_od_w_in', 'delta_od_w_out', 'new_m_ln_mix_w', 'new_m_ln_mlp_w', 'new_m_ln_ple_w', 'new_m_w_up', 'new_m_w_down', 'new_m_w_ple_proj', 'new_m_w_ple_gate', 'new_m_ln_final_w', 'new_m_ev_w_in', 'new_m_ev_w_out', 'new_m_dn_conv_w', 'new_m_dn_a_log', 'new_m_dn_dt_bias', 'new_m_dn_norm_w', 'new_m_lru_conv_w', 'new_m_lru_conv_b', 'new_m_lru_wa', 'new_m_lru_ba', 'new_m_lru_wx', 'new_m_lru_bx', 'new_m_lru_lambda', 'new_m_od_w_in', 'new_m_od_w_out', 'new_v_ln_mix_w', 'new_v_ln_mlp_w', 'new_v_ln_ple_w', 'new_v_w_up', 'new_v_w_down', 'new_v_w_ple_proj', 'new_v_w_ple_gate', 'new_v_ln_final_w', 'new_v_ev_w_in', 'new_v_ev_w_out', 'new_v_dn_conv_w', 'new_v_dn_a_log', 'new_v_dn_dt_bias', 'new_v_dn_norm_w', 'new_v_lru_conv_w', 'new_v_lru_conv_b', 'new_v_lru_wa', 'new_v_lru_ba', 'new_v_lru_wx', 'new_v_lru_bx', 'new_v_lru_lambda', 'new_v_od_w_in', 'new_v_od_w_out']
TWIN_LEAF_KINDS = {'loss': 'loss', 'grad_x': 'grad_x', 'grad_ln_mix_w': 'grad_w', 'grad_ln_mlp_w': 'grad_w', 'grad_ln_ple_w': 'grad_w', 'grad_w_up': 'grad_w', 'grad_w_down': 'grad_w', 'grad_w_ple_proj': 'grad_w', 'grad_w_ple_gate': 'grad_w', 'grad_ln_final_w': 'grad_w', 'grad_ev_w_in': 'grad_w', 'grad_ev_w_out': 'grad_w', 'grad_dn_conv_w': 'grad_w', 'grad_dn_a_log': 'grad_w', 'grad_dn_dt_bias': 'grad_w', 'grad_dn_norm_w': 'grad_w', 'grad_lru_conv_w': 'grad_w', 'grad_lru_conv_b': 'grad_w', 'grad_lru_wa': 'grad_w', 'grad_lru_ba': 'grad_w', 'grad_lru_wx': 'grad_w', 'grad_lru_bx': 'grad_w', 'grad_lru_lambda': 'grad_w', 'grad_od_w_in': 'grad_w', 'grad_od_w_out': 'grad_w', 'delta_ln_mix_w': 'delta_w', 'delta_ln_mlp_w': 'delta_w', 'delta_ln_ple_w': 'delta_w', 'delta_w_up': 'delta_w', 'delta_w_down': 'delta_w', 'delta_w_ple_proj': 'delta_w', 'delta_w_ple_gate': 'delta_w', 'delta_ln_final_w': 'delta_w', 'delta_ev_w_in': 'delta_w', 'delta_ev_w_out': 'delta_w', 'delta_dn_conv_w': 'delta_w', 'delta_dn_a_log': 'delta_w', 'delta_dn_dt_bias': 'delta_w', 'delta_dn_norm_w': 'delta_w', 'delta_lru_conv_w': 'delta_w', 'delta_lru_conv_b': 'delta_w', 'delta_lru_wa': 'delta_w', 'delta_lru_ba': 'delta_w', 'delta_lru_wx': 'delta_w', 'delta_lru_bx': 'delta_w', 'delta_lru_lambda': 'delta_w', 'delta_od_w_in': 'delta_w', 'delta_od_w_out': 'delta_w', 'new_m_ln_mix_w': 'new_m', 'new_m_ln_mlp_w': 'new_m', 'new_m_ln_ple_w': 'new_m', 'new_m_w_up': 'new_m', 'new_m_w_down': 'new_m', 'new_m_w_ple_proj': 'new_m', 'new_m_w_ple_gate': 'new_m', 'new_m_ln_final_w': 'new_m', 'new_m_ev_w_in': 'new_m', 'new_m_ev_w_out': 'new_m', 'new_m_dn_conv_w': 'new_m', 'new_m_dn_a_log': 'new_m', 'new_m_dn_dt_bias': 'new_m', 'new_m_dn_norm_w': 'new_m', 'new_m_lru_conv_w': 'new_m', 'new_m_lru_conv_b': 'new_m', 'new_m_lru_wa': 'new_m', 'new_m_lru_ba': 'new_m', 'new_m_lru_wx': 'new_m', 'new_m_lru_bx': 'new_m', 'new_m_lru_lambda': 'new_m', 'new_m_od_w_in': 'new_m', 'new_m_od_w_out': 'new_m', 'new_v_ln_mix_w': 'new_v', 'new_v_ln_mlp_w': 'new_v', 'new_v_ln_ple_w': 'new_v', 'new_v_w_up': 'new_v', 'new_v_w_down': 'new_v', 'new_v_w_ple_proj': 'new_v', 'new_v_w_ple_gate': 'new_v', 'new_v_ln_final_w': 'new_v', 'new_v_ev_w_in': 'new_v', 'new_v_ev_w_out': 'new_v', 'new_v_dn_conv_w': 'new_v', 'new_v_dn_a_log': 'new_v', 'new_v_dn_dt_bias': 'new_v', 'new_v_dn_norm_w': 'new_v', 'new_v_lru_conv_w': 'new_v', 'new_v_lru_conv_b': 'new_v', 'new_v_lru_wa': 'new_v', 'new_v_lru_ba': 'new_v', 'new_v_lru_wx': 'new_v', 'new_v_lru_bx': 'new_v', 'new_v_lru_lambda': 'new_v', 'new_v_od_w_in': 'new_v', 'new_v_od_w_out': 'new_v'}


def _forward(args):
    return _fwd_reference(*[args[k] for k in FWD_PARAMS])


def _output_shape():
    def fwd():
        inp = _fwd_setup_inputs(0)
        return _fwd_reference(*[inp[k] for k in FWD_PARAMS])
    out = _jax.eval_shape(fwd)
    return out.shape, out.dtype

N_MICROBATCH = 1
ADAM_LR = 0.001
ADAM_B1 = 0.9
ADAM_B2 = 0.999
ADAM_EPS = 1e-08
ADAM_WD = 0.01
ADAM_STEP = 10
PER_EXAMPLE_BATCH_AXIS = {'x': 0, 'p': 1, 'loss_target': 0}
SHARED_INPUTS = []
_WEIGHT_DTYPES = {'ln_mix_w': _jnp.float32, 'ln_mlp_w': _jnp.float32, 'ln_ple_w': _jnp.float32, 'w_up': _jnp.float32, 'w_down': _jnp.float32, 'w_ple_proj': _jnp.float32, 'w_ple_gate': _jnp.float32, 'ln_final_w': _jnp.float32, 'ev_w_in': _jnp.float32, 'ev_w_out': _jnp.float32, 'dn_conv_w': _jnp.float32, 'dn_a_log': _jnp.float32, 'dn_dt_bias': _jnp.float32, 'dn_norm_w': _jnp.float32, 'lru_conv_w': _jnp.float32, 'lru_conv_b': _jnp.float32, 'lru_wa': _jnp.float32, 'lru_ba': _jnp.float32, 'lru_wx': _jnp.float32, 'lru_bx': _jnp.float32, 'lru_lambda': _jnp.float32, 'od_w_in': _jnp.float32, 'od_w_out': _jnp.float32}
MOMENT_SCALE = {'ln_mix_w': 1.038187e-01, 'ln_mlp_w': 9.076984e-02, 'ln_ple_w': 1.313355e-02, 'w_up': 4.507091e-02, 'w_down': 1.064499e-01, 'w_ple_proj': 3.319320e-02, 'w_ple_gate': 1.302612e-02, 'ln_final_w': 3.289090e+01, 'ev_w_in': 6.789233e-02, 'ev_w_out': 8.427890e-02, 'dn_conv_w': 4.909908e-02, 'dn_a_log': 3.629474e-01, 'dn_dt_bias': 3.518118e-01, 'dn_norm_w': 1.873800e-01, 'lru_conv_w': 9.966744e-02, 'lru_conv_b': 7.208968e-01, 'lru_wa': 1.930148e-02, 'lru_ba': 2.425247e-02, 'lru_wx': 3.549563e-02, 'lru_bx': 3.600297e-02, 'lru_lambda': 5.975607e-02, 'od_w_in': 4.097586e-02, 'od_w_out': 4.579690e-02}


def _to_microbatches(a, axis):
    t = _jnp.moveaxis(a, axis, 0)
    t = t.reshape((N_MICROBATCH, t.shape[0] // N_MICROBATCH) + t.shape[1:])
    return _jnp.moveaxis(t, 1, axis + 1)


def setup_inputs(seed: int = 0) -> dict:
    inp = _fwd_setup_inputs(seed)
    key = _jax.random.fold_in(_jax.random.key(seed), 7919)
    shape, _ = _output_shape()
    out = dict(inp)
    out["loss_target"] = _jax.random.normal(_jax.random.fold_in(key, 0), shape, _jnp.float32)
    for i, name in enumerate(TWIN_WEIGHTS):
        w = inp[name].astype(_jnp.float32)
        if MOMENT_SCALE is None:
            s = _jnp.sqrt(_jnp.mean(_jnp.square(w)) + 1e-30)
        else:
            s = MOMENT_SCALE[name]
        km, kv = _jax.random.split(_jax.random.fold_in(key, i + 1))
        out[name] = w
        out["m_" + name] = s * _jax.random.normal(km, w.shape, _jnp.float32)
        out["v_" + name] = (s * s) * _jax.random.uniform(kv, w.shape, _jnp.float32, 0.5, 1.5)
    if N_MICROBATCH > 1:
        for name, axis in PER_EXAMPLE_BATCH_AXIS.items():
            out[name] = _to_microbatches(out[name], axis)
    return {'x': out['x'], 'p': out['p'], 'ln_mix_w': out['ln_mix_w'], 'ln_mlp_w': out['ln_mlp_w'], 'ln_ple_w': out['ln_ple_w'], 'w_up': out['w_up'], 'w_down': out['w_down'], 'w_ple_proj': out['w_ple_proj'], 'w_ple_gate': out['w_ple_gate'], 'ln_final_w': out['ln_final_w'], 'ev_w_in': out['ev_w_in'], 'ev_w_out': out['ev_w_out'], 'dn_conv_w': out['dn_conv_w'], 'dn_a_log': out['dn_a_log'], 'dn_dt_bias': out['dn_dt_bias'], 'dn_norm_w': out['dn_norm_w'], 'lru_conv_w': out['lru_conv_w'], 'lru_conv_b': out['lru_conv_b'], 'lru_wa': out['lru_wa'], 'lru_ba': out['lru_ba'], 'lru_wx': out['lru_wx'], 'lru_bx': out['lru_bx'], 'lru_lambda': out['lru_lambda'], 'od_w_in': out['od_w_in'], 'od_w_out': out['od_w_out'], 'loss_target': out['loss_target'], 'm_ln_mix_w': out['m_ln_mix_w'], 'm_ln_mlp_w': out['m_ln_mlp_w'], 'm_ln_ple_w': out['m_ln_ple_w'], 'm_w_up': out['m_w_up'], 'm_w_down': out['m_w_down'], 'm_w_ple_proj': out['m_w_ple_proj'], 'm_w_ple_gate': out['m_w_ple_gate'], 'm_ln_final_w': out['m_ln_final_w'], 'm_ev_w_in': out['m_ev_w_in'], 'm_ev_w_out': out['m_ev_w_out'], 'm_dn_conv_w': out['m_dn_conv_w'], 'm_dn_a_log': out['m_dn_a_log'], 'm_dn_dt_bias': out['m_dn_dt_bias'], 'm_dn_norm_w': out['m_dn_norm_w'], 'm_lru_conv_w': out['m_lru_conv_w'], 'm_lru_conv_b': out['m_lru_conv_b'], 'm_lru_wa': out['m_lru_wa'], 'm_lru_ba': out['m_lru_ba'], 'm_lru_wx': out['m_lru_wx'], 'm_lru_bx': out['m_lru_bx'], 'm_lru_lambda': out['m_lru_lambda'], 'm_od_w_in': out['m_od_w_in'], 'm_od_w_out': out['m_od_w_out'], 'v_ln_mix_w': out['v_ln_mix_w'], 'v_ln_mlp_w': out['v_ln_mlp_w'], 'v_ln_ple_w': out['v_ln_ple_w'], 'v_w_up': out['v_w_up'], 'v_w_down': out['v_w_down'], 'v_w_ple_proj': out['v_w_ple_proj'], 'v_w_ple_gate': out['v_w_ple_gate'], 'v_ln_final_w': out['v_ln_final_w'], 'v_ev_w_in': out['v_ev_w_in'], 'v_ev_w_out': out['v_ev_w_out'], 'v_dn_conv_w': out['v_dn_conv_w'], 'v_dn_a_log': out['v_dn_a_log'], 'v_dn_dt_bias': out['v_dn_dt_bias'], 'v_dn_norm_w': out['v_dn_norm_w'], 'v_lru_conv_w': out['v_lru_conv_w'], 'v_lru_conv_b': out['v_lru_conv_b'], 'v_lru_wa': out['v_lru_wa'], 'v_lru_ba': out['v_lru_ba'], 'v_lru_wx': out['v_lru_wx'], 'v_lru_bx': out['v_lru_bx'], 'v_lru_lambda': out['v_lru_lambda'], 'v_od_w_in': out['v_od_w_in'], 'v_od_w_out': out['v_od_w_out']}


def _loss(weights, diff, rest, loss_target):
    with _jax.named_scope("forward"):
        args = {**rest, TWIN_DIFF_INPUT: diff, **{k: w.astype(_WEIGHT_DTYPES[k]) for k, w in weights.items()}}
        y = _forward(args)
    with _jax.named_scope("loss_head"):
        err = _jnp.square(y.astype(_jnp.float32) - loss_target)
        return 0.5 * _jnp.sum(_jnp.mean(err, axis=-1)) if err.ndim else 0.5 * err


def _adamw(w, g, m, v):
    m = ADAM_B1 * m + (1.0 - ADAM_B1) * g
    v = ADAM_B2 * v + (1.0 - ADAM_B2) * _jnp.square(g)
    m_hat = m / (1.0 - ADAM_B1 ** ADAM_STEP)
    v_hat = v / (1.0 - ADAM_B2 ** ADAM_STEP)
    delta = -ADAM_LR * (m_hat / (_jnp.sqrt(v_hat) + ADAM_EPS) + ADAM_WD * w)
    return delta, m, v


def reference(x, p, ln_mix_w, ln_mlp_w, ln_ple_w, w_up, w_down, w_ple_proj, w_ple_gate, ln_final_w, ev_w_in, ev_w_out, dn_conv_w, dn_a_log, dn_dt_bias, dn_norm_w, lru_conv_w, lru_conv_b, lru_wa, lru_ba, lru_wx, lru_bx, lru_lambda, od_w_in, od_w_out, loss_target, m_ln_mix_w, m_ln_mlp_w, m_ln_ple_w, m_w_up, m_w_down, m_w_ple_proj, m_w_ple_gate, m_ln_final_w, m_ev_w_in, m_ev_w_out, m_dn_conv_w, m_dn_a_log, m_dn_dt_bias, m_dn_norm_w, m_lru_conv_w, m_lru_conv_b, m_lru_wa, m_lru_ba, m_lru_wx, m_lru_bx, m_lru_lambda, m_od_w_in, m_od_w_out, v_ln_mix_w, v_ln_mlp_w, v_ln_ple_w, v_w_up, v_w_down, v_w_ple_proj, v_w_ple_gate, v_ln_final_w, v_ev_w_in, v_ev_w_out, v_dn_conv_w, v_dn_a_log, v_dn_dt_bias, v_dn_norm_w, v_lru_conv_w, v_lru_conv_b, v_lru_wa, v_lru_ba, v_lru_wx, v_lru_bx, v_lru_lambda, v_od_w_in, v_od_w_out):
    given = dict(x=x, p=p, ln_mix_w=ln_mix_w, ln_mlp_w=ln_mlp_w, ln_ple_w=ln_ple_w, w_up=w_up, w_down=w_down, w_ple_proj=w_ple_proj, w_ple_gate=w_ple_gate, ln_final_w=ln_final_w, ev_w_in=ev_w_in, ev_w_out=ev_w_out, dn_conv_w=dn_conv_w, dn_a_log=dn_a_log, dn_dt_bias=dn_dt_bias, dn_norm_w=dn_norm_w, lru_conv_w=lru_conv_w, lru_conv_b=lru_conv_b, lru_wa=lru_wa, lru_ba=lru_ba, lru_wx=lru_wx, lru_bx=lru_bx, lru_lambda=lru_lambda, od_w_in=od_w_in, od_w_out=od_w_out, loss_target=loss_target, m_ln_mix_w=m_ln_mix_w, m_ln_mlp_w=m_ln_mlp_w, m_ln_ple_w=m_ln_ple_w, m_w_up=m_w_up, m_w_down=m_w_down, m_w_ple_proj=m_w_ple_proj, m_w_ple_gate=m_w_ple_gate, m_ln_final_w=m_ln_final_w, m_ev_w_in=m_ev_w_in, m_ev_w_out=m_ev_w_out, m_dn_conv_w=m_dn_conv_w, m_dn_a_log=m_dn_a_log, m_dn_dt_bias=m_dn_dt_bias, m_dn_norm_w=m_dn_norm_w, m_lru_conv_w=m_lru_conv_w, m_lru_conv_b=m_lru_conv_b, m_lru_wa=m_lru_wa, m_lru_ba=m_lru_ba, m_lru_wx=m_lru_wx, m_lru_bx=m_lru_bx, m_lru_lambda=m_lru_lambda, m_od_w_in=m_od_w_in, m_od_w_out=m_od_w_out, v_ln_mix_w=v_ln_mix_w, v_ln_mlp_w=v_ln_mlp_w, v_ln_ple_w=v_ln_ple_w, v_w_up=v_w_up, v_w_down=v_w_down, v_w_ple_proj=v_w_ple_proj, v_w_ple_gate=v_w_ple_gate, v_ln_final_w=v_ln_final_w, v_ev_w_in=v_ev_w_in, v_ev_w_out=v_ev_w_out, v_dn_conv_w=v_dn_conv_w, v_dn_a_log=v_dn_a_log, v_dn_dt_bias=v_dn_dt_bias, v_dn_norm_w=v_dn_norm_w, v_lru_conv_w=v_lru_conv_w, v_lru_conv_b=v_lru_conv_b, v_lru_wa=v_lru_wa, v_lru_ba=v_lru_ba, v_lru_wx=v_lru_wx, v_lru_bx=v_lru_bx, v_lru_lambda=v_lru_lambda, v_od_w_in=v_od_w_in, v_od_w_out=v_od_w_out)
    weights = {n: given[n] for n in TWIN_WEIGHTS}
    shared = {n: given[n] for n in SHARED_INPUTS}
    per_example = {n: given[n] for n in ['x', 'p']}
    grad_fn = _jax.value_and_grad(_loss, argnums=(0, 1))

    def one_microbatch(ex, loss_target):
        ex = dict(ex)
        diff = ex.pop(TWIN_DIFF_INPUT)
        return grad_fn(weights, diff, {**shared, **ex}, loss_target)

    if N_MICROBATCH == 1:
        loss, (grad_w, grad_x) = one_microbatch(per_example, given["loss_target"])
    else:
        def body(carry, xs):
            loss_sum, grad_sum = carry
            l_k, (gw_k, gx_k) = one_microbatch(xs[0], xs[1])
            with _jax.named_scope("update"):
                return (loss_sum + l_k, _jax.tree.map(_jnp.add, grad_sum, gw_k)), gx_k

        init = (_jnp.zeros((), _jnp.float32), _jax.tree.map(_jnp.zeros_like, weights))
        (loss, grad_w), grad_x = _jax.lax.scan(body, init, (per_example, given["loss_target"]))
    with _jax.named_scope("update"):
        delta_w, new_m, new_v = {}, {}, {}
        for n in TWIN_WEIGHTS:
            delta_w[n], new_m[n], new_v[n] = _adamw(weights[n], grad_w[n], given["m_" + n], given["v_" + n])
    return (loss, grad_x, *[grad_w[n] for n in TWIN_WEIGHTS], *[delta_w[n] for n in TWIN_WEIGHTS],
            *[new_m[n] for n in TWIN_WEIGHTS], *[new_v[n] for n in TWIN_WEIGHTS])
```

```python
import functools

import numpy as np
import jax
import jax.numpy as jnp
from jax import lax
from jax.experimental import pallas as pl
from jax.experimental.pallas import tpu as pltpu

F32 = jnp.float32
BF16 = jnp.bfloat16
N_DEV = 8
LANES = 128
SUBLANES = 8
VMEM_LIMIT = 56 * 1024 * 1024
PACK_COLS = 1024
NORM_EPS = 1e-6
GN_EPS = 1e-5
DN_CHUNK = 64
RET_CHUNK = 64
HEAD = 128
RET_DK = 128
RET_DV = 256
SWA_BLOCK = 128
SWA_BRANCHES = ((128, 1), (512, 4), (2048, 16))
LRU_C = 8.0
CONV_W = 4
ADAM_LR, ADAM_B1, ADAM_B2, ADAM_EPS, ADAM_WD, ADAM_STEP = 0.001, 0.9, 0.999, 1e-08, 0.01, 10
NEG = -1e30
MESH = pl.DeviceIdType.MESH


def _params(sem):
    return pltpu.CompilerParams(dimension_semantics=sem, vmem_limit_bytes=VMEM_LIMIT)


def _tile(n, cap):
    for t in (2048, 1024, 896, 768, 640, 512, 384, 256, 128, 64, 32, 16, 8):
        if t <= cap and n % t == 0:
            return t
    return n


def _bdot(a, b, dims):
    return lax.dot_general(a.astype(BF16), b.astype(BF16), (dims, ((), ())), preferred_element_type=F32)


_NN = ((1,), (0,))
_NT = ((1,), (1,))
_TN = ((0,), (0,))


def _hdot(a, b):
    return lax.dot_general(a, b, (_NN, ((), ())), precision=lax.Precision.HIGHEST, preferred_element_type=F32)


def _sigmoid(x):
    return jax.nn.sigmoid(x)


def _silu(x):
    return x * _sigmoid(x)


def _softplus(x):
    return jnp.maximum(x, 0.0) + jnp.log(1.0 + jnp.exp(-jnp.abs(x)))


def _gelu(x):
    return 0.5 * x * (1.0 + jnp.tanh(0.7978845608028654 * (x + 0.044715 * (x * x * x))))


def _mm_call(a, b, *, ta=False, tb=False, extras=(), epilogue=None, out_dtypes=(F32,), name):
    m, k = (a.shape[1], a.shape[0]) if ta else a.shape
    n = b.shape[0] if tb else b.shape[1]
    ne, no = len(extras), len(out_dtypes)
    tm, tn, tk = _tile(m, 1024), _tile(n, 512 if ne + no > 2 else 1024), _tile(k, 1024)
    nk = k // tk
    dims = ((0,) if ta else (1,), (1,) if tb else (0,))

    def body(*refs):
        a_ref, b_ref = refs[0], refs[1]
        ex = refs[2:2 + ne]
        outs = refs[2 + ne:2 + ne + no]
        acc = refs[-1]
        kk = pl.program_id(2)

        @pl.when(kk == 0)
        def _():
            acc[...] = jnp.zeros_like(acc)

        acc[...] += _bdot(a_ref[...], b_ref[...], dims)

        @pl.when(kk == nk - 1)
        def _():
            res = (acc[...],) if epilogue is None else epilogue(acc[...], *[e[...] for e in ex])
            for o, r in zip(outs, res):
                o[...] = r.astype(o.dtype)

    a_spec = pl.BlockSpec((tk, tm), lambda i, j, kk: (kk, i)) if ta else pl.BlockSpec((tm, tk), lambda i, j, kk: (i, kk))
    b_spec = pl.BlockSpec((tn, tk), lambda i, j, kk: (j, kk)) if tb else pl.BlockSpec((tk, tn), lambda i, j, kk: (kk, j))
    mn_spec = pl.BlockSpec((tm, tn), lambda i, j, kk: (i, j))
    return pl.pallas_call(
        body, name=name, grid=(m // tm, n // tn, nk),
        in_specs=[a_spec, b_spec] + [mn_spec] * ne,
        out_specs=[mn_spec] * no,
        out_shape=[jax.ShapeDtypeStruct((m, n), d) for d in out_dtypes],
        scratch_shapes=[pltpu.VMEM((tm, tn), F32)],
        compiler_params=_params(("parallel", "parallel", "arbitrary")),
    )(a, b, *extras)


def _make_mm(name, out_dtype):
    @jax.custom_vjp
    def op(a, w):
        return _mm_call(a, w, out_dtypes=(out_dtype,), name=name + "_f")[0]

    def fwd(a, w):
        return _mm_call(a, w, out_dtypes=(out_dtype,), name=name + "_f")[0], (a, w)

    def bwd(res, dy):
        a, w = res
        da = _mm_call(dy, w, tb=True, out_dtypes=(a.dtype,), name=name + "_da")[0]
        dw = _mm_call(a, dy, ta=True, out_dtypes=(w.dtype,), name=name + "_dw")[0]
        return da, dw

    op.defvjp(fwd, bwd)
    return op


def _make_mm_res(name):
    def call(a, w, h):
        return _mm_call(a, w, extras=(h,), epilogue=lambda acc, hv: (hv + acc,), out_dtypes=(F32,), name=name + "_f")[0]

    @jax.custom_vjp
    def op(a, w, h):
        return call(a, w, h)

    def fwd(a, w, h):
        return call(a, w, h), (a, w)

    def bwd(res, dy):
        a, w = res
        da = _mm_call(dy, w, tb=True, out_dtypes=(a.dtype,), name=name + "_da")[0]
        dw = _mm_call(a, dy, ta=True, out_dtypes=(w.dtype,), name=name + "_dw")[0]
        return da, dw, dy

    op.defvjp(fwd, bwd)
    return op


def _make_ffn(name):
    def forward(hn, w_up, w_down, h):
        def ep(acc):
            r = jnp.maximum(acc, 0.0)
            return acc, r * r
        u, act = _mm_call(hn, w_up, epilogue=ep, out_dtypes=(BF16, BF16), name=name + "_up")
        out = _mm_call(act, w_down, extras=(h,), epilogue=lambda acc, hv: (hv + acc,), out_dtypes=(F32,), name=name + "_down")[0]
        return out, (hn, w_up, w_down, u, act)

    @jax.custom_vjp
    def op(hn, w_up, w_down, h):
        return forward(hn, w_up, w_down, h)[0]

    def bwd(res, dy):
        hn, w_up, w_down, u, act = res
        d_wdown = _mm_call(act, dy, ta=True, out_dtypes=(w_down.dtype,), name=name + "_dwdown")[0]
        d_u = _mm_call(dy, w_down, tb=True, extras=(u,),
                       epilogue=lambda acc, uv: (acc * (2.0 * jnp.maximum(uv.astype(F32), 0.0)),),
                       out_dtypes=(BF16,), name=name + "_du")[0]
        d_wup = _mm_call(hn, d_u, ta=True, out_dtypes=(w_up.dtype,), name=name + "_dwup")[0]
        d_hn = _mm_call(d_u, w_up, tb=True, out_dtypes=(hn.dtype,), name=name + "_dhn")[0]
        return d_hn, d_wup, d_wdown, dy

    op.defvjp(forward, bwd)
    return op


def _make_ple(name):
    def forward(hn, w_gate, p, w_proj, h):
        pp = _mm_call(p, w_proj, out_dtypes=(F32,), name=name + "_proj")[0]
        out, gp = _mm_call(hn, w_gate, extras=(h, pp),
                           epilogue=lambda acc, hv, ppv: (hv + _sigmoid(acc) * ppv, acc),
                           out_dtypes=(F32, F32), name=name + "_gate")
        return out, (hn, w_gate, p, w_proj, gp, pp)

    @jax.custom_vjp
    def op(hn, w_gate, p, w_proj, h):
        return forward(hn, w_gate, p, w_proj, h)[0]

    def bwd(res, dy):
        hn, w_gate, p, w_proj, gp, pp = res

        def gate_grads(g, dyv, gpv, ppv):
            s = _sigmoid(gpv)
            return (dyv * ppv * s * (1.0 - s)).astype(BF16), (dyv * s).astype(BF16)

        t, d = dy.shape
        d_gp, d_pp = _rowmap_call(gate_grads, t, 1, [dy, gp, pp], [], [(d, BF16), (d, BF16)], name + "_dgate")
        d_wgate = _mm_call(hn, d_gp, ta=True, out_dtypes=(w_gate.dtype,), name=name + "_dwgate")[0]
        d_wproj = _mm_call(p, d_pp, ta=True, out_dtypes=(w_proj.dtype,), name=name + "_dwproj")[0]
        d_hn = _mm_call(d_gp, w_gate, tb=True, out_dtypes=(hn.dtype,), name=name + "_dhn")[0]
        return d_hn, d_wgate, jnp.zeros_like(p), d_wproj, dy

    op.defvjp(forward, bwd)
    return op


def _row_tile(t, widths):
    return _tile(t, max(SUBLANES, (128 * 1024) // max(widths)))


def _rowmap_specs(t, g, rows, bcs, tt):
    row_specs = [pl.BlockSpec((tt, r.shape[1] // g), lambda gg, i: (i, gg)) for r in rows]
    bc_specs = []
    for b, per_group in bcs:
        if per_group:
            bc_specs.append(pl.BlockSpec((b.shape[0], b.shape[1] // g), lambda gg, i: (0, gg)))
        else:
            bc_specs.append(pl.BlockSpec(b.shape, lambda gg, i: (0, 0)))
    return row_specs, bc_specs


def _rowmap_call(fn, t, g, rows, bcs, outs, name):
    widths = [r.shape[1] // g for r in rows] + [c // g for c, _ in outs]
    tt = _row_tile(t, widths)
    nr, nb = len(rows), len(bcs)
    row_specs, bc_specs = _rowmap_specs(t, g, rows, bcs, tt)

    def body(*refs):
        vals = [r[...] for r in refs[:nr + nb]]
        res = fn(pl.program_id(0), *vals)
        for o, r in zip(refs[nr + nb:], res):
            o[...] = r.astype(o.dtype)

    return pl.pallas_call(
        body, name=name, grid=(g, t // tt),
        in_specs=row_specs + bc_specs,
        out_specs=[pl.BlockSpec((tt, c // g), lambda gg, i: (i, gg)) for c, _ in outs],
        out_shape=[jax.ShapeDtypeStruct((t, c), d) for c, d in outs],
        compiler_params=_params(("parallel", "parallel")),
    )(*rows, *[b for b, _ in bcs])


def _rowmap_bwd_call(fn, t, g, rows, bcs, cots, name, add0=None):
    widths = [r.shape[1] // g for r in rows] + [c.shape[1] // g for c in cots]
    tt = _row_tile(t, widths)
    nr, nb, nc = len(rows), len(bcs), len(cots)
    na = 0 if add0 is None else 1
    row_specs, bc_specs = _rowmap_specs(t, g, rows, bcs, tt)
    cot_specs = [pl.BlockSpec((tt, c.shape[1] // g), lambda gg, i: (i, gg)) for c in cots]
    add_specs = [] if add0 is None else [row_specs[0]]
    shared = [not per_group for _, per_group in bcs]

    def body(*refs):
        ins = refs[:nr + nb]
        cot_refs = refs[nr + nb:nr + nb + nc]
        add_refs = refs[nr + nb + nc:nr + nb + nc + na]
        d_rows = refs[nr + nb + nc + na:nr + nb + nc + na + nr]
        d_bcs = refs[nr + nb + nc + na + nr:]
        gg, i = pl.program_id(0), pl.program_id(1)
        vals = [r[...] for r in ins]
        _, vjp = jax.vjp(lambda *v: tuple(fn(gg, *v)), *vals)
        grads = vjp(tuple(c[...] for c in cot_refs))
        for j, (o, gr) in enumerate(zip(d_rows, grads[:nr])):
            if j == 0 and na:
                gr = gr + add_refs[0][...]
            o[...] = gr.astype(o.dtype)
        for o, gr, sh in zip(d_bcs, grads[nr:], shared):
            first = jnp.logical_and(i == 0, gg == 0) if sh else i == 0

            @pl.when(first)
            def _():
                o[...] = jnp.zeros_like(o)

            o[...] += gr.astype(o.dtype)

    res = pl.pallas_call(
        body, name=name, grid=(g, t // tt),
        in_specs=row_specs + bc_specs + cot_specs + add_specs,
        out_specs=row_specs + bc_specs,
        out_shape=[jax.ShapeDtypeStruct(r.shape, r.dtype) for r in rows]
        + [jax.ShapeDtypeStruct(b.shape, F32) for b, _ in bcs],
        compiler_params=_params(("arbitrary", "arbitrary")),
    )(*rows, *[b for b, _ in bcs], *cots, *([] if add0 is None else [add0]))
    return res[:nr], res[nr:]


def _make_rowmap(fn, g, n_rows, per_group, outs, name):
    def call(*args):
        rows, bcs = list(args[:n_rows]), list(zip(args[n_rows:], per_group))
        return tuple(_rowmap_call(fn, rows[0].shape[0], g, rows, bcs, outs, name + "_f"))

    @jax.custom_vjp
    def op(*args):
        return call(*args)

    def fwd(*args):
        return call(*args), args

    def bwd(args, cots):
        rows, bcs = list(args[:n_rows]), list(zip(args[n_rows:], per_group))
        d_rows, d_bcs = _rowmap_bwd_call(fn, rows[0].shape[0], g, rows, bcs, list(cots), name + "_b")
        return tuple(d_rows) + tuple(d.astype(b.dtype) for d, (b, _) in zip(d_bcs, bcs))

    op.defvjp(fwd, bwd)
    return op


def _rms_fn(g, h, w):
    y = h * lax.rsqrt(jnp.mean(h * h, axis=-1, keepdims=True) + NORM_EPS)
    return ((y * w).astype(BF16),)


def _make_norm(name):
    def call(h, w):
        return _rowmap_call(_rms_fn, h.shape[0], 1, [h], [(w, False)], [(h.shape[1], BF16)], name + "_f")[0]

    @jax.custom_vjp
    def op(h, w):
        return h, call(h, w)

    def fwd(h, w):
        return (h, call(h, w)), (h, w)

    def bwd(res, cots):
        h, w = res
        dh_pass, dhn = cots
        d_rows, d_bcs = _rowmap_bwd_call(_rms_fn, h.shape[0], 1, [h], [(w, False)], [dhn], name + "_b", add0=dh_pass)
        return d_rows[0], d_bcs[0]

    op.defvjp(fwd, bwd)
    return op


def _loss_call(h, w, target, name):
    t, d = h.shape
    tt = _row_tile(t, [d])

    def body(h_ref, w_ref, t_ref, dh_ref, dw_ref, loss_ref):
        i = pl.program_id(0)
        tgt = t_ref[...]

        def lf(hv, wv):
            y = hv * lax.rsqrt(jnp.mean(hv * hv, axis=-1, keepdims=True) + NORM_EPS) * wv
            err = y - tgt
            return 0.5 * jnp.sum(jnp.mean(err * err, axis=-1, keepdims=True))

        lv, (dh, dw) = jax.value_and_grad(lf, argnums=(0, 1))(h_ref[...], w_ref[...])
        dh_ref[...] = dh

        @pl.when(i == 0)
        def _():
            dw_ref[...] = jnp.zeros_like(dw_ref)
            loss_ref[...] = jnp.zeros_like(loss_ref)

        dw_ref[...] += dw
        loss_ref[...] += jnp.full(loss_ref.shape, lv, F32)

    row = pl.BlockSpec((tt, d), lambda i: (i, 0))
    return pl.pallas_call(
        body, name=name, grid=(t // tt,),
        in_specs=[row, pl.BlockSpec((1, d), lambda i: (0, 0)), row],
        out_specs=[row, pl.BlockSpec((1, d), lambda i: (0, 0)), pl.BlockSpec((SUBLANES, LANES), lambda i: (0, 0))],
        out_shape=[jax.ShapeDtypeStruct((t, d), F32), jax.ShapeDtypeStruct((1, d), F32),
                   jax.ShapeDtypeStruct((SUBLANES, LANES), F32)],
        compiler_params=_params(("arbitrary",)),
    )(h, w, target)


def _make_loss(name):
    @jax.custom_vjp
    def op(h, w, target):
        return _loss_call(h, w, target, name)[2][0, 0]

    def fwd(h, w, target):
        dh, dw, lv = _loss_call(h, w, target, name)
        return lv[0, 0], (dh, dw, target)

    def bwd(res, ct):
        dh, dw, target = res
        return dh * ct, dw * ct, jnp.zeros_like(target)

    op.defvjp(fwd, bwd)
    return op


def _shift_down(cur, halo, s, first):
    if s == 0:
        return cur
    r = pltpu.roll(cur, s, 0)
    p = jnp.where(first, 0.0, pltpu.roll(halo, s, 0))
    rows = lax.broadcasted_iota(jnp.int32, p.shape, 0)
    head = jnp.where(rows < s, p, r[:SUBLANES])
    return jnp.concatenate([head, r[SUBLANES:]], axis=0)


def _shift_up(cur, halo, s, last):
    if s == 0:
        return cur
    n = cur.shape[0]
    r = pltpu.roll(cur, n - s, 0)
    p = jnp.where(last, 0.0, pltpu.roll(halo, SUBLANES - s, 0))
    rows = lax.broadcasted_iota(jnp.int32, p.shape, 0)
    tail = jnp.where(rows >= SUBLANES - s, p, r[n - SUBLANES:])
    return jnp.concatenate([r[:n - SUBLANES], tail], axis=0)


def _conv_specs(t, c):
    tt, cw = _tile(t, 512), _tile(c, 512)
    per = tt // SUBLANES
    nblk = t // SUBLANES
    cur = pl.BlockSpec((tt, cw), lambda j, i: (i, j))
    prev = pl.BlockSpec((SUBLANES, cw), lambda j, i: (jnp.maximum(i * per - 1, 0), j))
    nxt = pl.BlockSpec((SUBLANES, cw), lambda j, i: (jnp.minimum((i + 1) * per, nblk - 1), j))
    wsp = pl.BlockSpec((CONV_W, cw), lambda j, i: (0, j))
    bsp = pl.BlockSpec((1, cw), lambda j, i: (0, j))
    return tt, cw, cur, prev, nxt, wsp, bsp


def _conv_call(x, w, b, name):
    t, c = x.shape
    tt, cw, cur, prev, nxt, wsp, bsp = _conv_specs(t, c)

    def body(x_ref, p_ref, w_ref, b_ref, y_ref):
        first = pl.program_id(1) == 0
        xv, pv = x_ref[...], p_ref[...]
        y = jnp.zeros_like(xv) + b_ref[...]
        for j in range(CONV_W):
            y = y + w_ref[j:j + 1, :] * _shift_down(xv, pv, CONV_W - 1 - j, first)
        y_ref[...] = y

    return pl.pallas_call(
        body, name=name, grid=(c // cw, t // tt),
        in_specs=[cur, prev, wsp, bsp], out_specs=cur,
        out_shape=jax.ShapeDtypeStruct((t, c), F32),
        compiler_params=_params(("parallel", "parallel")),
    )(x, x, w, b)


def _conv_bwd_call(x, w, dy, name):
    t, c = x.shape
    tt, cw, cur, prev, nxt, wsp, bsp = _conv_specs(t, c)
    nt = t // tt

    def body(x_ref, p_ref, w_ref, dy_ref, n_ref, dx_ref, dw_ref, db_ref):
        i = pl.program_id(1)
        first, last = i == 0, i == nt - 1
        xv, pv, dyv, nv = x_ref[...], p_ref[...], dy_ref[...], n_ref[...]

        @pl.when(first)
        def _():
            dw_ref[...] = jnp.zeros_like(dw_ref)
            db_ref[...] = jnp.zeros_like(db_ref)

        dx = jnp.zeros_like(xv)
        for j in range(CONV_W):
            s = CONV_W - 1 - j
            dx = dx + w_ref[j:j + 1, :] * _shift_up(dyv, nv, s, last)
            dw_ref[j:j + 1, :] += jnp.sum(dyv * _shift_down(xv, pv, s, first), axis=0, keepdims=True)
        dx_ref[...] = dx
        db_ref[...] += jnp.sum(dyv, axis=0, keepdims=True)

    return pl.pallas_call(
        body, name=name, grid=(c // cw, nt),
        in_specs=[cur, prev, wsp, cur, nxt], out_specs=[cur, wsp, bsp],
        out_shape=[jax.ShapeDtypeStruct((t, c), F32), jax.ShapeDtypeStruct((CONV_W, c), F32),
                   jax.ShapeDtypeStruct((1, c), F32)],
        compiler_params=_params(("arbitrary", "arbitrary")),
    )(x, x, w, dy, dy)


def _make_conv(name):
    @jax.custom_vjp
    def op(x, w, b):
        return _conv_call(x, w, b, name + "_f")

    def fwd(x, w, b):
        return _conv_call(x, w, b, name + "_f"), (x, w)

    def bwd(res, dy):
        x, w = res
        return tuple(_conv_bwd_call(x, w, dy, name + "_b"))

    op.defvjp(fwd, bwd)
    return op


def _lru_call(a, u, name):
    t, nb, ln = a.shape
    tt = _tile(t, 1024)
    blk = pl.BlockSpec((tt, nb, ln), lambda i: (i, 0, 0))

    def body(a_ref, u_ref, h_ref, carry):
        @pl.when(pl.program_id(0) == 0)
        def _():
            carry[...] = jnp.zeros_like(carry)

        def step(k, h):
            h = a_ref[k] * h + u_ref[k]
            h_ref[k] = h
            return h

        carry[...] = lax.fori_loop(0, tt, step, carry[...], unroll=8)

    return pl.pallas_call(
        body, name=name, grid=(t // tt,), in_specs=[blk, blk], out_specs=blk,
        out_shape=jax.ShapeDtypeStruct(a.shape, F32), scratch_shapes=[pltpu.VMEM((nb, ln), F32)],
        compiler_params=_params(("arbitrary",)),
    )(a, u)


def _lru_bwd_call(a, hs, dy, name):
    t, nb, ln = a.shape
    tt = _tile(t, 1024)
    nt = t // tt
    blk = pl.BlockSpec((tt, nb, ln), lambda i: (nt - 1 - i, 0, 0))
    prev = pl.BlockSpec((1, nb, ln), lambda i: (jnp.maximum((nt - 1 - i) * tt - 1, 0), 0, 0))

    def body(a_ref, h_ref, hp_ref, dy_ref, da_ref, du_ref, carry):
        i = pl.program_id(0)

        @pl.when(i == 0)
        def _():
            carry[...] = jnp.zeros_like(carry)

        h_before = jnp.where(i == nt - 1, 0.0, hp_ref[0])

        def step(k, c):
            r = tt - 1 - k
            dh = dy_ref[r] + c
            du_ref[r] = dh
            da_ref[r] = dh * h_ref[jnp.maximum(r - 1, 0)]
            return a_ref[r] * dh

        carry[...] = lax.fori_loop(0, tt, step, carry[...], unroll=8)
        da_ref[0] = du_ref[0] * h_before

    return pl.pallas_call(
        body, name=name, grid=(nt,), in_specs=[blk, blk, prev, blk], out_specs=[blk, blk],
        out_shape=[jax.ShapeDtypeStruct(a.shape, F32), jax.ShapeDtypeStruct(a.shape, F32)],
        scratch_shapes=[pltpu.VMEM((nb, ln), F32)],
        compiler_params=_params(("arbitrary",)),
    )(a, hs, hs, dy)


def _make_lru(name):
    @jax.custom_vjp
    def op(a, u):
        return _lru_call(a, u, name + "_f")

    def fwd(a, u):
        hs = _lru_call(a, u, name + "_f")
        return hs, (a, hs)

    def bwd(res, dy):
        a, hs = res
        return tuple(_lru_bwd_call(a, hs, dy, name + "_b"))

    op.defvjp(fwd, bwd)
    return op


def _scan_specs(ins, const, heads, hp, chunk, rev_n):
    def tmap(n_of):
        return lambda hg, n: (n_of(n), hg)
    n_of = (lambda n: rev_n - 1 - n) if rev_n else (lambda n: n)
    in_specs = [pl.BlockSpec((chunk, hp * (x.shape[1] // heads)), tmap(n_of)) for x in ins]
    c_spec = pl.BlockSpec((1, hp * (const.shape[1] // heads)), lambda hg, n: (0, hg))
    return in_specs, c_spec, n_of


def _scan_call(chunk_fn, ins, const, heads, hp, chunk, state_shape, out_width, name):
    t = ins[0].shape[0]
    nc = t // chunk
    ni = len(ins)
    in_specs, c_spec, _ = _scan_specs(ins, const, heads, hp, chunk, 0)
    ws = [x.shape[1] // heads for x in ins]
    cw = const.shape[1] // heads
    dk, dv = state_shape

    def body(*refs):
        in_refs, c_ref, o_ref, s_ref, state = refs[:ni], refs[ni], refs[ni + 1], refs[ni + 2], refs[ni + 3]

        @pl.when(pl.program_id(1) == 0)
        def _():
            state[...] = jnp.zeros_like(state)

        for k in range(hp):
            vals = [r[:, k * w:(k + 1) * w] for r, w in zip(in_refs, ws)]
            s0 = state[k]
            s_ref[0, k] = s0
            o, s1 = chunk_fn(*vals, c_ref[:, k * cw:(k + 1) * cw], s0)
            o_ref[:, k * out_width:(k + 1) * out_width] = o
            state[k] = s1

    return pl.pallas_call(
        body, name=name, grid=(heads // hp, nc),
        in_specs=in_specs + [c_spec],
        out_specs=[pl.BlockSpec((chunk, hp * out_width), lambda hg, n: (n, hg)),
                   pl.BlockSpec((1, hp, dk, dv), lambda hg, n: (n, hg, 0, 0))],
        out_shape=[jax.ShapeDtypeStruct((t, heads * out_width), F32),
                   jax.ShapeDtypeStruct((nc, heads, dk, dv), F32)],
        scratch_shapes=[pltpu.VMEM((hp, dk, dv), F32)],
        compiler_params=_params(("parallel", "arbitrary")),
    )(*ins, const)


def _scan_bwd_call(chunk_fn, ins, const, states, d_out, heads, hp, chunk, state_shape, out_width, name):
    t = ins[0].shape[0]
    nc = t // chunk
    ni = len(ins)
    in_specs, c_spec, n_of = _scan_specs(ins, const, heads, hp, chunk, nc)
    ws = [x.shape[1] // heads for x in ins]
    cw = const.shape[1] // heads
    dk, dv = state_shape

    def body(*refs):
        in_refs, c_ref, s_ref, do_ref = refs[:ni], refs[ni], refs[ni + 1], refs[ni + 2]
        d_refs, dstate = refs[ni + 3:ni + 3 + ni], refs[-1]

        @pl.when(pl.program_id(1) == 0)
        def _():
            dstate[...] = jnp.zeros_like(dstate)

        for k in range(hp):
            vals = [r[:, k * w:(k + 1) * w] for r, w in zip(in_refs, ws)]
            cv = c_ref[:, k * cw:(k + 1) * cw]
            _, vjp = jax.vjp(lambda *v: chunk_fn(*v[:-1], cv, v[-1]), *vals, s_ref[0, k])
            grads = vjp((do_ref[:, k * out_width:(k + 1) * out_width], dstate[k]))
            for r, w, gr in zip(d_refs, ws, grads[:-1]):
                r[:, k * w:(k + 1) * w] = gr
            dstate[k] = grads[-1]

    return pl.pallas_call(
        body, name=name, grid=(heads // hp, nc),
        in_specs=in_specs + [c_spec,
                             pl.BlockSpec((1, hp, dk, dv), lambda hg, n: (n_of(n), hg, 0, 0)),
                             pl.BlockSpec((chunk, hp * out_width), lambda hg, n: (n_of(n), hg))],
        out_specs=in_specs,
        out_shape=[jax.ShapeDtypeStruct(x.shape, F32) for x in ins],
        scratch_shapes=[pltpu.VMEM((hp, dk, dv), F32)],
        compiler_params=_params(("parallel", "arbitrary")),
    )(*ins, const, states, d_out)


def _make_scan(chunk_fn, const, heads, hp, chunk, state_shape, out_width, name):
    def call(*ins):
        return _scan_call(chunk_fn, list(ins), const, heads, hp, chunk, state_shape, out_width, name + "_f")

    @jax.custom_vjp
    def op(*ins):
        return call(*ins)[0]

    def fwd(*ins):
        o, states = call(*ins)
        return o, (ins, states)

    def bwd(res, d_out):
        ins, states = res
        return tuple(_scan_bwd_call(chunk_fn, list(ins), const, states, d_out, heads, hp, chunk, state_shape,
                                    out_width, name + "_b"))

    op.defvjp(fwd, bwd)
    return op


def _tri(c):
    ri = lax.broadcasted_iota(jnp.int32, (c, c), 0)
    ci = lax.broadcasted_iota(jnp.int32, (c, c), 1)
    return ri, ci


def _dn_chunk(q, k, v, gb, bb, const, s):
    del const
    c = q.shape[0]
    ri, ci = _tri(c)
    causal, strict = ri >= ci, ri > ci
    gc_b = _hdot(causal.astype(F32), gb)
    gcol = jnp.mean(gc_b, axis=1, keepdims=True)
    grow = jnp.mean(gc_b.T, axis=0, keepdims=True)
    bcol = jnp.mean(bb, axis=1, keepdims=True)
    decay = jnp.where(causal, jnp.exp(jnp.where(causal, gcol - grow, 0.0)), 0.0)
    kb = k * bcol
    x = -jnp.where(strict, _bdot(kb, k, _NT) * decay, 0.0)
    inv = jnp.where(ri == ci, 1.0, 0.0) + x
    xp = x
    for _ in range(max(1, int(np.ceil(np.log2(c))) - 1)):
        xp = _hdot(xp, xp)
        inv = inv + _hdot(inv, xp)
    eg = jnp.exp(gcol)
    u = _hdot(inv, v * bcol)
    w = _hdot(inv, kb * eg)
    qk = _bdot(q, k, _NT) * decay
    g_last = jnp.sum(jnp.mean(gb, axis=1, keepdims=True), axis=0, keepdims=True)
    k_dec = k * jnp.exp(g_last - gcol)
    v_new = u - _bdot(w, s, _NN)
    o = _bdot(q * eg, s, _NN) + _bdot(qk, v_new, _NN)
    s_new = s * jnp.exp(g_last) + _bdot(k_dec, v_new, _TN)
    return o, s_new


def _ret_chunk(q, k, v, lg_b, s):
    c = q.shape[0]
    lg = jnp.mean(lg_b, axis=1, keepdims=True)
    ri, ci = _tri(c)
    rel = (ri - ci).astype(F32)
    dmask = jnp.where(rel >= 0, jnp.exp(jnp.maximum(rel, 0.0) * lg), 0.0)
    idx = lax.broadcasted_iota(jnp.int32, (c, 1), 0).astype(F32)
    ks = k * (RET_DK ** -0.5)
    intra = _bdot(_bdot(q, ks, _NT) * dmask, v, _NN)
    q_dec = q * jnp.exp((idx + 1.0) * lg)
    k_dec = ks * jnp.exp((c - 1.0 - idx) * lg)
    o = intra + _bdot(q_dec, s, _NN)
    s_new = s * jnp.exp(c * lg) + _bdot(k_dec, v, _TN)
    return o, s_new


def _attn_block(q, kp, kc, vp, vc, slope_b, first):
    b = q.shape[0]
    ri, ci = _tri(b)
    rel_c = ri - ci
    rel_p = rel_c + b
    qs = q * (HEAD ** -0.5)
    s_c = _bdot(qs, kc, _NT) - slope_b * rel_c.astype(F32)
    s_p = _bdot(qs, kp, _NT) - slope_b * rel_p.astype(F32)
    s_c = jnp.where(rel_c >= 0, s_c, NEG)
    s_p = jnp.where(jnp.logical_and(rel_p <= b, jnp.logical_not(first)), s_p, NEG)
    mx = lax.stop_gradient(jnp.maximum(jnp.max(s_c, axis=1, keepdims=True), jnp.max(s_p, axis=1, keepdims=True)))
    p_c, p_p = jnp.exp(s_c - mx), jnp.exp(s_p - mx)
    den = jnp.sum(p_c, axis=1, keepdims=True) + jnp.sum(p_p, axis=1, keepdims=True)
    o = (_bdot(p_c, vc, _NN) + _bdot(p_p, vp, _NN)) / den
    lse = jnp.broadcast_to(mx + jnp.log(den), o.shape)
    return o, lse


def _attn_call(q, k, v, slopes, name):
    l, cn = q.shape
    b = SWA_BLOCK
    cur = pl.BlockSpec((b, HEAD), lambda j, n: (n, j))
    prev = pl.BlockSpec((b, HEAD), lambda j, n: (jnp.maximum(n - 1, 0), j))
    ssp = pl.BlockSpec((1, HEAD), lambda j, n: (0, j))

    def body(q_ref, kp_ref, kc_ref, vp_ref, vc_ref, s_ref, o_ref, l_ref):
        o, lse = _attn_block(q_ref[...], kp_ref[...], kc_ref[...], vp_ref[...], vc_ref[...], s_ref[...],
                             pl.program_id(1) == 0)
        o_ref[...] = o
        l_ref[...] = lse

    return pl.pallas_call(
        body, name=name, grid=(cn // HEAD, l // b),
        in_specs=[cur, prev, cur, prev, cur, ssp], out_specs=[cur, cur],
        out_shape=[jax.ShapeDtypeStruct((l, cn), F32), jax.ShapeDtypeStruct((l, cn), F32)],
        compiler_params=_params(("parallel", "parallel")),
    )(q, k, k, v, v, slopes)


def _attn_bwd_call(q, k, v, slopes, d_o, d_l, name):
    l, cn = q.shape
    b = SWA_BLOCK
    nb = l // b
    cur = pl.BlockSpec((b, HEAD), lambda j, n: (jnp.minimum(n, nb - 1), j))
    prev = pl.BlockSpec((b, HEAD), lambda j, n: (jnp.clip(n - 1, 0, nb - 1), j))
    ssp = pl.BlockSpec((1, HEAD), lambda j, n: (0, j))

    def body(q_ref, kp_ref, kc_ref, vp_ref, vc_ref, s_ref, do_ref, dl_ref, dq_ref, dk_ref, dv_ref, ck, cv):
        n = pl.program_id(1)

        @pl.when(n == 0)
        def _():
            ck[...] = jnp.zeros_like(ck)
            cv[...] = jnp.zeros_like(cv)

        @pl.when(n < nb)
        def _():
            sv = s_ref[...]
            first = n == 0
            _, vjp = jax.vjp(lambda *a: _attn_block(*a, sv, first),
                             q_ref[...], kp_ref[...], kc_ref[...], vp_ref[...], vc_ref[...])
            dq, dkp, dkc, dvp, dvc = vjp((do_ref[...], dl_ref[...]))
            dq_ref[...] = dq
            dk_ref[...] = ck[...] + dkp
            dv_ref[...] = cv[...] + dvp
            ck[...] = dkc
            cv[...] = dvc

        @pl.when(n == nb)
        def _():
            dk_ref[...] = ck[...]
            dv_ref[...] = cv[...]

    return pl.pallas_call(
        body, name=name, grid=(cn // HEAD, nb + 1),
        in_specs=[cur, prev, cur, prev, cur, ssp, cur, cur], out_specs=[cur, prev, prev],
        out_shape=[jax.ShapeDtypeStruct((l, cn), F32)] * 3,
        scratch_shapes=[pltpu.VMEM((b, HEAD), F32), pltpu.VMEM((b, HEAD), F32)],
        compiler_params=_params(("parallel", "arbitrary")),
    )(q, k, k, v, v, slopes, d_o, d_l)


def _make_attn(slopes, name):
    @jax.custom_vjp
    def op(q, k, v):
        return tuple(_attn_call(q, k, v, slopes, name + "_f"))

    def fwd(q, k, v):
        return tuple(_attn_call(q, k, v, slopes, name + "_f")), (q, k, v)

    def bwd(res, cots):
        q, k, v = res
        return tuple(_attn_bwd_call(q, k, v, slopes, cots[0], cots[1], name + "_b"))

    op.defvjp(fwd, bwd)
    return op


def _dn_pre_fn(g, cq, ck, cv):
    sq, sk, sv = _silu(cq), _silu(ck), _silu(cv)
    qn = sq * lax.rsqrt(jnp.sum(sq * sq, axis=-1, keepdims=True) + 1e-6) * (HEAD ** -0.5)
    kn = sk * lax.rsqrt(jnp.sum(sk * sk, axis=-1, keepdims=True) + 1e-6)
    return qn, kn, sv


def _make_dn_gates_fn(heads):
    def fn(g, ba, a_log, dt_bias):
        lane = lax.broadcasted_iota(jnp.int32, ba.shape, 1)
        lane1 = lax.broadcasted_iota(jnp.int32, a_log.shape, 1)
        betas, gs = [], []
        for h in range(heads):
            b_raw = jnp.sum(jnp.where(lane == h, ba, 0.0), axis=1, keepdims=True)
            a_raw = jnp.sum(jnp.where(lane == heads + h, ba, 0.0), axis=1, keepdims=True)
            al = jnp.sum(jnp.where(lane1 == h, a_log, 0.0), axis=1, keepdims=True)
            dt = jnp.sum(jnp.where(lane1 == h, dt_bias, 0.0), axis=1, keepdims=True)
            beta = _sigmoid(b_raw)
            gl = -jnp.exp(al) * _softplus(a_raw + dt)
            betas.append(jnp.broadcast_to(beta, ba.shape))
            gs.append(jnp.broadcast_to(gl, ba.shape))
        return jnp.concatenate(betas, axis=1), jnp.concatenate(gs, axis=1)
    return fn


def _dn_post_fn(g, o, z, w):
    y = o * lax.rsqrt(jnp.mean(o * o, axis=-1, keepdims=True) + NORM_EPS) * w
    return ((y * _silu(z)).astype(BF16),)


def _lru_pre_fn(g, xc, wa, wx, ba, bx, lam):
    r = _sigmoid(_bdot(xc, wa, _NN) + ba)
    i = _sigmoid(_bdot(xc, wx, _NN) + bx)
    log_a = -LRU_C * r * _softplus(-lam)
    a = jnp.exp(log_a)
    u = jnp.sqrt(1.0 - jnp.exp(2.0 * log_a)) * (i * xc)
    return a, u


def _lru_post_fn(g, hs, yr):
    return ((hs * _gelu(yr)).astype(BF16),)


def _merge_fn(g, o1, o2, o3, l1, l2, l3):
    m = lax.stop_gradient(jnp.maximum(jnp.maximum(l1, l2), l3))
    e1, e2, e3 = jnp.exp(l1 - m), jnp.exp(l2 - m), jnp.exp(l3 - m)
    return (((e1 * o1 + e2 * o2 + e3 * o3) / (e1 + e2 + e3)).astype(BF16),)


def _ret_post_fn(g, o, gate):
    mu = jnp.mean(o, axis=-1, keepdims=True)
    xc = o - mu
    y = xc * lax.rsqrt(jnp.mean(xc * xc, axis=-1, keepdims=True) + GN_EPS)
    return ((y * _silu(gate)).astype(BF16),)


def _pad_lanes(v):
    return jnp.pad(v, (0, LANES - v.shape[0]))[None, :]


def _even_layout(half):
    heads = half // HEAD
    qkv = 3 * half
    segs = [(0, qkv, qkv), (qkv, half, half), (qkv + half, 2 * heads, LANES),
            (qkv + half + 2 * heads, half, half), (qkv + 2 * half + 2 * heads, half, half)]
    return segs


def _pad_ev_w_in(w, half):
    parts = []
    for start, width, padded in _even_layout(half):
        part = w[:, start:start + width]
        if padded != width:
            part = jnp.pad(part, ((0, 0), (0, padded - width)))
        parts.append(part)
    return jnp.concatenate(parts, axis=1)


def _even_mixer(hn, h, lw):
    t, d = h.shape
    half = d // 2
    heads = half // HEAD
    hp = 2 if heads % 2 == 0 else 1
    proj = _make_mm("ev_in", F32)(hn, _pad_ev_w_in(lw["w_in"], half))
    o0 = 0
    segs = []
    for _, _, padded in _even_layout(half):
        segs.append(proj[:, o0:o0 + padded])
        o0 += padded
    qkv, z, ba, xr, yr = segs
    c = _make_conv("dn_conv")(qkv, lw["dn_conv_w"], jnp.zeros((1, 3 * half), F32))
    q, k, v = _make_rowmap(_dn_pre_fn, heads, 3, [], [(half, F32)] * 3, "dn_pre")(
        c[:, :half], c[:, half:2 * half], c[:, 2 * half:])
    beta_b, g_b = _make_rowmap(_make_dn_gates_fn(heads), 1, 1, [False, False], [(half, F32)] * 2, "dn_gates")(
        ba, _pad_lanes(lw["dn_a_log"]), _pad_lanes(lw["dn_dt_bias"]))
    o = _make_scan(_dn_chunk, jnp.zeros((1, heads * LANES), F32), heads, hp, DN_CHUNK, (HEAD, HEAD), HEAD, "dn_core")(
        q, k, v, g_b, beta_b)
    ya = _make_rowmap(_dn_post_fn, heads, 2, [False], [(half, BF16)], "dn_post")(o, z, lw["dn_norm_w"][None, :])[0]
    nblk = lw["lru_wa"].shape[0]
    xc = _make_conv("lru_conv")(xr, lw["lru_conv_w"], lw["lru_conv_b"][None, :])
    wa = lw["lru_wa"].transpose(1, 0, 2).reshape(HEAD, nblk * HEAD)
    wx = lw["lru_wx"].transpose(1, 0, 2).reshape(HEAD, nblk * HEAD)
    a, u = _make_rowmap(_lru_pre_fn, nblk, 1, [True] * 5, [(half, F32)] * 2, "lru_pre")(
        xc, wa, wx, lw["lru_ba"][None, :], lw["lru_bx"][None, :], lw["lru_lambda"][None, :])
    hs = _make_lru("lru_scan")(a.reshape(t, nblk, HEAD), u.reshape(t, nblk, HEAD)).reshape(t, half)
    yb = _make_rowmap(_lru_post_fn, nblk, 2, [], [(half, BF16)], "lru_post")(hs, yr)[0]
    return _make_mm_res("ev_out")(jnp.concatenate([ya, yb], axis=1), lw["w_out"], h)


def _odd_mixer(hn, h, lw):
    t, d = h.shape
    half = d // 2
    heads = half // HEAD
    rheads = half // RET_DV
    rqk = rheads * RET_DK
    proj = _make_mm("od_in", F32)(hn, lw["w_in"])
    cq, ck, cv = proj[:, :half], proj[:, half:2 * half], proj[:, 2 * half:3 * half]
    o1 = 3 * half
    rq, rk = proj[:, o1:o1 + rqk], proj[:, o1 + rqk:o1 + 2 * rqk]
    rv, rg = proj[:, o1 + 2 * rqk:o1 + 2 * rqk + half], proj[:, o1 + 2 * rqk + half:]
    slopes = np.exp2(-8.0 * np.arange(1, heads + 1, dtype=np.float64) / heads)
    outs, lses = [], []
    for window, dil in SWA_BRANCHES:
        assert window // dil == SWA_BLOCK and (t // dil) % SWA_BLOCK == 0
        sl = jnp.asarray(np.tile(np.repeat(slopes * dil, HEAD), dil)[None, :], F32)
        shape = (t // dil, dil * half)
        o_i, l_i = _make_attn(sl, "swa_d%d" % dil)(cq.reshape(shape), ck.reshape(shape), cv.reshape(shape))
        outs.append(o_i.reshape(t, half))
        lses.append(l_i.reshape(t, half))
    yc = _make_rowmap(_merge_fn, heads, 6, [], [(half, BF16)], "swa_merge")(*outs, *lses)[0]
    lg = np.log1p(-np.exp2(-5.0 - np.arange(rheads, dtype=np.float64)))
    lg_b = jnp.asarray(np.repeat(lg, LANES)[None, :], F32)
    hp = 2 if rheads % 2 == 0 else 1
    o_r = _make_scan(_ret_chunk, lg_b, rheads, hp, RET_CHUNK, (RET_DK, RET_DV), RET_DV, "ret_core")(rq, rk, rv)
    yd = _make_rowmap(_ret_post_fn, rheads, 2, [], [(half, BF16)], "ret_post")(o_r, rg)[0]
    return _make_mm_res("od_out")(jnp.concatenate([yc, yd], axis=1), lw["w_out"], h)


def _local_loss(big, small, x, p, target):
    depth = big["w_up"].shape[0]
    h = x
    for i in range(depth):
        j = i // 2
        h, hn = _make_norm("ln_mix")(h, small["ln_mix_w"][i][None, :])
        if i % 2 == 0:
            lw = {"w_in": big["ev_w_in"][j], "w_out": big["ev_w_out"][j]}
            for nm in ("dn_conv_w", "dn_a_log", "dn_dt_bias", "dn_norm_w", "lru_conv_w", "lru_conv_b", "lru_wa",
                       "lru_ba", "lru_wx", "lru_bx", "lru_lambda"):
                lw[nm] = small[nm][j]
            h = _even_mixer(hn, h, lw)
        else:
            h = _odd_mixer(hn, h, {"w_in": big["od_w_in"][j], "w_out": big["od_w_out"][j]})
        h, hn = _make_norm("ln_mlp")(h, small["ln_mlp_w"][i][None, :])
        h = _make_ffn("ffn")(hn, big["w_up"][i], big["w_down"][i], h)
        h, hn = _make_norm("ln_ple")(h, small["ln_ple_w"][i][None, :])
        h = _make_ple("ple")(hn, big["w_ple_gate"][i], p[i], big["w_ple_proj"][i], h)
    return _make_loss("loss_head")(h, small["ln_final_w"][None, :], target)


def _all_gather(x, name):
    r, c = x.shape

    def body(x_ref, out_ref, send_sems, recv_sems, local_sem):
        mx, my, mc = lax.axis_index("x"), lax.axis_index("y"), lax.axis_index("c")
        me, sibling = (mx, my, mc), (mx, my, 1 - mc)
        chips = [(1 - mx, my), (mx, 1 - my), (1 - mx, 1 - my)]

        def slot(px, py, pc):
            return out_ref.at[4 * px + 2 * py + pc]

        def copy(k, block, to, src=None):
            return pltpu.make_async_remote_copy(
                src_ref=slot(*block) if src is None else src, dst_ref=slot(*block),
                send_sem=send_sems.at[k], recv_sem=recv_sems.at[k], device_id=to, device_id_type=MESH)

        mine = pltpu.make_async_copy(x_ref, slot(*me), local_sem)
        mine.start()
        first = [copy(0, me, sibling, src=x_ref)]
        first += [copy(1 + j, me, (*chip, mc), src=x_ref) for j, chip in enumerate(chips)]
        for cp in first:
            cp.start()
        passed = [copy(4 + j, (*chip, mc), sibling) for j, chip in enumerate(chips)]
        for j, chip in enumerate(chips):
            copy(1 + j, (*chip, mc), me).wait_recv()
            passed[j].start()
        copy(0, sibling, me).wait_recv()
        for j, chip in enumerate(chips):
            copy(4 + j, (*chip, 1 - mc), me).wait_recv()
        for cp in first + passed:
            cp.wait_send()
        mine.wait()

    return pl.pallas_call(
        body, name=name,
        out_shape=jax.ShapeDtypeStruct((N_DEV, r, c), x.dtype),
        in_specs=[pl.BlockSpec(memory_space=pl.ANY)],
        out_specs=pl.BlockSpec(memory_space=pl.ANY),
        scratch_shapes=[pltpu.SemaphoreType.DMA((7,)), pltpu.SemaphoreType.DMA((7,)), pltpu.SemaphoreType.DMA],
    )(x)


def _deliver_slots(g, name):
    _, r, c = g.shape

    def body(g_ref, out_ref, send_sems, recv_sems, local_sem):
        mx, my, mc = lax.axis_index("x"), lax.axis_index("y"), lax.axis_index("c")
        me = 4 * mx + 2 * my + mc
        mine = pltpu.make_async_copy(g_ref.at[me], out_ref.at[me], local_sem)
        mine.start()
        copies = []
        for k in range(1, N_DEV):
            fx, fy, fc = (k >> 2) & 1, (k >> 1) & 1, k & 1
            px = mx + fx - 2 * mx * fx
            py = my + fy - 2 * my * fy
            pc = mc + fc - 2 * mc * fc
            cp = pltpu.make_async_remote_copy(
                src_ref=g_ref.at[4 * px + 2 * py + pc], dst_ref=out_ref.at[me],
                send_sem=send_sems.at[k - 1], recv_sem=recv_sems.at[k - 1],
                device_id=(px, py, pc), device_id_type=MESH)
            cp.start()
            copies.append((cp, 4 * px + 2 * py + pc))
        for k, (cp, peer) in enumerate(copies):
            pltpu.make_async_remote_copy(
                src_ref=g_ref.at[peer], dst_ref=out_ref.at[peer], send_sem=send_sems.at[k], recv_sem=recv_sems.at[k],
                device_id=(mx, my, mc), device_id_type=MESH).wait_recv()
        for cp, _ in copies:
            cp.wait_send()
        mine.wait()

    return pl.pallas_call(
        body, name=name,
        out_shape=jax.ShapeDtypeStruct(g.shape, g.dtype),
        in_specs=[pl.BlockSpec(memory_space=pl.ANY)],
        out_specs=pl.BlockSpec(memory_space=pl.ANY),
        scratch_shapes=[pltpu.SemaphoreType.DMA((7,)), pltpu.SemaphoreType.DMA((7,)), pltpu.SemaphoreType.DMA],
    )(g)


def _slot_sum(slots, name):
    ns, r, c = slots.shape
    tr = _tile(r, 256)

    def body(s_ref, o_ref):
        acc = s_ref[0].astype(F32)
        for s in range(1, ns):
            acc = acc + s_ref[s].astype(F32)
        o_ref[...] = acc

    return pl.pallas_call(
        body, name=name, grid=(r // tr,),
        in_specs=[pl.BlockSpec((ns, tr, c), lambda i: (0, i, 0))],
        out_specs=pl.BlockSpec((tr, c), lambda i: (i, 0)),
        out_shape=jax.ShapeDtypeStruct((r, c), F32),
        compiler_params=_params(("parallel",)),
    )(slots)


def _adamw(slots, w, m, v, name):
    ns, r, c = slots.shape
    tr = _tile(r, 256)

    def body(s_ref, w_ref, m_ref, v_ref, g_out, d_out, m_out, v_out):
        g = s_ref[0].astype(F32)
        for s in range(1, ns):
            g = g + s_ref[s].astype(F32)
        mn = ADAM_B1 * m_ref[...] + (1.0 - ADAM_B1) * g
        vn = ADAM_B2 * v_ref[...] + (1.0 - ADAM_B2) * (g * g)
        m_hat = mn / (1.0 - ADAM_B1 ** ADAM_STEP)
        v_hat = vn / (1.0 - ADAM_B2 ** ADAM_STEP)
        g_out[...] = g
        d_out[...] = -ADAM_LR * (m_hat / (jnp.sqrt(v_hat) + ADAM_EPS) + ADAM_WD * w_ref[...])
        m_out[...] = mn
        v_out[...] = vn

    blk = pl.BlockSpec((tr, c), lambda i: (i, 0))
    return pl.pallas_call(
        body, name=name, grid=(r // tr,),
        in_specs=[pl.BlockSpec((ns, tr, c), lambda i: (0, i, 0)), blk, blk, blk],
        out_specs=[blk] * 4,
        out_shape=[jax.ShapeDtypeStruct((r, c), F32)] * 4,
        compiler_params=_params(("parallel",)),
    )(slots, w, m, v)


def _pack(arrays, dtype, row_multiple=SUBLANES):
    flat = jnp.concatenate([a.astype(dtype).reshape(-1) for a in arrays])
    unit = row_multiple * PACK_COLS
    total = -(-flat.shape[0] // unit) * unit
    if total != flat.shape[0]:
        flat = jnp.pad(flat, (0, total - flat.shape[0]))
    return flat.reshape(-1, PACK_COLS)


def _unpack(buf, shapes):
    flat = buf.reshape(-1)
    out, o = [], 0
    for s in shapes:
        n = int(np.prod(s))
        out.append(flat[o:o + n].reshape(s))
        o += n
    return out


BIG = ("w_up", "w_down", "w_ple_proj", "w_ple_gate", "ev_w_in", "ev_w_out", "od_w_in", "od_w_out")
BIG_COL_SHARDED = {"w_up": True, "w_down": False, "w_ple_proj": True, "w_ple_gate": False,
                   "ev_w_in": True, "ev_w_out": False, "od_w_in": True, "od_w_out": False}
SMALL_SHARDED = ("dn_conv_w", "lru_conv_w")
SMALL_REPLICATED = ("ln_mix_w", "ln_mlp_w", "ln_ple_w", "ln_final_w", "dn_a_log", "dn_dt_bias", "dn_norm_w",
                    "lru_conv_b", "lru_wa", "lru_ba", "lru_wx", "lru_bx", "lru_lambda")
WEIGHTS = ("ln_mix_w", "ln_mlp_w", "ln_ple_w", "w_up", "w_down", "w_ple_proj", "w_ple_gate", "ln_final_w",
           "ev_w_in", "ev_w_out", "dn_conv_w", "dn_a_log", "dn_dt_bias", "dn_norm_w", "lru_conv_w", "lru_conv_b",
           "lru_wa", "lru_ba", "lru_wx", "lru_bx", "lru_lambda", "od_w_in", "od_w_out")


def _assemble(slots, col_sharded):
    _, n, rows, cols = slots.shape
    if col_sharded:
        return slots.transpose(1, 2, 0, 3).reshape(n, rows, N_DEV * cols)
    return slots.transpose(1, 0, 2, 3).reshape(n, N_DEV * rows, cols)


def _to_slots(full, col_sharded):
    n = full.shape[0]
    if col_sharded:
        rows, cols = full.shape[1], full.shape[2] // N_DEV
        s = full.reshape(n, rows, N_DEV, cols).transpose(2, 0, 1, 3)
    else:
        rows, cols = full.shape[1] // N_DEV, full.shape[2]
        s = full.reshape(n, N_DEV, rows, cols).transpose(1, 0, 2, 3)
    return s.reshape(N_DEV, -1)


def kernel(x, p, ln_mix_w, ln_mlp_w, ln_ple_w, w_up, w_down, w_ple_proj, w_ple_gate, ln_final_w, ev_w_in, ev_w_out, dn_conv_w, dn_a_log, dn_dt_bias, dn_norm_w, lru_conv_w, lru_conv_b, lru_wa, lru_ba, lru_wx, lru_bx, lru_lambda, od_w_in, od_w_out, loss_target, m_ln_mix_w, m_ln_mlp_w, m_ln_ple_w, m_w_up, m_w_down, m_w_ple_proj, m_w_ple_gate, m_ln_final_w, m_ev_w_in, m_ev_w_out, m_dn_conv_w, m_dn_a_log, m_dn_dt_bias, m_dn_norm_w, m_lru_conv_w, m_lru_conv_b, m_lru_wa, m_lru_ba, m_lru_wx, m_lru_bx, m_lru_lambda, m_od_w_in, m_od_w_out, v_ln_mix_w, v_ln_mlp_w, v_ln_ple_w, v_w_up, v_w_down, v_w_ple_proj, v_w_ple_gate, v_ln_final_w, v_ev_w_in, v_ev_w_out, v_dn_conv_w, v_dn_a_log, v_dn_dt_bias, v_dn_norm_w, v_lru_conv_w, v_lru_conv_b, v_lru_wa, v_lru_ba, v_lru_wx, v_lru_bx, v_lru_lambda, v_od_w_in, v_od_w_out):
    w = dict(ln_mix_w=ln_mix_w, ln_mlp_w=ln_mlp_w, ln_ple_w=ln_ple_w, w_up=w_up, w_down=w_down,
             w_ple_proj=w_ple_proj, w_ple_gate=w_ple_gate, ln_final_w=ln_final_w, ev_w_in=ev_w_in,
             ev_w_out=ev_w_out, dn_conv_w=dn_conv_w, dn_a_log=dn_a_log, dn_dt_bias=dn_dt_bias,
             dn_norm_w=dn_norm_w, lru_conv_w=lru_conv_w, lru_conv_b=lru_conv_b, lru_wa=lru_wa, lru_ba=lru_ba,
             lru_wx=lru_wx, lru_bx=lru_bx, lru_lambda=lru_lambda, od_w_in=od_w_in, od_w_out=od_w_out)
    m = dict(ln_mix_w=m_ln_mix_w, ln_mlp_w=m_ln_mlp_w, ln_ple_w=m_ln_ple_w, w_up=m_w_up, w_down=m_w_down,
             w_ple_proj=m_w_ple_proj, w_ple_gate=m_w_ple_gate, ln_final_w=m_ln_final_w, ev_w_in=m_ev_w_in,
             ev_w_out=m_ev_w_out, dn_conv_w=m_dn_conv_w, dn_a_log=m_dn_a_log, dn_dt_bias=m_dn_dt_bias,
             dn_norm_w=m_dn_norm_w, lru_conv_w=m_lru_conv_w, lru_conv_b=m_lru_conv_b, lru_wa=m_lru_wa,
             lru_ba=m_lru_ba, lru_wx=m_lru_wx, lru_bx=m_lru_bx, lru_lambda=m_lru_lambda, od_w_in=m_od_w_in,
             od_w_out=m_od_w_out)
    v = dict(ln_mix_w=v_ln_mix_w, ln_mlp_w=v_ln_mlp_w, ln_ple_w=v_ln_ple_w, w_up=v_w_up, w_down=v_w_down,
             w_ple_proj=v_w_ple_proj, w_ple_gate=v_w_ple_gate, ln_final_w=v_ln_final_w, ev_w_in=v_ev_w_in,
             ev_w_out=v_ev_w_out, dn_conv_w=v_dn_conv_w, dn_a_log=v_dn_a_log, dn_dt_bias=v_dn_dt_bias,
             dn_norm_w=v_dn_norm_w, lru_conv_w=v_lru_conv_w, lru_conv_b=v_lru_conv_b, lru_wa=v_lru_wa,
             lru_ba=v_lru_ba, lru_wx=v_lru_wx, lru_bx=v_lru_bx, lru_lambda=v_lru_lambda, od_w_in=v_od_w_in,
             od_w_out=v_od_w_out)
    me = 4 * lax.axis_index("x") + 2 * lax.axis_index("y") + lax.axis_index("c")

    big_shapes = [w[n].shape for n in BIG]
    gathered = _all_gather(_pack([w[n] for n in BIG], BF16), "gather_weights")
    per_dev = [_unpack(gathered[s], big_shapes) for s in range(N_DEV)]
    big = {n: _assemble(jnp.stack([per_dev[s][i] for s in range(N_DEV)]), BIG_COL_SHARDED[n])
           for i, n in enumerate(BIG)}
    conv_shapes = [w[n].shape for n in SMALL_SHARDED]
    conv_g = _all_gather(_pack([w[n] for n in SMALL_SHARDED], F32), "gather_conv")
    conv_dev = [_unpack(conv_g[s], conv_shapes) for s in range(N_DEV)]
    small = {n: w[n] for n in SMALL_REPLICATED}
    for i, n in enumerate(SMALL_SHARDED):
        small[n] = jnp.concatenate([conv_dev[s][i] for s in range(N_DEV)], axis=-1)

    loss_local, (g_big, g_small, g_x) = jax.value_and_grad(_local_loss, argnums=(0, 1, 2))(
        big, small, x[0], p[:, 0], loss_target[0])
    loss = lax.psum(loss_local, ("x", "y", "c"))

    slots = jnp.concatenate([_to_slots(g_big[n], BIG_COL_SHARDED[n]) for n in BIG], axis=1)
    pad = (-slots.shape[1]) % (SUBLANES * PACK_COLS)
    if pad:
        slots = jnp.pad(slots, ((0, 0), (0, pad)))
    received = _deliver_slots(slots.reshape(N_DEV, -1, PACK_COLS), "deliver_grads")
    res_big = _adamw(received, _pack([w[n] for n in BIG], F32), _pack([m[n] for n in BIG], F32),
                     _pack([v[n] for n in BIG], F32), "adamw_big")
    out = {}
    for kind, buf in zip(("grad", "delta", "new_m", "new_v"), res_big):
        for n, a in zip(BIG, _unpack(buf, big_shapes)):
            out[kind, n] = a

    small_names = SMALL_REPLICATED + SMALL_SHARDED
    small_shapes = [small[n].shape for n in small_names]
    all_small = _all_gather(_pack([g_small[n] for n in small_names], F32), "gather_small_grads")
    total = dict(zip(small_names, _unpack(_slot_sum(all_small, "sum_small_grads"), small_shapes)))
    for n in SMALL_SHARDED:
        width = w[n].shape[-1]
        total[n] = lax.dynamic_slice_in_dim(total[n], me * width, width, axis=-1)
    own_shapes = [w[n].shape for n in small_names]
    res_small = _adamw(_pack([total[n] for n in small_names], F32)[None], _pack([w[n] for n in small_names], F32),
                       _pack([m[n] for n in small_names], F32), _pack([v[n] for n in small_names], F32), "adamw_small")
    for kind, buf in zip(("grad", "delta", "new_m", "new_v"), res_small):
        for n, a in zip(small_names, _unpack(buf, own_shapes)):
            out[kind, n] = a

    return (loss, g_x[None], *[out["grad", n] for n in WEIGHTS], *[out["delta", n] for n in WEIGHTS],
            *[out["new_m", n] for n in WEIGHTS], *[out["new_v", n] for n in WEIGHTS])
```

```python
import functools

import numpy as np
import jax
import jax.numpy as jnp
from jax import lax
from jax.experimental import pallas as pl
from jax.experimental.pallas import tpu as pltpu

F32 = jnp.float32
BF16 = jnp.bfloat16
N_DEV = 8
LANES = 128
SUBLANES = 8
VMEM_LIMIT = 56 * 1024 * 1024
PACK_COLS = 1024
NORM_EPS = 1e-6
GN_EPS = 1e-5
DN_CHUNK = 64
RET_CHUNK = 64
HEAD = 128
RET_DK = 128
RET_DV = 256
SWA_BLOCK = 128
SWA_BRANCHES = ((128, 1), (512, 4), (2048, 16))
LRU_C = 8.0
CONV_W = 4
ADAM_LR, ADAM_B1, ADAM_B2, ADAM_EPS, ADAM_WD, ADAM_STEP = 0.001, 0.9, 0.999, 1e-08, 0.01, 10
NEG = -1e30
MESH = pl.DeviceIdType.MESH


def _params(sem):
    return pltpu.CompilerParams(dimension_semantics=sem, vmem_limit_bytes=VMEM_LIMIT)


def _tile(n, cap):
    for t in (2048, 1024, 896, 768, 640, 512, 384, 256, 128, 64, 32, 16, 8):
        if t <= cap and n % t == 0:
            return t
    return n


def _bdot(a, b, dims):
    return lax.dot_general(a.astype(BF16), b.astype(BF16), (dims, ((), ())), preferred_element_type=F32)


_NN = ((1,), (0,))
_NT = ((1,), (1,))
_TN = ((0,), (0,))


def _hdot(a, b):
    return lax.dot_general(a, b, (_NN, ((), ())), precision=lax.Precision.HIGHEST, preferred_element_type=F32)


def _sigmoid(x):
    return jax.nn.sigmoid(x)


def _silu(x):
    return x * _sigmoid(x)


def _softplus(x):
    return jnp.maximum(x, 0.0) + jnp.log(1.0 + jnp.exp(-jnp.abs(x)))


def _gelu(x):
    return 0.5 * x * (1.0 + jnp.tanh(0.7978845608028654 * (x + 0.044715 * (x * x * x))))


def _mm_call(a, b, *, ta=False, tb=False, extras=(), epilogue=None, out_dtypes=(F32,), b_slots=False,
             out_slots=False, name):
    m, k = (a.shape[1], a.shape[0]) if ta else a.shape
    ne, no = len(extras), len(out_dtypes)
    cap_n = 512 if ne + no > 2 else 1024
    shard = b.shape[2] if b_slots else None
    if b_slots:
        n = b.shape[1] if tb else N_DEV * shard
    else:
        n = b.shape[0] if tb else b.shape[1]
    if out_slots:
        shard = n // N_DEV
    tm = _tile(m, 1024)
    tn = _tile(shard if (out_slots or (b_slots and not tb)) else n, cap_n)
    tk = _tile(shard if (b_slots and tb) else k, 1024)
    nk = k // tk
    dims = ((0,) if ta else (1,), (1,) if tb else (0,))

    def body(*refs):
        a_ref, b_ref = refs[0], refs[1]
        ex = refs[2:2 + ne]
        outs = refs[2 + ne:2 + ne + no]
        acc = refs[-1]
        kk = pl.program_id(2)

        @pl.when(kk == 0)
        def _():
            acc[...] = jnp.zeros_like(acc)

        acc[...] += _bdot(a_ref[...], b_ref[...], dims)

        @pl.when(kk == nk - 1)
        def _():
            res = (acc[...],) if epilogue is None else epilogue(acc[...], *[e[...] for e in ex])
            for o, r in zip(outs, res):
                o[...] = r.astype(o.dtype)

    a_spec = pl.BlockSpec((tk, tm), lambda i, j, kk: (kk, i)) if ta else pl.BlockSpec((tm, tk), lambda i, j, kk: (i, kk))
    if b_slots and tb:
        per = shard // tk
        b_spec = pl.BlockSpec((None, tn, tk), lambda i, j, kk: (kk // per, j, kk % per))
    elif b_slots:
        per = shard // tn
        b_spec = pl.BlockSpec((None, tk, tn), lambda i, j, kk: (j // per, kk, j % per))
    elif tb:
        b_spec = pl.BlockSpec((tn, tk), lambda i, j, kk: (j, kk))
    else:
        b_spec = pl.BlockSpec((tk, tn), lambda i, j, kk: (kk, j))
    mn_spec = pl.BlockSpec((tm, tn), lambda i, j, kk: (i, j))
    if out_slots:
        per_o = shard // tn
        out_specs = [pl.BlockSpec((None, tm, tn), lambda i, j, kk: (j // per_o, i, j % per_o))]
        out_shape = [jax.ShapeDtypeStruct((N_DEV, m, shard), out_dtypes[0])]
    else:
        out_specs = [mn_spec] * no
        out_shape = [jax.ShapeDtypeStruct((m, n), d) for d in out_dtypes]
    return pl.pallas_call(
        body, name=name, grid=(m // tm, n // tn, nk),
        in_specs=[a_spec, b_spec] + [mn_spec] * ne,
        out_specs=out_specs,
        out_shape=out_shape,
        scratch_shapes=[pltpu.VMEM((tm, tn), F32)],
        compiler_params=_params(("parallel", "parallel", "arbitrary")),
    )(a, b, *extras)


def _make_mm(name, out_dtype, slots=False):
    @jax.custom_vjp
    def op(a, w):
        return _mm_call(a, w, out_dtypes=(out_dtype,), b_slots=slots, name=name + "_f")[0]

    def fwd(a, w):
        return _mm_call(a, w, out_dtypes=(out_dtype,), b_slots=slots, name=name + "_f")[0], (a, w)

    def bwd(res, dy):
        a, w = res
        da = _mm_call(dy, w, tb=True, out_dtypes=(a.dtype,), b_slots=slots, name=name + "_da")[0]
        dw = _mm_call(a, dy, ta=True, out_dtypes=(w.dtype,), out_slots=slots, name=name + "_dw")[0]
        return da, dw

    op.defvjp(fwd, bwd)
    return op


def _make_mm_res(name):
    def call(a, w, h):
        return _mm_call(a, w, extras=(h,), epilogue=lambda acc, hv: (hv + acc,), out_dtypes=(F32,), name=name + "_f")[0]

    @jax.custom_vjp
    def op(a, w, h):
        return call(a, w, h)

    def fwd(a, w, h):
        return call(a, w, h), (a, w)

    def bwd(res, dy):
        a, w = res
        da = _mm_call(dy, w, tb=True, out_dtypes=(a.dtype,), name=name + "_da")[0]
        dw = _mm_call(a, dy, ta=True, out_dtypes=(w.dtype,), name=name + "_dw")[0]
        return da, dw, dy

    op.defvjp(fwd, bwd)
    return op


def _make_ffn(name):
    def forward(hn, w_up, w_down, h):
        def ep(acc):
            r = jnp.maximum(acc, 0.0)
            return acc, r * r
        u, act = _mm_call(hn, w_up, epilogue=ep, out_dtypes=(BF16, BF16), b_slots=True, name=name + "_up")
        out = _mm_call(act, w_down, extras=(h,), epilogue=lambda acc, hv: (hv + acc,), out_dtypes=(F32,), name=name + "_down")[0]
        return out, (hn, w_up, w_down, u, act)

    @jax.custom_vjp
    def op(hn, w_up, w_down, h):
        return forward(hn, w_up, w_down, h)[0]

    def bwd(res, dy):
        hn, w_up, w_down, u, act = res
        d_wdown = _mm_call(act, dy, ta=True, out_dtypes=(w_down.dtype,), name=name + "_dwdown")[0]
        d_u = _mm_call(dy, w_down, tb=True, extras=(u,),
                       epilogue=lambda acc, uv: (acc * (2.0 * jnp.maximum(uv.astype(F32), 0.0)),),
                       out_dtypes=(BF16,), name=name + "_du")[0]
        d_wup = _mm_call(hn, d_u, ta=True, out_dtypes=(w_up.dtype,), out_slots=True, name=name + "_dwup")[0]
        d_hn = _mm_call(d_u, w_up, tb=True, out_dtypes=(hn.dtype,), b_slots=True, name=name + "_dhn")[0]
        return d_hn, d_wup, d_wdown, dy

    op.defvjp(forward, bwd)
    return op


def _make_ple(name):
    def forward(hn, w_gate, p, w_proj, h):
        pp = _mm_call(p, w_proj, out_dtypes=(F32,), b_slots=True, name=name + "_proj")[0]
        out, gp = _mm_call(hn, w_gate, extras=(h, pp),
                           epilogue=lambda acc, hv, ppv: (hv + _sigmoid(acc) * ppv, acc),
                           out_dtypes=(F32, F32), name=name + "_gate")
        return out, (hn, w_gate, p, w_proj, gp, pp)

    @jax.custom_vjp
    def op(hn, w_gate, p, w_proj, h):
        return forward(hn, w_gate, p, w_proj, h)[0]

    def bwd(res, dy):
        hn, w_gate, p, w_proj, gp, pp = res

        def gate_grads(g, dyv, gpv, ppv):
            s = _sigmoid(gpv)
            return (dyv * ppv * s * (1.0 - s)).astype(BF16), (dyv * s).astype(BF16)

        t, d = dy.shape
        d_gp, d_pp = _rowmap_call(gate_grads, t, 1, [dy, gp, pp], [], [(d, BF16), (d, BF16)], name + "_dgate")
        d_wgate = _mm_call(hn, d_gp, ta=True, out_dtypes=(w_gate.dtype,), name=name + "_dwgate")[0]
        d_wproj = _mm_call(p, d_pp, ta=True, out_dtypes=(w_proj.dtype,), out_slots=True, name=name + "_dwproj")[0]
        d_hn = _mm_call(d_gp, w_gate, tb=True, out_dtypes=(hn.dtype,), name=name + "_dhn")[0]
        return d_hn, d_wgate, jnp.zeros_like(p), d_wproj, dy

    op.defvjp(forward, bwd)
    return op


def _row_tile(t, widths):
    return _tile(t, max(SUBLANES, (256 * 1024) // max(widths)))


def _rowmap_specs(t, g, rows, bcs, tt):
    row_specs = [pl.BlockSpec((tt, r.shape[1] // g), lambda gg, i: (i, gg)) for r in rows]
    bc_specs = []
    for b, per_group in bcs:
        if per_group:
            bc_specs.append(pl.BlockSpec((b.shape[0], b.shape[1] // g), lambda gg, i: (0, gg)))
        else:
            bc_specs.append(pl.BlockSpec(b.shape, lambda gg, i: (0, 0)))
    return row_specs, bc_specs


def _rowmap_call(fn, t, g, rows, bcs, outs, name):
    widths = [r.shape[1] // g for r in rows] + [c // g for c, _ in outs]
    tt = _row_tile(t, widths)
    nr, nb = len(rows), len(bcs)
    row_specs, bc_specs = _rowmap_specs(t, g, rows, bcs, tt)

    def body(*refs):
        vals = [r[...] for r in refs[:nr + nb]]
        res = fn(pl.program_id(0), *vals)
        for o, r in zip(refs[nr + nb:], res):
            o[...] = r.astype(o.dtype)

    return pl.pallas_call(
        body, name=name, grid=(g, t // tt),
        in_specs=row_specs + bc_specs,
        out_specs=[pl.BlockSpec((tt, c // g), lambda gg, i: (i, gg)) for c, _ in outs],
        out_shape=[jax.ShapeDtypeStruct((t, c), d) for c, d in outs],
        compiler_params=_params(("parallel", "parallel")),
    )(*rows, *[b for b, _ in bcs])


def _rowmap_bwd_call(fn, t, g, rows, bcs, cots, name, add0=None):
    widths = [r.shape[1] // g for r in rows] + [c.shape[1] // g for c in cots]
    tt = _row_tile(t, widths)
    nr, nb, nc = len(rows), len(bcs), len(cots)
    na = 0 if add0 is None else 1
    row_specs, bc_specs = _rowmap_specs(t, g, rows, bcs, tt)
    cot_specs = [pl.BlockSpec((tt, c.shape[1] // g), lambda gg, i: (i, gg)) for c in cots]
    add_specs = [] if add0 is None else [row_specs[0]]
    shared = [not per_group for _, per_group in bcs]

    def body(*refs):
        ins = refs[:nr + nb]
        cot_refs = refs[nr + nb:nr + nb + nc]
        add_refs = refs[nr + nb + nc:nr + nb + nc + na]
        d_rows = refs[nr + nb + nc + na:nr + nb + nc + na + nr]
        d_bcs = refs[nr + nb + nc + na + nr:]
        gg, i = pl.program_id(0), pl.program_id(1)
        vals = [r[...] for r in ins]
        _, vjp = jax.vjp(lambda *v: tuple(fn(gg, *v)), *vals)
        grads = vjp(tuple(c[...] for c in cot_refs))
        for j, (o, gr) in enumerate(zip(d_rows, grads[:nr])):
            if j == 0 and na:
                gr = gr + add_refs[0][...]
            o[...] = gr.astype(o.dtype)
        for o, gr, sh in zip(d_bcs, grads[nr:], shared):
            first = jnp.logical_and(i == 0, gg == 0) if sh else i == 0

            @pl.when(first)
            def _():
                o[...] = jnp.zeros_like(o)

            o[...] += gr.astype(o.dtype)

    res = pl.pallas_call(
        body, name=name, grid=(g, t // tt),
        in_specs=row_specs + bc_specs + cot_specs + add_specs,
        out_specs=row_specs + bc_specs,
        out_shape=[jax.ShapeDtypeStruct(r.shape, r.dtype) for r in rows]
        + [jax.ShapeDtypeStruct(b.shape, F32) for b, _ in bcs],
        compiler_params=_params(("arbitrary", "arbitrary")),
    )(*rows, *[b for b, _ in bcs], *cots, *([] if add0 is None else [add0]))
    return res[:nr], res[nr:]


def _make_rowmap(fn, g, n_rows, per_group, outs, name):
    def call(*args):
        rows, bcs = list(args[:n_rows]), list(zip(args[n_rows:], per_group))
        return tuple(_rowmap_call(fn, rows[0].shape[0], g, rows, bcs, outs, name + "_f"))

    @jax.custom_vjp
    def op(*args):
        return call(*args)

    def fwd(*args):
        return call(*args), args

    def bwd(args, cots):
        rows, bcs = list(args[:n_rows]), list(zip(args[n_rows:], per_group))
        d_rows, d_bcs = _rowmap_bwd_call(fn, rows[0].shape[0], g, rows, bcs, list(cots), name + "_b")
        return tuple(d_rows) + tuple(d.astype(b.dtype) for d, (b, _) in zip(d_bcs, bcs))

    op.defvjp(fwd, bwd)
    return op


def _rms_fn(g, h, w):
    y = h * lax.rsqrt(jnp.mean(h * h, axis=-1, keepdims=True) + NORM_EPS)
    return ((y * w).astype(BF16),)


def _make_norm(name):
    def call(h, w):
        return _rowmap_call(_rms_fn, h.shape[0], 1, [h], [(w, False)], [(h.shape[1], BF16)], name + "_f")[0]

    @jax.custom_vjp
    def op(h, w):
        return h, call(h, w)

    def fwd(h, w):
        return (h, call(h, w)), (h, w)

    def bwd(res, cots):
        h, w = res
        dh_pass, dhn = cots
        d_rows, d_bcs = _rowmap_bwd_call(_rms_fn, h.shape[0], 1, [h], [(w, False)], [dhn], name + "_b", add0=dh_pass)
        return d_rows[0], d_bcs[0]

    op.defvjp(fwd, bwd)
    return op


def _loss_call(h, w, target, name):
    t, d = h.shape
    tt = _row_tile(t, [d])

    def body(h_ref, w_ref, t_ref, dh_ref, dw_ref, loss_ref):
        i = pl.program_id(0)
        tgt = t_ref[...]

        def lf(hv, wv):
            y = hv * lax.rsqrt(jnp.mean(hv * hv, axis=-1, keepdims=True) + NORM_EPS) * wv
            err = y - tgt
            return 0.5 * jnp.sum(jnp.mean(err * err, axis=-1, keepdims=True))

        lv, (dh, dw) = jax.value_and_grad(lf, argnums=(0, 1))(h_ref[...], w_ref[...])
        dh_ref[...] = dh

        @pl.when(i == 0)
        def _():
            dw_ref[...] = jnp.zeros_like(dw_ref)
            loss_ref[...] = jnp.zeros_like(loss_ref)

        dw_ref[...] += dw
        loss_ref[...] += jnp.full(loss_ref.shape, lv, F32)

    row = pl.BlockSpec((tt, d), lambda i: (i, 0))
    return pl.pallas_call(
        body, name=name, grid=(t // tt,),
        in_specs=[row, pl.BlockSpec((1, d), lambda i: (0, 0)), row],
        out_specs=[row, pl.BlockSpec((1, d), lambda i: (0, 0)), pl.BlockSpec((SUBLANES, LANES), lambda i: (0, 0))],
        out_shape=[jax.ShapeDtypeStruct((t, d), F32), jax.ShapeDtypeStruct((1, d), F32),
                   jax.ShapeDtypeStruct((SUBLANES, LANES), F32)],
        compiler_params=_params(("arbitrary",)),
    )(h, w, target)


def _make_loss(name):
    @jax.custom_vjp
    def op(h, w, target):
        return _loss_call(h, w, target, name)[2][0, 0]

    def fwd(h, w, target):
        dh, dw, lv = _loss_call(h, w, target, name)
        return lv[0, 0], (dh, dw, target)

    def bwd(res, ct):
        dh, dw, target = res
        return dh * ct, dw * ct, jnp.zeros_like(target)

    op.defvjp(fwd, bwd)
    return op


def _shift_down(cur, halo, s, first):
    if s == 0:
        return cur
    r = pltpu.roll(cur, s, 0)
    p = jnp.where(first, 0.0, pltpu.roll(halo, s, 0))
    rows = lax.broadcasted_iota(jnp.int32, p.shape, 0)
    head = jnp.where(rows < s, p, r[:SUBLANES])
    return jnp.concatenate([head, r[SUBLANES:]], axis=0)


def _shift_up(cur, halo, s, last):
    if s == 0:
        return cur
    n = cur.shape[0]
    r = pltpu.roll(cur, n - s, 0)
    p = jnp.where(last, 0.0, pltpu.roll(halo, SUBLANES - s, 0))
    rows = lax.broadcasted_iota(jnp.int32, p.shape, 0)
    tail = jnp.where(rows >= SUBLANES - s, p, r[n - SUBLANES:])
    return jnp.concatenate([r[:n - SUBLANES], tail], axis=0)


def _conv_specs(t, c):
    tt, cw = _tile(t, 512), _tile(c, 512)
    per = tt // SUBLANES
    nblk = t // SUBLANES
    cur = pl.BlockSpec((tt, cw), lambda j, i: (i, j))
    prev = pl.BlockSpec((SUBLANES, cw), lambda j, i: (jnp.maximum(i * per - 1, 0), j))
    nxt = pl.BlockSpec((SUBLANES, cw), lambda j, i: (jnp.minimum((i + 1) * per, nblk - 1), j))
    wsp = pl.BlockSpec((CONV_W, cw), lambda j, i: (0, j))
    bsp = pl.BlockSpec((1, cw), lambda j, i: (0, j))
    return tt, cw, cur, prev, nxt, wsp, bsp


def _conv_call(x, w, b, name):
    t, c = x.shape
    tt, cw, cur, prev, nxt, wsp, bsp = _conv_specs(t, c)

    def body(x_ref, p_ref, w_ref, b_ref, y_ref):
        first = pl.program_id(1) == 0
        xv, pv = x_ref[...], p_ref[...]
        y = jnp.zeros_like(xv) + b_ref[...]
        for j in range(CONV_W):
            y = y + w_ref[j:j + 1, :] * _shift_down(xv, pv, CONV_W - 1 - j, first)
        y_ref[...] = y

    return pl.pallas_call(
        body, name=name, grid=(c // cw, t // tt),
        in_specs=[cur, prev, wsp, bsp], out_specs=cur,
        out_shape=jax.ShapeDtypeStruct((t, c), F32),
        compiler_params=_params(("parallel", "parallel")),
    )(x, x, w, b)


def _conv_bwd_call(x, w, dy, name):
    t, c = x.shape
    tt, cw, cur, prev, nxt, wsp, bsp = _conv_specs(t, c)
    nt = t // tt

    def body(x_ref, p_ref, w_ref, dy_ref, n_ref, dx_ref, dw_ref, db_ref):
        i = pl.program_id(1)
        first, last = i == 0, i == nt - 1
        xv, pv, dyv, nv = x_ref[...], p_ref[...], dy_ref[...], n_ref[...]

        @pl.when(first)
        def _():
            dw_ref[...] = jnp.zeros_like(dw_ref)
            db_ref[...] = jnp.zeros_like(db_ref)

        dx = jnp.zeros_like(xv)
        for j in range(CONV_W):
            s = CONV_W - 1 - j
            dx = dx + w_ref[j:j + 1, :] * _shift_up(dyv, nv, s, last)
            dw_ref[j:j + 1, :] += jnp.sum(dyv * _shift_down(xv, pv, s, first), axis=0, keepdims=True)
        dx_ref[...] = dx
        db_ref[...] += jnp.sum(dyv, axis=0, keepdims=True)

    return pl.pallas_call(
        body, name=name, grid=(c // cw, nt),
        in_specs=[cur, prev, wsp, cur, nxt], out_specs=[cur, wsp, bsp],
        out_shape=[jax.ShapeDtypeStruct((t, c), F32), jax.ShapeDtypeStruct((CONV_W, c), F32),
                   jax.ShapeDtypeStruct((1, c), F32)],
        compiler_params=_params(("arbitrary", "arbitrary")),
    )(x, x, w, dy, dy)


def _make_conv(name):
    @jax.custom_vjp
    def op(x, w, b):
        return _conv_call(x, w, b, name + "_f")

    def fwd(x, w, b):
        return _conv_call(x, w, b, name + "_f"), (x, w)

    def bwd(res, dy):
        x, w = res
        return tuple(_conv_bwd_call(x, w, dy, name + "_b"))

    op.defvjp(fwd, bwd)
    return op


def _lru_call(a, u, name):
    t, nb, ln = a.shape
    tt = _tile(t, 1024)
    blk = pl.BlockSpec((tt, nb, ln), lambda i: (i, 0, 0))

    def body(a_ref, u_ref, h_ref, carry):
        @pl.when(pl.program_id(0) == 0)
        def _():
            carry[...] = jnp.zeros_like(carry)

        def step(k, h):
            h = a_ref[k] * h + u_ref[k]
            h_ref[k] = h
            return h

        carry[...] = lax.fori_loop(0, tt, step, carry[...], unroll=8)

    return pl.pallas_call(
        body, name=name, grid=(t // tt,), in_specs=[blk, blk], out_specs=blk,
        out_shape=jax.ShapeDtypeStruct(a.shape, F32), scratch_shapes=[pltpu.VMEM((nb, ln), F32)],
        compiler_params=_params(("arbitrary",)),
    )(a, u)


def _lru_bwd_call(a, hs, dy, name):
    t, nb, ln = a.shape
    tt = _tile(t, 1024)
    nt = t // tt
    blk = pl.BlockSpec((tt, nb, ln), lambda i: (nt - 1 - i, 0, 0))
    prev = pl.BlockSpec((1, nb, ln), lambda i: (jnp.maximum((nt - 1 - i) * tt - 1, 0), 0, 0))

    def body(a_ref, h_ref, hp_ref, dy_ref, da_ref, du_ref, carry):
        i = pl.program_id(0)

        @pl.when(i == 0)
        def _():
            carry[...] = jnp.zeros_like(carry)

        h_before = jnp.where(i == nt - 1, 0.0, hp_ref[0])

        def step(k, c):
            r = tt - 1 - k
            dh = dy_ref[r] + c
            du_ref[r] = dh
            da_ref[r] = dh * h_ref[jnp.maximum(r - 1, 0)]
            return a_ref[r] * dh

        carry[...] = lax.fori_loop(0, tt, step, carry[...], unroll=8)
        da_ref[0] = du_ref[0] * h_before

    return pl.pallas_call(
        body, name=name, grid=(nt,), in_specs=[blk, blk, prev, blk], out_specs=[blk, blk],
        out_shape=[jax.ShapeDtypeStruct(a.shape, F32), jax.ShapeDtypeStruct(a.shape, F32)],
        scratch_shapes=[pltpu.VMEM((nb, ln), F32)],
        compiler_params=_params(("arbitrary",)),
    )(a, hs, hs, dy)


def _make_lru(name):
    @jax.custom_vjp
    def op(a, u):
        return _lru_call(a, u, name + "_f")

    def fwd(a, u):
        hs = _lru_call(a, u, name + "_f")
        return hs, (a, hs)

    def bwd(res, dy):
        a, hs = res
        return tuple(_lru_bwd_call(a, hs, dy, name + "_b"))

    op.defvjp(fwd, bwd)
    return op


def _scan_specs(ins, const, heads, hp, chunk, rev_n):
    def tmap(n_of):
        return lambda hg, n: (n_of(n), hg)
    n_of = (lambda n: rev_n - 1 - n) if rev_n else (lambda n: n)
    in_specs = [pl.BlockSpec((chunk, hp * (x.shape[1] // heads)), tmap(n_of)) for x in ins]
    c_spec = pl.BlockSpec((1, hp * (const.shape[1] // heads)), lambda hg, n: (0, hg))
    return in_specs, c_spec, n_of


def _scan_call(chunk_fn, ins, const, heads, hp, chunk, state_shape, out_width, name):
    t = ins[0].shape[0]
    nc = t // chunk
    ni = len(ins)
    in_specs, c_spec, _ = _scan_specs(ins, const, heads, hp, chunk, 0)
    ws = [x.shape[1] // heads for x in ins]
    cw = const.shape[1] // heads
    dk, dv = state_shape

    def body(*refs):
        in_refs, c_ref, o_ref, s_ref, state = refs[:ni], refs[ni], refs[ni + 1], refs[ni + 2], refs[ni + 3]

        @pl.when(pl.program_id(1) == 0)
        def _():
            state[...] = jnp.zeros_like(state)

        for k in range(hp):
            vals = [r[:, k * w:(k + 1) * w] for r, w in zip(in_refs, ws)]
            s0 = state[k]
            s_ref[0, k] = s0
            o, s1 = chunk_fn(*vals, c_ref[:, k * cw:(k + 1) * cw], s0)
            o_ref[:, k * out_width:(k + 1) * out_width] = o
            state[k] = s1

    return pl.pallas_call(
        body, name=name, grid=(heads // hp, nc),
        in_specs=in_specs + [c_spec],
        out_specs=[pl.BlockSpec((chunk, hp * out_width), lambda hg, n: (n, hg)),
                   pl.BlockSpec((1, hp, dk, dv), lambda hg, n: (n, hg, 0, 0))],
        out_shape=[jax.ShapeDtypeStruct((t, heads * out_width), F32),
                   jax.ShapeDtypeStruct((nc, heads, dk, dv), F32)],
        scratch_shapes=[pltpu.VMEM((hp, dk, dv), F32)],
        compiler_params=_params(("parallel", "arbitrary")),
    )(*ins, const)


def _scan_bwd_call(chunk_fn, ins, const, states, d_out, heads, hp, chunk, state_shape, out_width, name):
    t = ins[0].shape[0]
    nc = t // chunk
    ni = len(ins)
    in_specs, c_spec, n_of = _scan_specs(ins, const, heads, hp, chunk, nc)
    ws = [x.shape[1] // heads for x in ins]
    cw = const.shape[1] // heads
    dk, dv = state_shape

    def body(*refs):
        in_refs, c_ref, s_ref, do_ref = refs[:ni], refs[ni], refs[ni + 1], refs[ni + 2]
        d_refs, dstate = refs[ni + 3:ni + 3 + ni], refs[-1]

        @pl.when(pl.program_id(1) == 0)
        def _():
            dstate[...] = jnp.zeros_like(dstate)

        for k in range(hp):
            vals = [r[:, k * w:(k + 1) * w] for r, w in zip(in_refs, ws)]
            cv = c_ref[:, k * cw:(k + 1) * cw]
            _, vjp = jax.vjp(lambda *v: chunk_fn(*v[:-1], cv, v[-1]), *vals, s_ref[0, k])
            grads = vjp((do_ref[:, k * out_width:(k + 1) * out_width], dstate[k]))
            for r, w, gr in zip(d_refs, ws, grads[:-1]):
                r[:, k * w:(k + 1) * w] = gr
            dstate[k] = grads[-1]

    return pl.pallas_call(
        body, name=name, grid=(heads // hp, nc),
        in_specs=in_specs + [c_spec,
                             pl.BlockSpec((1, hp, dk, dv), lambda hg, n: (n_of(n), hg, 0, 0)),
                             pl.BlockSpec((chunk, hp * out_width), lambda hg, n: (n_of(n), hg))],
        out_specs=in_specs,
        out_shape=[jax.ShapeDtypeStruct(x.shape, F32) for x in ins],
        scratch_shapes=[pltpu.VMEM((hp, dk, dv), F32)],
        compiler_params=_params(("parallel", "arbitrary")),
    )(*ins, const, states, d_out)


def _make_scan(chunk_fn, const, heads, hp, chunk, state_shape, out_width, name):
    def call(*ins):
        return _scan_call(chunk_fn, list(ins), const, heads, hp, chunk, state_shape, out_width, name + "_f")

    @jax.custom_vjp
    def op(*ins):
        return call(*ins)[0]

    def fwd(*ins):
        o, states = call(*ins)
        return o, (ins, states)

    def bwd(res, d_out):
        ins, states = res
        return tuple(_scan_bwd_call(chunk_fn, list(ins), const, states, d_out, heads, hp, chunk, state_shape,
                                    out_width, name + "_b"))

    op.defvjp(fwd, bwd)
    return op


def _tri(c):
    ri = lax.broadcasted_iota(jnp.int32, (c, c), 0)
    ci = lax.broadcasted_iota(jnp.int32, (c, c), 1)
    return ri, ci


@jax.custom_vjp
def _neumann_inverse(x):
    c = x.shape[0]
    ri, ci = _tri(c)
    inv = jnp.where(ri == ci, 1.0, 0.0) + x
    xp = x
    for _ in range(max(1, int(np.ceil(np.log2(c))) - 1)):
        xp = _hdot(xp, xp)
        inv = inv + _hdot(inv, xp)
    return inv


def _neumann_inverse_fwd(x):
    inv = _neumann_inverse(x)
    return inv, inv


def _neumann_inverse_bwd(inv, d):
    hi = lax.Precision.HIGHEST
    t = lax.dot_general(d, inv, (_NT, ((), ())), precision=hi, preferred_element_type=F32)
    return (lax.dot_general(inv, t, (_TN, ((), ())), precision=hi, preferred_element_type=F32),)


_neumann_inverse.defvjp(_neumann_inverse_fwd, _neumann_inverse_bwd)


def _dn_chunk(q, k, v, gb, bb, const, s):
    del const
    c = q.shape[0]
    ri, ci = _tri(c)
    causal, strict = ri >= ci, ri > ci
    gc_b = _hdot(causal.astype(F32), gb)
    gcol = jnp.mean(gc_b, axis=1, keepdims=True)
    grow = jnp.mean(gc_b.T, axis=0, keepdims=True)
    bcol = jnp.mean(bb, axis=1, keepdims=True)
    decay = jnp.where(causal, jnp.exp(jnp.where(causal, gcol - grow, 0.0)), 0.0)
    kb = k * bcol
    inv = _neumann_inverse(-jnp.where(strict, _bdot(kb, k, _NT) * decay, 0.0))
    eg = jnp.exp(gcol)
    u = _hdot(inv, v * bcol)
    w = _hdot(inv, kb * eg)
    qk = _bdot(q, k, _NT) * decay
    g_last = jnp.sum(jnp.mean(gb, axis=1, keepdims=True), axis=0, keepdims=True)
    k_dec = k * jnp.exp(g_last - gcol)
    v_new = u - _bdot(w, s, _NN)
    o = _bdot(q * eg, s, _NN) + _bdot(qk, v_new, _NN)
    s_new = s * jnp.exp(g_last) + _bdot(k_dec, v_new, _TN)
    return o, s_new


def _ret_chunk(q, k, v, lg_b, s):
    c = q.shape[0]
    lg = jnp.mean(lg_b, axis=1, keepdims=True)
    ri, ci = _tri(c)
    rel = (ri - ci).astype(F32)
    dmask = jnp.where(rel >= 0, jnp.exp(jnp.maximum(rel, 0.0) * lg), 0.0)
    idx = lax.broadcasted_iota(jnp.int32, (c, 1), 0).astype(F32)
    ks = k * (RET_DK ** -0.5)
    intra = _bdot(_bdot(q, ks, _NT) * dmask, v, _NN)
    q_dec = q * jnp.exp((idx + 1.0) * lg)
    k_dec = ks * jnp.exp((c - 1.0 - idx) * lg)
    o = intra + _bdot(q_dec, s, _NN)
    s_new = s * jnp.exp(c * lg) + _bdot(k_dec, v, _TN)
    return o, s_new


def _attn_block(q, kp, kc, vp, vc, slope_b, first):
    b = q.shape[0]
    ri, ci = _tri(b)
    rel_c = ri - ci
    rel_p = rel_c + b
    qs = q * (HEAD ** -0.5)
    s_c = _bdot(qs, kc, _NT) - slope_b * rel_c.astype(F32)
    s_p = _bdot(qs, kp, _NT) - slope_b * rel_p.astype(F32)
    s_c = jnp.where(rel_c >= 0, s_c, NEG)
    s_p = jnp.where(jnp.logical_and(rel_p <= b, jnp.logical_not(first)), s_p, NEG)
    mx = lax.stop_gradient(jnp.maximum(jnp.max(s_c, axis=1, keepdims=True), jnp.max(s_p, axis=1, keepdims=True)))
    p_c, p_p = jnp.exp(s_c - mx), jnp.exp(s_p - mx)
    den = jnp.sum(p_c, axis=1, keepdims=True) + jnp.sum(p_p, axis=1, keepdims=True)
    o = (_bdot(p_c, vc, _NN) + _bdot(p_p, vp, _NN)) / den
    lse = jnp.broadcast_to(mx + jnp.log(den), o.shape)
    return o, lse


def _attn_call(q, k, v, slopes, name):
    l, cn = q.shape
    b = SWA_BLOCK
    cur = pl.BlockSpec((b, HEAD), lambda j, n: (n, j))
    prev = pl.BlockSpec((b, HEAD), lambda j, n: (jnp.maximum(n - 1, 0), j))
    ssp = pl.BlockSpec((1, HEAD), lambda j, n: (0, j))

    def body(q_ref, kp_ref, kc_ref, vp_ref, vc_ref, s_ref, o_ref, l_ref):
        o, lse = _attn_block(q_ref[...], kp_ref[...], kc_ref[...], vp_ref[...], vc_ref[...], s_ref[...],
                             pl.program_id(1) == 0)
        o_ref[...] = o
        l_ref[...] = lse

    return pl.pallas_call(
        body, name=name, grid=(cn // HEAD, l // b),
        in_specs=[cur, prev, cur, prev, cur, ssp], out_specs=[cur, cur],
        out_shape=[jax.ShapeDtypeStruct((l, cn), F32), jax.ShapeDtypeStruct((l, cn), F32)],
        compiler_params=_params(("parallel", "parallel")),
    )(q, k, k, v, v, slopes)


def _attn_bwd_call(q, k, v, slopes, d_o, d_l, name):
    l, cn = q.shape
    b = SWA_BLOCK
    nb = l // b
    cur = pl.BlockSpec((b, HEAD), lambda j, n: (jnp.minimum(n, nb - 1), j))
    prev = pl.BlockSpec((b, HEAD), lambda j, n: (jnp.clip(n - 1, 0, nb - 1), j))
    ssp = pl.BlockSpec((1, HEAD), lambda j, n: (0, j))

    def body(q_ref, kp_ref, kc_ref, vp_ref, vc_ref, s_ref, do_ref, dl_ref, dq_ref, dk_ref, dv_ref, ck, cv):
        n = pl.program_id(1)

        @pl.when(n == 0)
        def _():
            ck[...] = jnp.zeros_like(ck)
            cv[...] = jnp.zeros_like(cv)

        @pl.when(n < nb)
        def _():
            sv = s_ref[...]
            first = n == 0
            _, vjp = jax.vjp(lambda *a: _attn_block(*a, sv, first),
                             q_ref[...], kp_ref[...], kc_ref[...], vp_ref[...], vc_ref[...])
            dq, dkp, dkc, dvp, dvc = vjp((do_ref[...], dl_ref[...]))
            dq_ref[...] = dq
            dk_ref[...] = ck[...] + dkp
            dv_ref[...] = cv[...] + dvp
            ck[...] = dkc
            cv[...] = dvc

        @pl.when(n == nb)
        def _():
            dk_ref[...] = ck[...]
            dv_ref[...] = cv[...]

    return pl.pallas_call(
        body, name=name, grid=(cn // HEAD, nb + 1),
        in_specs=[cur, prev, cur, prev, cur, ssp, cur, cur], out_specs=[cur, prev, prev],
        out_shape=[jax.ShapeDtypeStruct((l, cn), F32)] * 3,
        scratch_shapes=[pltpu.VMEM((b, HEAD), F32), pltpu.VMEM((b, HEAD), F32)],
        compiler_params=_params(("parallel", "arbitrary")),
    )(q, k, k, v, v, slopes, d_o, d_l)


def _make_attn(slopes, name):
    @jax.custom_vjp
    def op(q, k, v):
        return tuple(_attn_call(q, k, v, slopes, name + "_f"))

    def fwd(q, k, v):
        return tuple(_attn_call(q, k, v, slopes, name + "_f")), (q, k, v)

    def bwd(res, cots):
        q, k, v = res
        return tuple(_attn_bwd_call(q, k, v, slopes, cots[0], cots[1], name + "_b"))

    op.defvjp(fwd, bwd)
    return op


def _dn_pre_fn(g, cq, ck, cv):
    sq, sk, sv = _silu(cq), _silu(ck), _silu(cv)
    qn = sq * lax.rsqrt(jnp.sum(sq * sq, axis=-1, keepdims=True) + 1e-6) * (HEAD ** -0.5)
    kn = sk * lax.rsqrt(jnp.sum(sk * sk, axis=-1, keepdims=True) + 1e-6)
    return qn, kn, sv


def _make_dn_gates_fn(heads):
    def fn(g, ba, a_log, dt_bias):
        lane = lax.broadcasted_iota(jnp.int32, ba.shape, 1)
        lane1 = lax.broadcasted_iota(jnp.int32, a_log.shape, 1)
        betas, gs = [], []
        for h in range(heads):
            b_raw = jnp.sum(jnp.where(lane == h, ba, 0.0), axis=1, keepdims=True)
            a_raw = jnp.sum(jnp.where(lane == heads + h, ba, 0.0), axis=1, keepdims=True)
            al = jnp.sum(jnp.where(lane1 == h, a_log, 0.0), axis=1, keepdims=True)
            dt = jnp.sum(jnp.where(lane1 == h, dt_bias, 0.0), axis=1, keepdims=True)
            beta = _sigmoid(b_raw)
            gl = -jnp.exp(al) * _softplus(a_raw + dt)
            betas.append(jnp.broadcast_to(beta, ba.shape))
            gs.append(jnp.broadcast_to(gl, ba.shape))
        return jnp.concatenate(betas, axis=1), jnp.concatenate(gs, axis=1)
    return fn


def _dn_post_fn(g, o, z, w):
    y = o * lax.rsqrt(jnp.mean(o * o, axis=-1, keepdims=True) + NORM_EPS) * w
    return ((y * _silu(z)).astype(BF16),)


def _lru_pre_fn(g, xc, wa, wx, ba, bx, lam):
    r = _sigmoid(_bdot(xc, wa, _NN) + ba)
    i = _sigmoid(_bdot(xc, wx, _NN) + bx)
    log_a = -LRU_C * r * _softplus(-lam)
    a = jnp.exp(log_a)
    u = jnp.sqrt(1.0 - jnp.exp(2.0 * log_a)) * (i * xc)
    return a, u


def _lru_post_fn(g, hs, yr):
    return ((hs * _gelu(yr)).astype(BF16),)


def _merge_fn(g, o1, o2, o3, l1, l2, l3):
    m = lax.stop_gradient(jnp.maximum(jnp.maximum(l1, l2), l3))
    e1, e2, e3 = jnp.exp(l1 - m), jnp.exp(l2 - m), jnp.exp(l3 - m)
    return (((e1 * o1 + e2 * o2 + e3 * o3) / (e1 + e2 + e3)).astype(BF16),)


def _ret_post_fn(g, o, gate):
    mu = jnp.mean(o, axis=-1, keepdims=True)
    xc = o - mu
    y = xc * lax.rsqrt(jnp.mean(xc * xc, axis=-1, keepdims=True) + GN_EPS)
    return ((y * _silu(gate)).astype(BF16),)


def _pad_lanes(v):
    return jnp.pad(v, (0, LANES - v.shape[0]))[None, :]


def _even_layout(half):
    heads = half // HEAD
    qkv = 3 * half
    segs = [(0, qkv, qkv), (qkv, half, half), (qkv + half, 2 * heads, LANES),
            (qkv + half + 2 * heads, half, half), (qkv + 2 * half + 2 * heads, half, half)]
    return segs


def _pad_ev_w_in(w, half):
    parts = []
    for start, width, padded in _even_layout(half):
        part = w[:, start:start + width]
        if padded != width:
            part = jnp.pad(part, ((0, 0), (0, padded - width)))
        parts.append(part)
    return jnp.concatenate(parts, axis=1)


def _even_mixer(hn, h, lw):
    t, d = h.shape
    half = d // 2
    heads = half // HEAD
    hp = 4 if heads % 4 == 0 else 1
    w_in = lw["w_in"].transpose(1, 0, 2).reshape(d, -1)
    proj = _make_mm("ev_in", F32)(hn, _pad_ev_w_in(w_in, half))
    o0 = 0
    segs = []
    for _, _, padded in _even_layout(half):
        segs.append(proj[:, o0:o0 + padded])
        o0 += padded
    qkv, z, ba, xr, yr = segs
    c = _make_conv("dn_conv")(qkv, lw["dn_conv_w"], jnp.zeros((1, 3 * half), F32))
    q, k, v = _make_rowmap(_dn_pre_fn, heads, 3, [], [(half, F32)] * 3, "dn_pre")(
        c[:, :half], c[:, half:2 * half], c[:, 2 * half:])
    beta_b, g_b = _make_rowmap(_make_dn_gates_fn(heads), 1, 1, [False, False], [(half, F32)] * 2, "dn_gates")(
        ba, _pad_lanes(lw["dn_a_log"]), _pad_lanes(lw["dn_dt_bias"]))
    o = _make_scan(_dn_chunk, jnp.zeros((1, heads * LANES), F32), heads, hp, DN_CHUNK, (HEAD, HEAD), HEAD, "dn_core")(
        q, k, v, g_b, beta_b)
    ya = _make_rowmap(_dn_post_fn, heads, 2, [False], [(half, BF16)], "dn_post")(o, z, lw["dn_norm_w"][None, :])[0]
    nblk = lw["lru_wa"].shape[0]
    xc = _make_conv("lru_conv")(xr, lw["lru_conv_w"], lw["lru_conv_b"][None, :])
    wa = lw["lru_wa"].transpose(1, 0, 2).reshape(HEAD, nblk * HEAD)
    wx = lw["lru_wx"].transpose(1, 0, 2).reshape(HEAD, nblk * HEAD)
    a, u = _make_rowmap(_lru_pre_fn, nblk, 1, [True] * 5, [(half, F32)] * 2, "lru_pre")(
        xc, wa, wx, lw["lru_ba"][None, :], lw["lru_bx"][None, :], lw["lru_lambda"][None, :])
    hs = _make_lru("lru_scan")(a.reshape(t, nblk, HEAD), u.reshape(t, nblk, HEAD)).reshape(t, half)
    yb = _make_rowmap(_lru_post_fn, nblk, 2, [], [(half, BF16)], "lru_post")(hs, yr)[0]
    return _make_mm_res("ev_out")(jnp.concatenate([ya, yb], axis=1), lw["w_out"], h)


def _odd_mixer(hn, h, lw):
    t, d = h.shape
    half = d // 2
    heads = half // HEAD
    rheads = half // RET_DV
    rqk = rheads * RET_DK
    proj = _make_mm("od_in", F32, slots=True)(hn, lw["w_in"])
    cq, ck, cv = proj[:, :half], proj[:, half:2 * half], proj[:, 2 * half:3 * half]
    o1 = 3 * half
    rq, rk = proj[:, o1:o1 + rqk], proj[:, o1 + rqk:o1 + 2 * rqk]
    rv, rg = proj[:, o1 + 2 * rqk:o1 + 2 * rqk + half], proj[:, o1 + 2 * rqk + half:]
    slopes = np.exp2(-8.0 * np.arange(1, heads + 1, dtype=np.float64) / heads)
    outs, lses = [], []
    for window, dil in SWA_BRANCHES:
        assert window // dil == SWA_BLOCK and (t // dil) % SWA_BLOCK == 0
        sl = jnp.asarray(np.tile(np.repeat(slopes * dil, HEAD), dil)[None, :], F32)
        shape = (t // dil, dil * half)
        o_i, l_i = _make_attn(sl, "swa_d%d" % dil)(cq.reshape(shape), ck.reshape(shape), cv.reshape(shape))
        outs.append(o_i.reshape(t, half))
        lses.append(l_i.reshape(t, half))
    yc = _make_rowmap(_merge_fn, heads, 6, [], [(half, BF16)], "swa_merge")(*outs, *lses)[0]
    lg = np.log1p(-np.exp2(-5.0 - np.arange(rheads, dtype=np.float64)))
    lg_b = jnp.asarray(np.repeat(lg, LANES)[None, :], F32)
    hp = 2 if rheads % 2 == 0 else 1
    o_r = _make_scan(_ret_chunk, lg_b, rheads, hp, RET_CHUNK, (RET_DK, RET_DV), RET_DV, "ret_core")(rq, rk, rv)
    yd = _make_rowmap(_ret_post_fn, rheads, 2, [], [(half, BF16)], "ret_post")(o_r, rg)[0]
    return _make_mm_res("od_out")(jnp.concatenate([yc, yd], axis=1), lw["w_out"], h)


def _local_loss(big, small, x, p, target):
    depth = len(big["w_up"])
    h = x
    for i in range(depth):
        j = i // 2
        h, hn = _make_norm("ln_mix")(h, small["ln_mix_w"][i][None, :])
        if i % 2 == 0:
            lw = {"w_in": big["ev_w_in"][j], "w_out": big["ev_w_out"][j]}
            for nm in ("dn_conv_w", "dn_a_log", "dn_dt_bias", "dn_norm_w", "lru_conv_w", "lru_conv_b", "lru_wa",
                       "lru_ba", "lru_wx", "lru_bx", "lru_lambda"):
                lw[nm] = small[nm][j]
            h = _even_mixer(hn, h, lw)
        else:
            h = _odd_mixer(hn, h, {"w_in": big["od_w_in"][j], "w_out": big["od_w_out"][j]})
        h, hn = _make_norm("ln_mlp")(h, small["ln_mlp_w"][i][None, :])
        h = _make_ffn("ffn")(hn, big["w_up"][i], big["w_down"][i], h)
        h, hn = _make_norm("ln_ple")(h, small["ln_ple_w"][i][None, :])
        h = _make_ple("ple")(hn, big["w_ple_gate"][i], p[i], big["w_ple_proj"][i], h)
    return _make_loss("loss_head")(h, small["ln_final_w"][None, :], target)


def _all_gather(x, name):
    r, c = x.shape

    def body(x_ref, out_ref, send_sems, recv_sems, local_sem):
        mx, my, mc = lax.axis_index("x"), lax.axis_index("y"), lax.axis_index("c")
        me, sibling = (mx, my, mc), (mx, my, 1 - mc)
        chips = [(1 - mx, my), (mx, 1 - my), (1 - mx, 1 - my)]

        def slot(px, py, pc):
            return out_ref.at[4 * px + 2 * py + pc]

        def copy(k, block, to, src=None):
            return pltpu.make_async_remote_copy(
                src_ref=slot(*block) if src is None else src, dst_ref=slot(*block),
                send_sem=send_sems.at[k], recv_sem=recv_sems.at[k], device_id=to, device_id_type=MESH)

        mine = pltpu.make_async_copy(x_ref, slot(*me), local_sem)
        mine.start()
        first = [copy(0, me, sibling, src=x_ref)]
        first += [copy(1 + j, me, (*chip, mc), src=x_ref) for j, chip in enumerate(chips)]
        for cp in first:
            cp.start()
        passed = [copy(4 + j, (*chip, mc), sibling) for j, chip in enumerate(chips)]
        for j, chip in enumerate(chips):
            copy(1 + j, (*chip, mc), me).wait_recv()
            passed[j].start()
        copy(0, sibling, me).wait_recv()
        for j, chip in enumerate(chips):
            copy(4 + j, (*chip, 1 - mc), me).wait_recv()
        for cp in first + passed:
            cp.wait_send()
        mine.wait()

    return pl.pallas_call(
        body, name=name,
        out_shape=jax.ShapeDtypeStruct((N_DEV, r, c), x.dtype),
        in_specs=[pl.BlockSpec(memory_space=pl.ANY)],
        out_specs=pl.BlockSpec(memory_space=pl.ANY),
        scratch_shapes=[pltpu.SemaphoreType.DMA((7,)), pltpu.SemaphoreType.DMA((7,)), pltpu.SemaphoreType.DMA],
    )(x)


def _gather_layers(x, name):
    n, r, c = x.shape

    def body(x_ref, *rest):
        outs, (send_sems, recv_sems, local_sems) = rest[:n], rest[n:]
        mx, my, mc = lax.axis_index("x"), lax.axis_index("y"), lax.axis_index("c")
        me, sibling = (mx, my, mc), (mx, my, 1 - mc)
        chips = [(1 - mx, my), (mx, 1 - my), (1 - mx, 1 - my)]

        def slot(l, px, py, pc):
            return outs[l].at[4 * px + 2 * py + pc]

        def copy(k, l, block, to, from_shard=False):
            return pltpu.make_async_remote_copy(
                src_ref=x_ref.at[l] if from_shard else slot(l, *block), dst_ref=slot(l, *block),
                send_sem=send_sems.at[k, l], recv_sem=recv_sems.at[k, l], device_id=to, device_id_type=MESH)

        mine = [pltpu.make_async_copy(x_ref.at[l], slot(l, *me), local_sems.at[l]) for l in range(n)]
        for cp in mine:
            cp.start()
        sent = [copy(0, l, me, sibling, True) for l in range(n)]
        sent += [copy(1 + j, l, me, (*chip, mc), True) for j, chip in enumerate(chips) for l in range(n)]
        for cp in sent:
            cp.start()
        for j, chip in enumerate(chips):
            for l in range(n):
                copy(1 + j, l, (*chip, mc), me).wait_recv()
                passed = copy(4 + j, l, (*chip, mc), sibling)
                passed.start()
                sent.append(passed)
        for l in range(n):
            copy(0, l, sibling, me).wait_recv()
        for j, chip in enumerate(chips):
            for l in range(n):
                copy(4 + j, l, (*chip, 1 - mc), me).wait_recv()
        for cp in sent:
            cp.wait_send()
        for cp in mine:
            cp.wait()

    return pl.pallas_call(
        body, name=name,
        out_shape=[jax.ShapeDtypeStruct((N_DEV, r, c), x.dtype)] * n,
        in_specs=[pl.BlockSpec(memory_space=pl.ANY)],
        out_specs=[pl.BlockSpec(memory_space=pl.ANY)] * n,
        scratch_shapes=[pltpu.SemaphoreType.DMA((7, n)), pltpu.SemaphoreType.DMA((7, n)),
                        pltpu.SemaphoreType.DMA((n,))],
    )(x)


def _swap_pairs(gs, name):
    n = len(gs)
    _, r, c = gs[0].shape

    def body(*refs):
        g_refs, out_ref, send_sems, recv_sems = refs[:n], refs[n], refs[n + 1], refs[n + 2]
        mx, my, mc = lax.axis_index("x"), lax.axis_index("y"), lax.axis_index("c")

        def copy(q, l):
            return pltpu.make_async_remote_copy(
                src_ref=g_refs[l].at[2 * q + (1 - mc)], dst_ref=out_ref.at[q, l],
                send_sem=send_sems.at[q, l], recv_sem=recv_sems.at[q, l],
                device_id=(mx, my, 1 - mc), device_id_type=MESH)

        copies = [copy(q, l) for q in range(4) for l in range(n)]
        for cp in copies:
            cp.start()
        for cp in copies:
            cp.wait_recv()
        for cp in copies:
            cp.wait_send()

    return pl.pallas_call(
        body, name=name,
        out_shape=jax.ShapeDtypeStruct((4, n, r, c), gs[0].dtype),
        in_specs=[pl.BlockSpec(memory_space=pl.ANY)] * n,
        out_specs=pl.BlockSpec(memory_space=pl.ANY),
        scratch_shapes=[pltpu.SemaphoreType.DMA((4, n)), pltpu.SemaphoreType.DMA((4, n))],
    )(*gs)


def _pair_sum(g, recv, layer, side, name):
    _, r, c = g.shape
    tr = _tile(r, max(SUBLANES, (256 * 1024) // c))

    def body(side_ref, g_ref, r_ref, o_ref):
        del side_ref
        o_ref[...] = (g_ref[...].astype(F32) + r_ref[...].astype(F32)).astype(o_ref.dtype)

    return pl.pallas_call(
        body, name=name,
        grid_spec=pltpu.PrefetchScalarGridSpec(
            num_scalar_prefetch=1, grid=(4, r // tr),
            in_specs=[pl.BlockSpec((None, None, tr, c), lambda q, t, side_ref: (q, side_ref[0], t, 0)),
                      pl.BlockSpec((None, None, tr, c), lambda q, t, side_ref: (q, layer, t, 0))],
            out_specs=pl.BlockSpec((None, tr, c), lambda q, t, side_ref: (q, t, 0))),
        out_shape=jax.ShapeDtypeStruct((4, r, c), g.dtype),
        compiler_params=_params(("parallel", "parallel")),
    )(side, g.reshape(4, 2, r, c), recv)


def _deliver_chips(ps, name):
    n = len(ps)
    _, r, c = ps[0].shape

    def body(*refs):
        p_refs, out_ref, send_sems, recv_sems, local_sems = refs[:n], refs[n], refs[n + 1], refs[n + 2], refs[n + 3]
        mx, my, mc = lax.axis_index("x"), lax.axis_index("y"), lax.axis_index("c")
        q_me = 2 * mx + my
        mine = [pltpu.make_async_copy(p_refs[l].at[q_me], out_ref.at[q_me, l], local_sems.at[l]) for l in range(n)]
        for cp in mine:
            cp.start()
        sent, expected = [], []
        for k in range(1, 4):
            fx, fy = (k >> 1) & 1, k & 1
            px = mx + fx - 2 * mx * fx
            py = my + fy - 2 * my * fy
            q_peer = 2 * px + py
            for l in range(n):
                sent.append(pltpu.make_async_remote_copy(
                    src_ref=p_refs[l].at[q_peer], dst_ref=out_ref.at[q_me, l],
                    send_sem=send_sems.at[k - 1, l], recv_sem=recv_sems.at[k - 1, l],
                    device_id=(px, py, mc), device_id_type=MESH))
                expected.append(pltpu.make_async_remote_copy(
                    src_ref=p_refs[l].at[q_peer], dst_ref=out_ref.at[q_peer, l],
                    send_sem=send_sems.at[k - 1, l], recv_sem=recv_sems.at[k - 1, l],
                    device_id=(px, py, mc), device_id_type=MESH))
        for cp in sent:
            cp.start()
        for cp in expected:
            cp.wait_recv()
        for cp in sent:
            cp.wait_send()
        for cp in mine:
            cp.wait()

    return pl.pallas_call(
        body, name=name,
        out_shape=jax.ShapeDtypeStruct((4, n, r, c), ps[0].dtype),
        in_specs=[pl.BlockSpec(memory_space=pl.ANY)] * n,
        out_specs=pl.BlockSpec(memory_space=pl.ANY),
        scratch_shapes=[pltpu.SemaphoreType.DMA((3, n)), pltpu.SemaphoreType.DMA((3, n)),
                        pltpu.SemaphoreType.DMA((n,))],
    )(*ps)


def _slot_sum(slots, name):
    ns, r, c = slots.shape
    tr = _tile(r, 256)

    def body(s_ref, o_ref):
        acc = s_ref[0].astype(F32)
        for s in range(1, ns):
            acc = acc + s_ref[s].astype(F32)
        o_ref[...] = acc

    return pl.pallas_call(
        body, name=name, grid=(r // tr,),
        in_specs=[pl.BlockSpec((ns, tr, c), lambda i: (0, i, 0))],
        out_specs=pl.BlockSpec((tr, c), lambda i: (i, 0)),
        out_shape=jax.ShapeDtypeStruct((r, c), F32),
        compiler_params=_params(("parallel",)),
    )(slots)


def _adamw(slots, w, m, v, name):
    ns, r, c = slots.shape
    tr = _tile(r, max(SUBLANES, (128 * 1024) // c))

    def body(s_ref, w_ref, m_ref, v_ref, g_out, d_out, m_out, v_out):
        g = s_ref[0].astype(F32)
        for s in range(1, ns):
            g = g + s_ref[s].astype(F32)
        mn = ADAM_B1 * m_ref[...] + (1.0 - ADAM_B1) * g
        vn = ADAM_B2 * v_ref[...] + (1.0 - ADAM_B2) * (g * g)
        m_hat = mn / (1.0 - ADAM_B1 ** ADAM_STEP)
        v_hat = vn / (1.0 - ADAM_B2 ** ADAM_STEP)
        g_out[...] = g
        d_out[...] = -ADAM_LR * (m_hat / (jnp.sqrt(v_hat) + ADAM_EPS) + ADAM_WD * w_ref[...])
        m_out[...] = mn
        v_out[...] = vn

    blk = pl.BlockSpec((tr, c), lambda i: (i, 0))
    return pl.pallas_call(
        body, name=name, grid=(r // tr,),
        in_specs=[pl.BlockSpec((ns, tr, c), lambda i: (0, i, 0)), blk, blk, blk],
        out_specs=[blk] * 4,
        out_shape=[jax.ShapeDtypeStruct((r, c), F32)] * 4,
        compiler_params=_params(("parallel",)),
    )(slots, w, m, v)


def _pack(arrays, dtype, row_multiple=SUBLANES):
    flat = jnp.concatenate([a.astype(dtype).reshape(-1) for a in arrays])
    unit = row_multiple * PACK_COLS
    total = -(-flat.shape[0] // unit) * unit
    if total != flat.shape[0]:
        flat = jnp.pad(flat, (0, total - flat.shape[0]))
    return flat.reshape(-1, PACK_COLS)


def _unpack(buf, shapes):
    flat = buf.reshape(-1)
    out, o = [], 0
    for s in shapes:
        n = int(np.prod(s))
        out.append(flat[o:o + n].reshape(s))
        o += n
    return out


BIG = ("w_up", "w_down", "w_ple_proj", "w_ple_gate", "ev_w_in", "ev_w_out", "od_w_in", "od_w_out")
BIG_COL_SHARDED = {"w_up": True, "w_down": False, "w_ple_proj": True, "w_ple_gate": False,
                   "ev_w_in": True, "ev_w_out": False, "od_w_in": True, "od_w_out": False}
SMALL_SHARDED = ("dn_conv_w", "lru_conv_w")
SMALL_REPLICATED = ("ln_mix_w", "ln_mlp_w", "ln_ple_w", "ln_final_w", "dn_a_log", "dn_dt_bias", "dn_norm_w",
                    "lru_conv_b", "lru_wa", "lru_ba", "lru_wx", "lru_bx", "lru_lambda")
WEIGHTS = ("ln_mix_w", "ln_mlp_w", "ln_ple_w", "w_up", "w_down", "w_ple_proj", "w_ple_gate", "ln_final_w",
           "ev_w_in", "ev_w_out", "dn_conv_w", "dn_a_log", "dn_dt_bias", "dn_norm_w", "lru_conv_w", "lru_conv_b",
           "lru_wa", "lru_ba", "lru_wx", "lru_bx", "lru_lambda", "od_w_in", "od_w_out")


def kernel(x, p, ln_mix_w, ln_mlp_w, ln_ple_w, w_up, w_down, w_ple_proj, w_ple_gate, ln_final_w, ev_w_in, ev_w_out, dn_conv_w, dn_a_log, dn_dt_bias, dn_norm_w, lru_conv_w, lru_conv_b, lru_wa, lru_ba, lru_wx, lru_bx, lru_lambda, od_w_in, od_w_out, loss_target, m_ln_mix_w, m_ln_mlp_w, m_ln_ple_w, m_w_up, m_w_down, m_w_ple_proj, m_w_ple_gate, m_ln_final_w, m_ev_w_in, m_ev_w_out, m_dn_conv_w, m_dn_a_log, m_dn_dt_bias, m_dn_norm_w, m_lru_conv_w, m_lru_conv_b, m_lru_wa, m_lru_ba, m_lru_wx, m_lru_bx, m_lru_lambda, m_od_w_in, m_od_w_out, v_ln_mix_w, v_ln_mlp_w, v_ln_ple_w, v_w_up, v_w_down, v_w_ple_proj, v_w_ple_gate, v_ln_final_w, v_ev_w_in, v_ev_w_out, v_dn_conv_w, v_dn_a_log, v_dn_dt_bias, v_dn_norm_w, v_lru_conv_w, v_lru_conv_b, v_lru_wa, v_lru_ba, v_lru_wx, v_lru_bx, v_lru_lambda, v_od_w_in, v_od_w_out):
    w = dict(ln_mix_w=ln_mix_w, ln_mlp_w=ln_mlp_w, ln_ple_w=ln_ple_w, w_up=w_up, w_down=w_down,
             w_ple_proj=w_ple_proj, w_ple_gate=w_ple_gate, ln_final_w=ln_final_w, ev_w_in=ev_w_in,
             ev_w_out=ev_w_out, dn_conv_w=dn_conv_w, dn_a_log=dn_a_log, dn_dt_bias=dn_dt_bias,
             dn_norm_w=dn_norm_w, lru_conv_w=lru_conv_w, lru_conv_b=lru_conv_b, lru_wa=lru_wa, lru_ba=lru_ba,
             lru_wx=lru_wx, lru_bx=lru_bx, lru_lambda=lru_lambda, od_w_in=od_w_in, od_w_out=od_w_out)
    m = dict(ln_mix_w=m_ln_mix_w, ln_mlp_w=m_ln_mlp_w, ln_ple_w=m_ln_ple_w, w_up=m_w_up, w_down=m_w_down,
             w_ple_proj=m_w_ple_proj, w_ple_gate=m_w_ple_gate, ln_final_w=m_ln_final_w, ev_w_in=m_ev_w_in,
             ev_w_out=m_ev_w_out, dn_conv_w=m_dn_conv_w, dn_a_log=m_dn_a_log, dn_dt_bias=m_dn_dt_bias,
             dn_norm_w=m_dn_norm_w, lru_conv_w=m_lru_conv_w, lru_conv_b=m_lru_conv_b, lru_wa=m_lru_wa,
             lru_ba=m_lru_ba, lru_wx=m_lru_wx, lru_bx=m_lru_bx, lru_lambda=m_lru_lambda, od_w_in=m_od_w_in,
             od_w_out=m_od_w_out)
    v = dict(ln_mix_w=v_ln_mix_w, ln_mlp_w=v_ln_mlp_w, ln_ple_w=v_ln_ple_w, w_up=v_w_up, w_down=v_w_down,
             w_ple_proj=v_w_ple_proj, w_ple_gate=v_w_ple_gate, ln_final_w=v_ln_final_w, ev_w_in=v_ev_w_in,
             ev_w_out=v_ev_w_out, dn_conv_w=v_dn_conv_w, dn_a_log=v_dn_a_log, dn_dt_bias=v_dn_dt_bias,
             dn_norm_w=v_dn_norm_w, lru_conv_w=v_lru_conv_w, lru_conv_b=v_lru_conv_b, lru_wa=v_lru_wa,
             lru_ba=v_lru_ba, lru_wx=v_lru_wx, lru_bx=v_lru_bx, lru_lambda=v_lru_lambda, od_w_in=v_od_w_in,
             od_w_out=v_od_w_out)
    me = 4 * lax.axis_index("x") + 2 * lax.axis_index("y") + lax.axis_index("c")

    big = {}
    for n in BIG:
        shards = _gather_layers(w[n].astype(BF16), "gather_" + n)
        big[n] = [s if BIG_COL_SHARDED[n] else s.reshape(-1, s.shape[2]) for s in shards]
    conv_shapes = [w[n].shape for n in SMALL_SHARDED]
    conv_g = _all_gather(_pack([w[n] for n in SMALL_SHARDED], F32), "gather_conv")
    conv_dev = [_unpack(conv_g[s], conv_shapes) for s in range(N_DEV)]
    small = {n: w[n] for n in SMALL_REPLICATED}
    for i, n in enumerate(SMALL_SHARDED):
        small[n] = jnp.concatenate([conv_dev[s][i] for s in range(N_DEV)], axis=-1)

    loss_local, (g_big, g_small, g_x) = jax.value_and_grad(_local_loss, argnums=(0, 1, 2))(
        big, small, x[0], p[:, 0], loss_target[0])
    loss = lax.psum(loss_local, ("x", "y", "c"))

    side = lax.axis_index("c").astype(jnp.int32).reshape(1)
    out = {}
    for n in BIG:
        nl, r, c = w[n].shape
        gs = [g.reshape(N_DEV, r, c) for g in g_big[n]]
        recv = _swap_pairs(gs, "swap_" + n)
        sums = [_pair_sum(g, recv, l, side, "pairsum_" + n) for l, g in enumerate(gs)]
        got = _deliver_chips(sums, "deliver_" + n)
        res = _adamw(got.reshape(4, nl * r, c), w[n].reshape(nl * r, c), m[n].reshape(nl * r, c),
                     v[n].reshape(nl * r, c), "adamw_" + n)
        for kind, buf in zip(("grad", "delta", "new_m", "new_v"), res):
            out[kind, n] = buf.reshape(nl, r, c)

    small_names = SMALL_REPLICATED + SMALL_SHARDED
    small_shapes = [small[n].shape for n in small_names]
    all_small = _all_gather(_pack([g_small[n] for n in small_names], F32), "gather_small_grads")
    total = dict(zip(small_names, _unpack(_slot_sum(all_small, "sum_small_grads"), small_shapes)))
    for n in SMALL_SHARDED:
        width = w[n].shape[-1]
        total[n] = lax.dynamic_slice_in_dim(total[n], me * width, width, axis=-1)
    own_shapes = [w[n].shape for n in small_names]
    res_small = _adamw(_pack([total[n] for n in small_names], F32)[None], _pack([w[n] for n in small_names], F32),
                       _pack([m[n] for n in small_names], F32), _pack([v[n] for n in small_names], F32), "adamw_small")
    for kind, buf in zip(("grad", "delta", "new_m", "new_v"), res_small):
        for n, a in zip(small_names, _unpack(buf, own_shapes)):
            out[kind, n] = a

    return (loss, g_x[None], *[out["grad", n] for n in WEIGHTS], *[out["delta", n] for n in WEIGHTS],
            *[out["new_m", n] for n in WEIGHTS], *[out["new_v", n] for n in WEIGHTS])
```

```python
import functools

import numpy as np
import jax
import jax.numpy as jnp
from jax import lax
from jax.experimental import pallas as pl
from jax.experimental.pallas import tpu as pltpu

F32 = jnp.float32
BF16 = jnp.bfloat16
N_DEV = 8
LANES = 128
SUBLANES = 8
VMEM_LIMIT = 56 * 1024 * 1024
PACK_COLS = 1024
NORM_EPS = 1e-6
GN_EPS = 1e-5
DN_CHUNK = 64
RET_CHUNK = 64
HEAD = 128
RET_DK = 128
RET_DV = 256
SWA_BLOCK = 128
SWA_BRANCHES = ((128, 1), (512, 4), (2048, 16))
LRU_C = 8.0
CONV_W = 4
ADAM_LR, ADAM_B1, ADAM_B2, ADAM_EPS, ADAM_WD, ADAM_STEP = 0.001, 0.9, 0.999, 1e-08, 0.01, 10
NEG = -1e30
MESH = pl.DeviceIdType.MESH


def _params(sem):
    return pltpu.CompilerParams(dimension_semantics=sem, vmem_limit_bytes=VMEM_LIMIT)


def _tile(n, cap):
    for t in (2048, 1024, 896, 768, 640, 512, 384, 256, 128, 64, 32, 16, 8):
        if t <= cap and n % t == 0:
            return t
    return n


def _bdot(a, b, dims):
    return lax.dot_general(a.astype(BF16), b.astype(BF16), (dims, ((), ())), preferred_element_type=F32)


_NN = ((1,), (0,))
_NT = ((1,), (1,))
_TN = ((0,), (0,))


def _hdot(a, b):
    return lax.dot_general(a, b, (_NN, ((), ())), precision=lax.Precision.HIGHEST, preferred_element_type=F32)


def _sigmoid(x):
    return jax.nn.sigmoid(x)


def _silu(x):
    return x * _sigmoid(x)


def _softplus(x):
    return jnp.maximum(x, 0.0) + jnp.log(1.0 + jnp.exp(-jnp.abs(x)))


def _gelu(x):
    return 0.5 * x * (1.0 + jnp.tanh(0.7978845608028654 * (x + 0.044715 * (x * x * x))))


def _mm_call(a, b, *, ta=False, tb=False, extras=(), epilogue=None, out_dtypes=(F32,), b_slots=False,
             out_slots=False, name):
    m, k = (a.shape[1], a.shape[0]) if ta else a.shape
    ne, no = len(extras), len(out_dtypes)
    cap_n = 512 if ne + no > 2 else 1024
    shard = b.shape[2] if b_slots else None
    if b_slots:
        n = b.shape[1] if tb else N_DEV * shard
    else:
        n = b.shape[0] if tb else b.shape[1]
    if out_slots:
        shard = n // N_DEV
    tm = _tile(m, 1024)
    tn = _tile(shard if (out_slots or (b_slots and not tb)) else n, cap_n)
    cap_k = 2048 if (a.dtype == BF16 and b.dtype == BF16) else 1024
    tk = _tile(shard if (b_slots and tb) else k, cap_k)
    nk = k // tk
    dims = ((0,) if ta else (1,), (1,) if tb else (0,))

    def body(*refs):
        a_ref, b_ref = refs[0], refs[1]
        ex = refs[2:2 + ne]
        outs = refs[2 + ne:2 + ne + no]
        acc = refs[-1]
        kk = pl.program_id(2)
        part = _bdot(a_ref[...], b_ref[...], dims)

        def finish(total):
            res = (total,) if epilogue is None else epilogue(total, *[e[...] for e in ex])
            for o, r in zip(outs, res):
                o[...] = r.astype(o.dtype)

        if nk == 1:
            finish(part)
        else:
            @pl.when(kk == 0)
            def _():
                acc[...] = part

            @pl.when(jnp.logical_and(kk > 0, kk < nk - 1))
            def _():
                acc[...] += part

            @pl.when(kk == nk - 1)
            def _():
                finish(acc[...] + part)

    a_spec = pl.BlockSpec((tk, tm), lambda i, j, kk: (kk, i)) if ta else pl.BlockSpec((tm, tk), lambda i, j, kk: (i, kk))
    if b_slots and tb:
        per = shard // tk
        b_spec = pl.BlockSpec((None, tn, tk), lambda i, j, kk: (kk // per, j, kk % per))
    elif b_slots:
        per = shard // tn
        b_spec = pl.BlockSpec((None, tk, tn), lambda i, j, kk: (j // per, kk, j % per))
    elif tb:
        b_spec = pl.BlockSpec((tn, tk), lambda i, j, kk: (j, kk))
    else:
        b_spec = pl.BlockSpec((tk, tn), lambda i, j, kk: (kk, j))
    mn_spec = pl.BlockSpec((tm, tn), lambda i, j, kk: (i, j))
    if out_slots:
        per_o = shard // tn
        out_specs = [pl.BlockSpec((None, tm, tn), lambda i, j, kk: (j // per_o, i, j % per_o))]
        out_shape = [jax.ShapeDtypeStruct((N_DEV, m, shard), out_dtypes[0])]
    else:
        out_specs = [mn_spec] * no
        out_shape = [jax.ShapeDtypeStruct((m, n), d) for d in out_dtypes]
    return pl.pallas_call(
        body, name=name, grid=(m // tm, n // tn, nk),
        in_specs=[a_spec, b_spec] + [mn_spec] * ne,
        out_specs=out_specs,
        out_shape=out_shape,
        scratch_shapes=[pltpu.VMEM((tm, tn), F32)],
        compiler_params=_params(("parallel", "parallel", "arbitrary")),
    )(a, b, *extras)


def _make_mm(name, out_dtype, slots=False):
    @jax.custom_vjp
    def op(a, w):
        return _mm_call(a, w, out_dtypes=(out_dtype,), b_slots=slots, name=name + "_f")[0]

    def fwd(a, w):
        return _mm_call(a, w, out_dtypes=(out_dtype,), b_slots=slots, name=name + "_f")[0], (a, w)

    def bwd(res, dy):
        a, w = res
        da = _mm_call(dy, w, tb=True, out_dtypes=(a.dtype,), b_slots=slots, name=name + "_da")[0]
        dw = _mm_call(a, dy, ta=True, out_dtypes=(w.dtype,), out_slots=slots, name=name + "_dw")[0]
        return da, dw

    op.defvjp(fwd, bwd)
    return op


def _make_mm_res(name):
    def call(a, w, h):
        return _mm_call(a, w, extras=(h,), epilogue=lambda acc, hv: (hv + acc,), out_dtypes=(F32,), name=name + "_f")[0]

    @jax.custom_vjp
    def op(a, w, h):
        return call(a, w, h)

    def fwd(a, w, h):
        return call(a, w, h), (a, w)

    def bwd(res, dy):
        a, w = res
        da = _mm_call(dy, w, tb=True, out_dtypes=(a.dtype,), name=name + "_da")[0]
        dw = _mm_call(a, dy, ta=True, out_dtypes=(w.dtype,), name=name + "_dw")[0]
        return da, dw, dy

    op.defvjp(fwd, bwd)
    return op


def _make_ffn(name):
    def forward(hn, w_up, w_down, h):
        def ep(acc):
            r = jnp.maximum(acc, 0.0)
            return acc, r * r
        u, act = _mm_call(hn, w_up, epilogue=ep, out_dtypes=(BF16, BF16), b_slots=True, name=name + "_up")
        out = _mm_call(act, w_down, extras=(h,), epilogue=lambda acc, hv: (hv + acc,), out_dtypes=(F32,), name=name + "_down")[0]
        return out, (hn, w_up, w_down, u, act)

    @jax.custom_vjp
    def op(hn, w_up, w_down, h):
        return forward(hn, w_up, w_down, h)[0]

    def bwd(res, dy):
        hn, w_up, w_down, u, act = res
        d_wdown = _mm_call(act, dy, ta=True, out_dtypes=(w_down.dtype,), name=name + "_dwdown")[0]
        d_u = _mm_call(dy, w_down, tb=True, extras=(u,),
                       epilogue=lambda acc, uv: (acc * (2.0 * jnp.maximum(uv.astype(F32), 0.0)),),
                       out_dtypes=(BF16,), name=name + "_du")[0]
        d_wup = _mm_call(hn, d_u, ta=True, out_dtypes=(w_up.dtype,), out_slots=True, name=name + "_dwup")[0]
        d_hn = _mm_call(d_u, w_up, tb=True, out_dtypes=(hn.dtype,), b_slots=True, name=name + "_dhn")[0]
        return d_hn, d_wup, d_wdown, dy

    op.defvjp(forward, bwd)
    return op


def _make_ple(name):
    def forward(hn, w_gate, p, w_proj, h):
        pp = _mm_call(p, w_proj, out_dtypes=(F32,), b_slots=True, name=name + "_proj")[0]
        out, gp = _mm_call(hn, w_gate, extras=(h, pp),
                           epilogue=lambda acc, hv, ppv: (hv + _sigmoid(acc) * ppv, acc),
                           out_dtypes=(F32, F32), name=name + "_gate")
        return out, (hn, w_gate, p, w_proj, gp, pp)

    @jax.custom_vjp
    def op(hn, w_gate, p, w_proj, h):
        return forward(hn, w_gate, p, w_proj, h)[0]

    def bwd(res, dy):
        hn, w_gate, p, w_proj, gp, pp = res

        def gate_grads(g, dyv, gpv, ppv):
            s = _sigmoid(gpv)
            return (dyv * ppv * s * (1.0 - s)).astype(BF16), (dyv * s).astype(BF16)

        t, d = dy.shape
        d_gp, d_pp = _rowmap_call(gate_grads, t, 1, [dy, gp, pp], [], [(d, BF16), (d, BF16)], name + "_dgate")
        d_wgate = _mm_call(hn, d_gp, ta=True, out_dtypes=(w_gate.dtype,), name=name + "_dwgate")[0]
        d_wproj = _mm_call(p, d_pp, ta=True, out_dtypes=(w_proj.dtype,), out_slots=True, name=name + "_dwproj")[0]
        d_hn = _mm_call(d_gp, w_gate, tb=True, out_dtypes=(hn.dtype,), name=name + "_dhn")[0]
        return d_hn, d_wgate, jnp.zeros_like(p), d_wproj, dy

    op.defvjp(forward, bwd)
    return op


def _row_tile(t, widths):
    return _tile(t, max(SUBLANES, (256 * 1024) // max(widths)))


def _rowmap_specs(t, g, rows, bcs, tt):
    row_specs = [pl.BlockSpec((tt, r.shape[1] // g), lambda gg, i: (i, gg)) for r in rows]
    bc_specs = []
    for b, per_group in bcs:
        if per_group:
            bc_specs.append(pl.BlockSpec((b.shape[0], b.shape[1] // g), lambda gg, i: (0, gg)))
        else:
            bc_specs.append(pl.BlockSpec(b.shape, lambda gg, i: (0, 0)))
    return row_specs, bc_specs


def _rowmap_call(fn, t, g, rows, bcs, outs, name):
    widths = [r.shape[1] // g for r in rows] + [c // g for c, _ in outs]
    tt = _row_tile(t, widths)
    nr, nb = len(rows), len(bcs)
    row_specs, bc_specs = _rowmap_specs(t, g, rows, bcs, tt)

    def body(*refs):
        vals = [r[...] for r in refs[:nr + nb]]
        res = fn(pl.program_id(0), *vals)
        for o, r in zip(refs[nr + nb:], res):
            o[...] = r.astype(o.dtype)

    return pl.pallas_call(
        body, name=name, grid=(g, t // tt),
        in_specs=row_specs + bc_specs,
        out_specs=[pl.BlockSpec((tt, c // g), lambda gg, i: (i, gg)) for c, _ in outs],
        out_shape=[jax.ShapeDtypeStruct((t, c), d) for c, d in outs],
        compiler_params=_params(("parallel", "parallel")),
    )(*rows, *[b for b, _ in bcs])


def _rowmap_bwd_call(fn, t, g, rows, bcs, cots, name, add0=None):
    widths = [r.shape[1] // g for r in rows] + [c.shape[1] // g for c in cots]
    tt = _row_tile(t, widths)
    nr, nb, nc = len(rows), len(bcs), len(cots)
    na = 0 if add0 is None else 1
    row_specs, bc_specs = _rowmap_specs(t, g, rows, bcs, tt)
    cot_specs = [pl.BlockSpec((tt, c.shape[1] // g), lambda gg, i: (i, gg)) for c in cots]
    add_specs = [] if add0 is None else [row_specs[0]]
    shared = [not per_group for _, per_group in bcs]

    def body(*refs):
        ins = refs[:nr + nb]
        cot_refs = refs[nr + nb:nr + nb + nc]
        add_refs = refs[nr + nb + nc:nr + nb + nc + na]
        d_rows = refs[nr + nb + nc + na:nr + nb + nc + na + nr]
        d_bcs = refs[nr + nb + nc + na + nr:]
        gg, i = pl.program_id(0), pl.program_id(1)
        vals = [r[...] for r in ins]
        _, vjp = jax.vjp(lambda *v: tuple(fn(gg, *v)), *vals)
        grads = vjp(tuple(c[...] for c in cot_refs))
        for j, (o, gr) in enumerate(zip(d_rows, grads[:nr])):
            if j == 0 and na:
                gr = gr + add_refs[0][...]
            o[...] = gr.astype(o.dtype)
        for o, gr, sh in zip(d_bcs, grads[nr:], shared):
            first = jnp.logical_and(i == 0, gg == 0) if sh else i == 0

            @pl.when(first)
            def _():
                o[...] = jnp.zeros_like(o)

            o[...] += gr.astype(o.dtype)

    res = pl.pallas_call(
        body, name=name, grid=(g, t // tt),
        in_specs=row_specs + bc_specs + cot_specs + add_specs,
        out_specs=row_specs + bc_specs,
        out_shape=[jax.ShapeDtypeStruct(r.shape, r.dtype) for r in rows]
        + [jax.ShapeDtypeStruct(b.shape, F32) for b, _ in bcs],
        compiler_params=_params(("arbitrary", "arbitrary")),
    )(*rows, *[b for b, _ in bcs], *cots, *([] if add0 is None else [add0]))
    return res[:nr], res[nr:]


def _make_rowmap(fn, g, n_rows, per_group, outs, name):
    def call(*args):
        rows, bcs = list(args[:n_rows]), list(zip(args[n_rows:], per_group))
        return tuple(_rowmap_call(fn, rows[0].shape[0], g, rows, bcs, outs, name + "_f"))

    @jax.custom_vjp
    def op(*args):
        return call(*args)

    def fwd(*args):
        return call(*args), args

    def bwd(args, cots):
        rows, bcs = list(args[:n_rows]), list(zip(args[n_rows:], per_group))
        d_rows, d_bcs = _rowmap_bwd_call(fn, rows[0].shape[0], g, rows, bcs, list(cots), name + "_b")
        return tuple(d_rows) + tuple(d.astype(b.dtype) for d, (b, _) in zip(d_bcs, bcs))

    op.defvjp(fwd, bwd)
    return op


def _rms_fn(g, h, w):
    y = h * lax.rsqrt(jnp.mean(h * h, axis=-1, keepdims=True) + NORM_EPS)
    return ((y * w).astype(BF16),)


def _make_norm(name):
    def call(h, w):
        return _rowmap_call(_rms_fn, h.shape[0], 1, [h], [(w, False)], [(h.shape[1], BF16)], name + "_f")[0]

    @jax.custom_vjp
    def op(h, w):
        return h, call(h, w)

    def fwd(h, w):
        return (h, call(h, w)), (h, w)

    def bwd(res, cots):
        h, w = res
        dh_pass, dhn = cots
        d_rows, d_bcs = _rowmap_bwd_call(_rms_fn, h.shape[0], 1, [h], [(w, False)], [dhn], name + "_b", add0=dh_pass)
        return d_rows[0], d_bcs[0]

    op.defvjp(fwd, bwd)
    return op


def _loss_call(h, w, target, name):
    t, d = h.shape
    tt = _row_tile(t, [d])

    def body(h_ref, w_ref, t_ref, dh_ref, dw_ref, loss_ref):
        i = pl.program_id(0)
        tgt = t_ref[...]

        def lf(hv, wv):
            y = hv * lax.rsqrt(jnp.mean(hv * hv, axis=-1, keepdims=True) + NORM_EPS) * wv
            err = y - tgt
            return 0.5 * jnp.sum(jnp.mean(err * err, axis=-1, keepdims=True))

        lv, (dh, dw) = jax.value_and_grad(lf, argnums=(0, 1))(h_ref[...], w_ref[...])
        dh_ref[...] = dh

        @pl.when(i == 0)
        def _():
            dw_ref[...] = jnp.zeros_like(dw_ref)
            loss_ref[...] = jnp.zeros_like(loss_ref)

        dw_ref[...] += dw
        loss_ref[...] += jnp.full(loss_ref.shape, lv, F32)

    row = pl.BlockSpec((tt, d), lambda i: (i, 0))
    return pl.pallas_call(
        body, name=name, grid=(t // tt,),
        in_specs=[row, pl.BlockSpec((1, d), lambda i: (0, 0)), row],
        out_specs=[row, pl.BlockSpec((1, d), lambda i: (0, 0)), pl.BlockSpec((SUBLANES, LANES), lambda i: (0, 0))],
        out_shape=[jax.ShapeDtypeStruct((t, d), F32), jax.ShapeDtypeStruct((1, d), F32),
                   jax.ShapeDtypeStruct((SUBLANES, LANES), F32)],
        compiler_params=_params(("arbitrary",)),
    )(h, w, target)


def _make_loss(name):
    @jax.custom_vjp
    def op(h, w, target):
        return _loss_call(h, w, target, name)[2][0, 0]

    def fwd(h, w, target):
        dh, dw, lv = _loss_call(h, w, target, name)
        return lv[0, 0], (dh, dw, target)

    def bwd(res, ct):
        dh, dw, target = res
        return dh * ct, dw * ct, jnp.zeros_like(target)

    op.defvjp(fwd, bwd)
    return op


def _shift_down(cur, halo, s, first):
    if s == 0:
        return cur
    r = pltpu.roll(cur, s, 0)
    p = jnp.where(first, 0.0, pltpu.roll(halo, s, 0))
    rows = lax.broadcasted_iota(jnp.int32, p.shape, 0)
    head = jnp.where(rows < s, p, r[:SUBLANES])
    return jnp.concatenate([head, r[SUBLANES:]], axis=0)


def _shift_up(cur, halo, s, last):
    if s == 0:
        return cur
    n = cur.shape[0]
    r = pltpu.roll(cur, n - s, 0)
    p = jnp.where(last, 0.0, pltpu.roll(halo, SUBLANES - s, 0))
    rows = lax.broadcasted_iota(jnp.int32, p.shape, 0)
    tail = jnp.where(rows >= SUBLANES - s, p, r[n - SUBLANES:])
    return jnp.concatenate([r[:n - SUBLANES], tail], axis=0)


def _conv_specs(t, c):
    tt, cw = _tile(t, 512), _tile(c, 512)
    per = tt // SUBLANES
    nblk = t // SUBLANES
    cur = pl.BlockSpec((tt, cw), lambda j, i: (i, j))
    prev = pl.BlockSpec((SUBLANES, cw), lambda j, i: (jnp.maximum(i * per - 1, 0), j))
    nxt = pl.BlockSpec((SUBLANES, cw), lambda j, i: (jnp.minimum((i + 1) * per, nblk - 1), j))
    wsp = pl.BlockSpec((CONV_W, cw), lambda j, i: (0, j))
    bsp = pl.BlockSpec((1, cw), lambda j, i: (0, j))
    return tt, cw, cur, prev, nxt, wsp, bsp


def _conv_call(x, w, b, name):
    t, c = x.shape
    tt, cw, cur, prev, nxt, wsp, bsp = _conv_specs(t, c)

    def body(x_ref, p_ref, w_ref, b_ref, y_ref):
        first = pl.program_id(1) == 0
        xv, pv = x_ref[...], p_ref[...]
        y = jnp.zeros_like(xv) + b_ref[...]
        for j in range(CONV_W):
            y = y + w_ref[j:j + 1, :] * _shift_down(xv, pv, CONV_W - 1 - j, first)
        y_ref[...] = y

    return pl.pallas_call(
        body, name=name, grid=(c // cw, t // tt),
        in_specs=[cur, prev, wsp, bsp], out_specs=cur,
        out_shape=jax.ShapeDtypeStruct((t, c), F32),
        compiler_params=_params(("parallel", "parallel")),
    )(x, x, w, b)


def _conv_bwd_call(x, w, dy, name):
    t, c = x.shape
    tt, cw, cur, prev, nxt, wsp, bsp = _conv_specs(t, c)
    nt = t // tt

    def body(x_ref, p_ref, w_ref, dy_ref, n_ref, dx_ref, dw_ref, db_ref):
        i = pl.program_id(1)
        first, last = i == 0, i == nt - 1
        xv, pv, dyv, nv = x_ref[...], p_ref[...], dy_ref[...], n_ref[...]

        @pl.when(first)
        def _():
            dw_ref[...] = jnp.zeros_like(dw_ref)
            db_ref[...] = jnp.zeros_like(db_ref)

        dx = jnp.zeros_like(xv)
        for j in range(CONV_W):
            s = CONV_W - 1 - j
            dx = dx + w_ref[j:j + 1, :] * _shift_up(dyv, nv, s, last)
            dw_ref[j:j + 1, :] += jnp.sum(dyv * _shift_down(xv, pv, s, first), axis=0, keepdims=True)
        dx_ref[...] = dx
        db_ref[...] += jnp.sum(dyv, axis=0, keepdims=True)

    return pl.pallas_call(
        body, name=name, grid=(c // cw, nt),
        in_specs=[cur, prev, wsp, cur, nxt], out_specs=[cur, wsp, bsp],
        out_shape=[jax.ShapeDtypeStruct((t, c), F32), jax.ShapeDtypeStruct((CONV_W, c), F32),
                   jax.ShapeDtypeStruct((1, c), F32)],
        compiler_params=_params(("arbitrary", "arbitrary")),
    )(x, x, w, dy, dy)


def _make_conv(name):
    @jax.custom_vjp
    def op(x, w, b):
        return _conv_call(x, w, b, name + "_f")

    def fwd(x, w, b):
        return _conv_call(x, w, b, name + "_f"), (x, w)

    def bwd(res, dy):
        x, w = res
        return tuple(_conv_bwd_call(x, w, dy, name + "_b"))

    op.defvjp(fwd, bwd)
    return op


def _lru_call(a, u, name):
    t, nb, ln = a.shape
    tt = _tile(t, 1024)
    blk = pl.BlockSpec((tt, nb, ln), lambda i: (i, 0, 0))

    def body(a_ref, u_ref, h_ref, carry):
        @pl.when(pl.program_id(0) == 0)
        def _():
            carry[...] = jnp.zeros_like(carry)

        def step(k, h):
            h = a_ref[k] * h + u_ref[k]
            h_ref[k] = h
            return h

        carry[...] = lax.fori_loop(0, tt, step, carry[...], unroll=8)

    return pl.pallas_call(
        body, name=name, grid=(t // tt,), in_specs=[blk, blk], out_specs=blk,
        out_shape=jax.ShapeDtypeStruct(a.shape, F32), scratch_shapes=[pltpu.VMEM((nb, ln), F32)],
        compiler_params=_params(("arbitrary",)),
    )(a, u)


def _lru_bwd_call(a, hs, dy, name):
    t, nb, ln = a.shape
    tt = _tile(t, 1024)
    nt = t // tt
    blk = pl.BlockSpec((tt, nb, ln), lambda i: (nt - 1 - i, 0, 0))
    prev = pl.BlockSpec((1, nb, ln), lambda i: (jnp.maximum((nt - 1 - i) * tt - 1, 0), 0, 0))

    def body(a_ref, h_ref, hp_ref, dy_ref, da_ref, du_ref, carry):
        i = pl.program_id(0)

        @pl.when(i == 0)
        def _():
            carry[...] = jnp.zeros_like(carry)

        h_before = jnp.where(i == nt - 1, 0.0, hp_ref[0])

        def step(k, c):
            r = tt - 1 - k
            dh = dy_ref[r] + c
            du_ref[r] = dh
            da_ref[r] = dh * h_ref[jnp.maximum(r - 1, 0)]
            return a_ref[r] * dh

        carry[...] = lax.fori_loop(0, tt, step, carry[...], unroll=8)
        da_ref[0] = du_ref[0] * h_before

    return pl.pallas_call(
        body, name=name, grid=(nt,), in_specs=[blk, blk, prev, blk], out_specs=[blk, blk],
        out_shape=[jax.ShapeDtypeStruct(a.shape, F32), jax.ShapeDtypeStruct(a.shape, F32)],
        scratch_shapes=[pltpu.VMEM((nb, ln), F32)],
        compiler_params=_params(("arbitrary",)),
    )(a, hs, hs, dy)


def _make_lru(name):
    @jax.custom_vjp
    def op(a, u):
        return _lru_call(a, u, name + "_f")

    def fwd(a, u):
        hs = _lru_call(a, u, name + "_f")
        return hs, (a, hs)

    def bwd(res, dy):
        a, hs = res
        return tuple(_lru_bwd_call(a, hs, dy, name + "_b"))

    op.defvjp(fwd, bwd)
    return op


def _scan_specs(ins, const, heads, hp, chunk, rev_n):
    def tmap(n_of):
        return lambda hg, n: (n_of(n), hg)
    n_of = (lambda n: rev_n - 1 - n) if rev_n else (lambda n: n)
    in_specs = [pl.BlockSpec((chunk, hp * (x.shape[1] // heads)), tmap(n_of)) for x in ins]
    c_spec = pl.BlockSpec((1, hp * (const.shape[1] // heads)), lambda hg, n: (0, hg))
    return in_specs, c_spec, n_of


def _scan_call(chunk_fn, ins, const, heads, hp, chunk, state_shape, out_width, name):
    t = ins[0].shape[0]
    nc = t // chunk
    ni = len(ins)
    in_specs, c_spec, _ = _scan_specs(ins, const, heads, hp, chunk, 0)
    ws = [x.shape[1] // heads for x in ins]
    cw = const.shape[1] // heads
    dk, dv = state_shape

    def body(*refs):
        in_refs, c_ref, o_ref, s_ref, state = refs[:ni], refs[ni], refs[ni + 1], refs[ni + 2], refs[ni + 3]

        @pl.when(pl.program_id(1) == 0)
        def _():
            state[...] = jnp.zeros_like(state)

        vals = [[r[:, k * w:(k + 1) * w] for r, w in zip(in_refs, ws)] for k in range(hp)]
        consts = [c_ref[:, k * cw:(k + 1) * cw] for k in range(hp)]
        s0 = [state[k] for k in range(hp)]
        outs, s1 = chunk_fn(vals, consts, s0)
        for k in range(hp):
            s_ref[0, k] = s0[k]
            o_ref[:, k * out_width:(k + 1) * out_width] = outs[k]
            state[k] = s1[k]

    return pl.pallas_call(
        body, name=name, grid=(heads // hp, nc),
        in_specs=in_specs + [c_spec],
        out_specs=[pl.BlockSpec((chunk, hp * out_width), lambda hg, n: (n, hg)),
                   pl.BlockSpec((1, hp, dk, dv), lambda hg, n: (n, hg, 0, 0))],
        out_shape=[jax.ShapeDtypeStruct((t, heads * out_width), F32),
                   jax.ShapeDtypeStruct((nc, heads, dk, dv), F32)],
        scratch_shapes=[pltpu.VMEM((hp, dk, dv), F32)],
        compiler_params=_params(("parallel", "arbitrary")),
    )(*ins, const)


def _scan_bwd_call(chunk_fn, ins, const, states, d_out, heads, hp, chunk, state_shape, out_width, name):
    t = ins[0].shape[0]
    nc = t // chunk
    ni = len(ins)
    in_specs, c_spec, n_of = _scan_specs(ins, const, heads, hp, chunk, nc)
    ws = [x.shape[1] // heads for x in ins]
    cw = const.shape[1] // heads
    dk, dv = state_shape

    def body(*refs):
        in_refs, c_ref, s_ref, do_ref = refs[:ni], refs[ni], refs[ni + 1], refs[ni + 2]
        d_refs, dstate = refs[ni + 3:ni + 3 + ni], refs[-1]

        @pl.when(pl.program_id(1) == 0)
        def _():
            dstate[...] = jnp.zeros_like(dstate)

        vals = [[r[:, k * w:(k + 1) * w] for r, w in zip(in_refs, ws)] for k in range(hp)]
        consts = [c_ref[:, k * cw:(k + 1) * cw] for k in range(hp)]
        s0 = [s_ref[0, k] for k in range(hp)]
        d_o = [do_ref[:, k * out_width:(k + 1) * out_width] for k in range(hp)]
        d_s1 = [dstate[k] for k in range(hp)]
        _, vjp = jax.vjp(lambda vv, ss: chunk_fn(vv, consts, ss), vals, s0)
        d_vals, d_s0 = vjp((d_o, d_s1))
        for k in range(hp):
            for r, w, gr in zip(d_refs, ws, d_vals[k]):
                r[:, k * w:(k + 1) * w] = gr
            dstate[k] = d_s0[k]

    return pl.pallas_call(
        body, name=name, grid=(heads // hp, nc),
        in_specs=in_specs + [c_spec,
                             pl.BlockSpec((1, hp, dk, dv), lambda hg, n: (n_of(n), hg, 0, 0)),
                             pl.BlockSpec((chunk, hp * out_width), lambda hg, n: (n_of(n), hg))],
        out_specs=in_specs,
        out_shape=[jax.ShapeDtypeStruct(x.shape, F32) for x in ins],
        scratch_shapes=[pltpu.VMEM((hp, dk, dv), F32)],
        compiler_params=_params(("parallel", "arbitrary")),
    )(*ins, const, states, d_out)


def _make_scan(chunk_fn, const, heads, hp, chunk, state_shape, out_width, name):
    def call(*ins):
        return _scan_call(chunk_fn, list(ins), const, heads, hp, chunk, state_shape, out_width, name + "_f")

    @jax.custom_vjp
    def op(*ins):
        return call(*ins)[0]

    def fwd(*ins):
        o, states = call(*ins)
        return o, (ins, states)

    def bwd(res, d_out):
        ins, states = res
        return tuple(_scan_bwd_call(chunk_fn, list(ins), const, states, d_out, heads, hp, chunk, state_shape,
                                    out_width, name + "_b"))

    op.defvjp(fwd, bwd)
    return op


def _tri(c):
    ri = lax.broadcasted_iota(jnp.int32, (c, c), 0)
    ci = lax.broadcasted_iota(jnp.int32, (c, c), 1)
    return ri, ci


def _each(fn, *lists):
    return [fn(*a) for a in zip(*lists)]


@jax.custom_vjp
def _neumann_inverses(xs):
    c = xs[0].shape[0]
    ri, ci = _tri(c)
    eye = jnp.where(ri == ci, 1.0, 0.0)
    invs = [eye + x for x in xs]
    xps = list(xs)
    for _ in range(max(1, int(np.ceil(np.log2(c))) - 1)):
        xps = [_hdot(xp, xp) for xp in xps]
        invs = [inv + _hdot(inv, xp) for inv, xp in zip(invs, xps)]
    return invs


def _neumann_inverses_fwd(xs):
    invs = _neumann_inverses(xs)
    return invs, invs


def _neumann_inverses_bwd(invs, ds):
    hi = lax.Precision.HIGHEST
    ts = [lax.dot_general(d, inv, (_NT, ((), ())), precision=hi, preferred_element_type=F32)
          for d, inv in zip(ds, invs)]
    return ([lax.dot_general(inv, t, (_TN, ((), ())), precision=hi, preferred_element_type=F32)
             for inv, t in zip(invs, ts)],)


_neumann_inverses.defvjp(_neumann_inverses_fwd, _neumann_inverses_bwd)


def _dn_chunk(vals, consts, ss):
    del consts
    qs, ks, vs, gbs, bbs = (list(x) for x in zip(*vals))
    c = qs[0].shape[0]
    ri, ci = _tri(c)
    causal, strict = ri >= ci, ri > ci
    tri_f = causal.astype(F32)
    gc_b = [_hdot(tri_f, gb) for gb in gbs]
    gcol = [jnp.mean(x, axis=1, keepdims=True) for x in gc_b]
    grow = [jnp.mean(x.T, axis=0, keepdims=True) for x in gc_b]
    bcol = [jnp.mean(bb, axis=1, keepdims=True) for bb in bbs]
    decay = _each(lambda a, b: jnp.where(causal, jnp.exp(jnp.where(causal, a - b, 0.0)), 0.0), gcol, grow)
    kb = _each(lambda k, b: k * b, ks, bcol)
    m = _each(lambda a, k: _bdot(a, k, _NT), kb, ks)
    invs = _neumann_inverses(_each(lambda mm, d: -jnp.where(strict, mm * d, 0.0), m, decay))
    eg = [jnp.exp(a) for a in gcol]
    u = _each(lambda inv, v, b: _hdot(inv, v * b), invs, vs, bcol)
    w = _each(lambda inv, a, e: _hdot(inv, a * e), invs, kb, eg)
    qk = _each(lambda q, k, d: _bdot(q, k, _NT) * d, qs, ks, decay)
    g_last = [jnp.sum(jnp.mean(gb, axis=1, keepdims=True), axis=0, keepdims=True) for gb in gbs]
    k_dec = _each(lambda k, gl, a: k * jnp.exp(gl - a), ks, g_last, gcol)
    ws = _each(lambda w_, s: _bdot(w_, s, _NN), w, ss)
    v_new = _each(lambda u_, x: u_ - x, u, ws)
    o_state = _each(lambda q, e, s: _bdot(q * e, s, _NN), qs, eg, ss)
    o_intra = _each(lambda a, vn: _bdot(a, vn, _NN), qk, v_new)
    s_add = _each(lambda kd, vn: _bdot(kd, vn, _TN), k_dec, v_new)
    outs = _each(lambda a, b: a + b, o_state, o_intra)
    s_new = _each(lambda s, gl, a: s * jnp.exp(gl) + a, ss, g_last, s_add)
    return outs, s_new


def _ret_chunk(vals, consts, ss):
    qs, ks, vs = (list(x) for x in zip(*vals))
    c = qs[0].shape[0]
    ri, ci = _tri(c)
    rel = (ri - ci).astype(F32)
    idx = lax.broadcasted_iota(jnp.int32, (c, 1), 0).astype(F32)
    lg = [jnp.mean(x, axis=1, keepdims=True) for x in consts]
    dmask = [jnp.where(rel >= 0, jnp.exp(jnp.maximum(rel, 0.0) * a), 0.0) for a in lg]
    ksc = [k * (RET_DK ** -0.5) for k in ks]
    qk = _each(lambda q, k, d: _bdot(q, k, _NT) * d, qs, ksc, dmask)
    intra = _each(lambda a, v: _bdot(a, v, _NN), qk, vs)
    inter = _each(lambda q, a, s: _bdot(q * jnp.exp((idx + 1.0) * a), s, _NN), qs, lg, ss)
    s_add = _each(lambda k, a, v: _bdot(k * jnp.exp((c - 1.0 - idx) * a), v, _TN), ksc, lg, vs)
    outs = _each(lambda a, b: a + b, intra, inter)
    s_new = _each(lambda s, a, x: s * jnp.exp(c * a) + x, ss, lg, s_add)
    return outs, s_new


def _attn_block(args, slopes, first):
    qs, kps, kcs, vps, vcs = (list(x) for x in zip(*args))
    b = qs[0].shape[0]
    ri, ci = _tri(b)
    rel_c = ri - ci
    rel_p = rel_c + b
    ok_c = rel_c >= 0
    ok_p = jnp.logical_and(rel_p <= b, jnp.logical_not(first))
    rel_cf, rel_pf = rel_c.astype(F32), rel_p.astype(F32)
    qsc = [q * (HEAD ** -0.5) for q in qs]
    s_c = _each(lambda q, k, sl: jnp.where(ok_c, _bdot(q, k, _NT) - sl * rel_cf, NEG), qsc, kcs, slopes)
    s_p = _each(lambda q, k, sl: jnp.where(ok_p, _bdot(q, k, _NT) - sl * rel_pf, NEG), qsc, kps, slopes)
    mx = _each(lambda a, c: lax.stop_gradient(jnp.maximum(jnp.max(a, axis=1, keepdims=True),
                                                          jnp.max(c, axis=1, keepdims=True))), s_c, s_p)
    p_c = _each(lambda a, m: jnp.exp(a - m), s_c, mx)
    p_p = _each(lambda a, m: jnp.exp(a - m), s_p, mx)
    den = _each(lambda a, c: jnp.sum(a, axis=1, keepdims=True) + jnp.sum(c, axis=1, keepdims=True), p_c, p_p)
    o_c = _each(lambda a, v: _bdot(a, v, _NN), p_c, vcs)
    o_p = _each(lambda a, v: _bdot(a, v, _NN), p_p, vps)
    outs = _each(lambda a, c, d: (a + c) / d, o_c, o_p, den)
    lses = _each(lambda m, d, o: jnp.broadcast_to(m + jnp.log(d), o.shape), mx, den, outs)
    return outs, lses


def _attn_heads_per_step(cn):
    return 4 if (cn // HEAD) % 4 == 0 else 1


def _head_cols(ref, k):
    return ref[:, k * HEAD:(k + 1) * HEAD]


def _attn_call(q, k, v, slopes, name):
    l, cn = q.shape
    b = SWA_BLOCK
    hb = _attn_heads_per_step(cn)
    wide = hb * HEAD
    cur = pl.BlockSpec((b, wide), lambda j, n: (n, j))
    prev = pl.BlockSpec((b, wide), lambda j, n: (jnp.maximum(n - 1, 0), j))
    ssp = pl.BlockSpec((1, wide), lambda j, n: (0, j))

    def body(q_ref, kp_ref, kc_ref, vp_ref, vc_ref, s_ref, o_ref, l_ref):
        first = pl.program_id(1) == 0
        args = [[_head_cols(r, h) for r in (q_ref, kp_ref, kc_ref, vp_ref, vc_ref)] for h in range(hb)]
        outs, lses = _attn_block(args, [_head_cols(s_ref, h) for h in range(hb)], first)
        for h in range(hb):
            o_ref[:, h * HEAD:(h + 1) * HEAD] = outs[h]
            l_ref[:, h * HEAD:(h + 1) * HEAD] = lses[h]

    return pl.pallas_call(
        body, name=name, grid=(cn // wide, l // b),
        in_specs=[cur, prev, cur, prev, cur, ssp], out_specs=[cur, cur],
        out_shape=[jax.ShapeDtypeStruct((l, cn), F32), jax.ShapeDtypeStruct((l, cn), F32)],
        compiler_params=_params(("parallel", "parallel")),
    )(q, k, k, v, v, slopes)


def _attn_bwd_call(q, k, v, slopes, d_o, d_l, name):
    l, cn = q.shape
    b = SWA_BLOCK
    nb = l // b
    hb = _attn_heads_per_step(cn)
    wide = hb * HEAD
    cur = pl.BlockSpec((b, wide), lambda j, n: (jnp.minimum(n, nb - 1), j))
    prev = pl.BlockSpec((b, wide), lambda j, n: (jnp.clip(n - 1, 0, nb - 1), j))
    ssp = pl.BlockSpec((1, wide), lambda j, n: (0, j))

    def body(q_ref, kp_ref, kc_ref, vp_ref, vc_ref, s_ref, do_ref, dl_ref, dq_ref, dk_ref, dv_ref, ck, cv):
        n = pl.program_id(1)

        @pl.when(n == 0)
        def _():
            ck[...] = jnp.zeros_like(ck)
            cv[...] = jnp.zeros_like(cv)

        @pl.when(n < nb)
        def _():
            first = n == 0
            args = [[_head_cols(r, h) for r in (q_ref, kp_ref, kc_ref, vp_ref, vc_ref)] for h in range(hb)]
            svs = [_head_cols(s_ref, h) for h in range(hb)]
            cots = ([_head_cols(do_ref, h) for h in range(hb)], [_head_cols(dl_ref, h) for h in range(hb)])
            carry = [(_head_cols(ck, h), _head_cols(cv, h)) for h in range(hb)]
            _, vjp = jax.vjp(lambda a: _attn_block(a, svs, first), args)
            grads = vjp(cots)[0]
            for h in range(hb):
                dq, dkp, dkc, dvp, dvc = grads[h]
                cols = slice(h * HEAD, (h + 1) * HEAD)
                dq_ref[:, cols] = dq
                dk_ref[:, cols] = carry[h][0] + dkp
                dv_ref[:, cols] = carry[h][1] + dvp
                ck[:, cols] = dkc
                cv[:, cols] = dvc

        @pl.when(n == nb)
        def _():
            dk_ref[...] = ck[...]
            dv_ref[...] = cv[...]

    return pl.pallas_call(
        body, name=name, grid=(cn // wide, nb + 1),
        in_specs=[cur, prev, cur, prev, cur, ssp, cur, cur], out_specs=[cur, prev, prev],
        out_shape=[jax.ShapeDtypeStruct((l, cn), F32)] * 3,
        scratch_shapes=[pltpu.VMEM((b, wide), F32), pltpu.VMEM((b, wide), F32)],
        compiler_params=_params(("parallel", "arbitrary")),
    )(q, k, k, v, v, slopes, d_o, d_l)


def _make_attn(slopes, name):
    @jax.custom_vjp
    def op(q, k, v):
        return tuple(_attn_call(q, k, v, slopes, name + "_f"))

    def fwd(q, k, v):
        return tuple(_attn_call(q, k, v, slopes, name + "_f")), (q, k, v)

    def bwd(res, cots):
        q, k, v = res
        return tuple(_attn_bwd_call(q, k, v, slopes, cots[0], cots[1], name + "_b"))

    op.defvjp(fwd, bwd)
    return op


def _dn_pre_fn(g, cq, ck, cv):
    sq, sk, sv = _silu(cq), _silu(ck), _silu(cv)
    qn = sq * lax.rsqrt(jnp.sum(sq * sq, axis=-1, keepdims=True) + 1e-6) * (HEAD ** -0.5)
    kn = sk * lax.rsqrt(jnp.sum(sk * sk, axis=-1, keepdims=True) + 1e-6)
    return qn, kn, sv


def _make_dn_gates_fn(heads):
    def fn(g, ba, a_log, dt_bias):
        lane = lax.broadcasted_iota(jnp.int32, ba.shape, 1)
        lane1 = lax.broadcasted_iota(jnp.int32, a_log.shape, 1)
        betas, gs = [], []
        for h in range(heads):
            b_raw = jnp.sum(jnp.where(lane == h, ba, 0.0), axis=1, keepdims=True)
            a_raw = jnp.sum(jnp.where(lane == heads + h, ba, 0.0), axis=1, keepdims=True)
            al = jnp.sum(jnp.where(lane1 == h, a_log, 0.0), axis=1, keepdims=True)
            dt = jnp.sum(jnp.where(lane1 == h, dt_bias, 0.0), axis=1, keepdims=True)
            beta = _sigmoid(b_raw)
            gl = -jnp.exp(al) * _softplus(a_raw + dt)
            betas.append(jnp.broadcast_to(beta, ba.shape))
            gs.append(jnp.broadcast_to(gl, ba.shape))
        return jnp.concatenate(betas, axis=1), jnp.concatenate(gs, axis=1)
    return fn


def _dn_post_fn(g, o, z, w):
    y = o * lax.rsqrt(jnp.mean(o * o, axis=-1, keepdims=True) + NORM_EPS) * w
    return ((y * _silu(z)).astype(BF16),)


def _lru_pre_fn(g, xc, wa, wx, ba, bx, lam):
    r = _sigmoid(_bdot(xc, wa, _NN) + ba)
    i = _sigmoid(_bdot(xc, wx, _NN) + bx)
    log_a = -LRU_C * r * _softplus(-lam)
    a = jnp.exp(log_a)
    u = jnp.sqrt(1.0 - jnp.exp(2.0 * log_a)) * (i * xc)
    return a, u


def _lru_post_fn(g, hs, yr):
    return ((hs * _gelu(yr)).astype(BF16),)


def _merge_fn(g, o1, o2, o3, l1, l2, l3):
    m = lax.stop_gradient(jnp.maximum(jnp.maximum(l1, l2), l3))
    e1, e2, e3 = jnp.exp(l1 - m), jnp.exp(l2 - m), jnp.exp(l3 - m)
    return (((e1 * o1 + e2 * o2 + e3 * o3) / (e1 + e2 + e3)).astype(BF16),)


def _ret_post_fn(g, o, gate):
    mu = jnp.mean(o, axis=-1, keepdims=True)
    xc = o - mu
    y = xc * lax.rsqrt(jnp.mean(xc * xc, axis=-1, keepdims=True) + GN_EPS)
    return ((y * _silu(gate)).astype(BF16),)


def _pad_lanes(v):
    return jnp.pad(v, (0, LANES - v.shape[0]))[None, :]


def _even_layout(half):
    heads = half // HEAD
    qkv = 3 * half
    segs = [(0, qkv, qkv), (qkv, half, half), (qkv + half, 2 * heads, LANES),
            (qkv + half + 2 * heads, half, half), (qkv + 2 * half + 2 * heads, half, half)]
    return segs


def _pad_ev_w_in(w, half):
    parts = []
    for start, width, padded in _even_layout(half):
        part = w[:, start:start + width]
        if padded != width:
            part = jnp.pad(part, ((0, 0), (0, padded - width)))
        parts.append(part)
    return jnp.concatenate(parts, axis=1)


def _even_mixer(hn, h, lw):
    t, d = h.shape
    half = d // 2
    heads = half // HEAD
    hp = 4 if heads % 4 == 0 else 1
    w_in = lw["w_in"].transpose(1, 0, 2).reshape(d, -1)
    proj = _make_mm("ev_in", F32)(hn, _pad_ev_w_in(w_in, half))
    o0 = 0
    segs = []
    for _, _, padded in _even_layout(half):
        segs.append(proj[:, o0:o0 + padded])
        o0 += padded
    qkv, z, ba, xr, yr = segs
    c = _make_conv("dn_conv")(qkv, lw["dn_conv_w"], jnp.zeros((1, 3 * half), F32))
    q, k, v = _make_rowmap(_dn_pre_fn, heads, 3, [], [(half, F32)] * 3, "dn_pre")(
        c[:, :half], c[:, half:2 * half], c[:, 2 * half:])
    beta_b, g_b = _make_rowmap(_make_dn_gates_fn(heads), 1, 1, [False, False], [(half, F32)] * 2, "dn_gates")(
        ba, _pad_lanes(lw["dn_a_log"]), _pad_lanes(lw["dn_dt_bias"]))
    o = _make_scan(_dn_chunk, jnp.zeros((1, heads * LANES), F32), heads, hp, DN_CHUNK, (HEAD, HEAD), HEAD, "dn_core")(
        q, k, v, g_b, beta_b)
    ya = _make_rowmap(_dn_post_fn, heads, 2, [False], [(half, BF16)], "dn_post")(o, z, lw["dn_norm_w"][None, :])[0]
    nblk = lw["lru_wa"].shape[0]
    xc = _make_conv("lru_conv")(xr, lw["lru_conv_w"], lw["lru_conv_b"][None, :])
    wa = lw["lru_wa"].transpose(1, 0, 2).reshape(HEAD, nblk * HEAD)
    wx = lw["lru_wx"].transpose(1, 0, 2).reshape(HEAD, nblk * HEAD)
    a, u = _make_rowmap(_lru_pre_fn, nblk, 1, [True] * 5, [(half, F32)] * 2, "lru_pre")(
        xc, wa, wx, lw["lru_ba"][None, :], lw["lru_bx"][None, :], lw["lru_lambda"][None, :])
    hs = _make_lru("lru_scan")(a.reshape(t, nblk, HEAD), u.reshape(t, nblk, HEAD)).reshape(t, half)
    yb = _make_rowmap(_lru_post_fn, nblk, 2, [], [(half, BF16)], "lru_post")(hs, yr)[0]
    return _make_mm_res("ev_out")(jnp.concatenate([ya, yb], axis=1), lw["w_out"], h)


def _odd_mixer(hn, h, lw):
    t, d = h.shape
    half = d // 2
    heads = half // HEAD
    rheads = half // RET_DV
    rqk = rheads * RET_DK
    proj = _make_mm("od_in", F32, slots=True)(hn, lw["w_in"])
    cq, ck, cv = proj[:, :half], proj[:, half:2 * half], proj[:, 2 * half:3 * half]
    o1 = 3 * half
    rq, rk = proj[:, o1:o1 + rqk], proj[:, o1 + rqk:o1 + 2 * rqk]
    rv, rg = proj[:, o1 + 2 * rqk:o1 + 2 * rqk + half], proj[:, o1 + 2 * rqk + half:]
    slopes = np.exp2(-8.0 * np.arange(1, heads + 1, dtype=np.float64) / heads)
    outs, lses = [], []
    for window, dil in SWA_BRANCHES:
        assert window // dil == SWA_BLOCK and (t // dil) % SWA_BLOCK == 0
        sl = jnp.asarray(np.tile(np.repeat(slopes * dil, HEAD), dil)[None, :], F32)
        shape = (t // dil, dil * half)
        o_i, l_i = _make_attn(sl, "swa_d%d" % dil)(cq.reshape(shape), ck.reshape(shape), cv.reshape(shape))
        outs.append(o_i.reshape(t, half))
        lses.append(l_i.reshape(t, half))
    yc = _make_rowmap(_merge_fn, heads, 6, [], [(half, BF16)], "swa_merge")(*outs, *lses)[0]
    lg = np.log1p(-np.exp2(-5.0 - np.arange(rheads, dtype=np.float64)))
    lg_b = jnp.asarray(np.repeat(lg, LANES)[None, :], F32)
    hp = 2 if rheads % 2 == 0 else 1
    o_r = _make_scan(_ret_chunk, lg_b, rheads, hp, RET_CHUNK, (RET_DK, RET_DV), RET_DV, "ret_core")(rq, rk, rv)
    yd = _make_rowmap(_ret_post_fn, rheads, 2, [], [(half, BF16)], "ret_post")(o_r, rg)[0]
    return _make_mm_res("od_out")(jnp.concatenate([yc, yd], axis=1), lw["w_out"], h)


def _local_loss(big, small, x, p, target):
    depth = len(big["w_up"])
    h = x
    for i in range(depth):
        j = i // 2
        h, hn = _make_norm("ln_mix")(h, small["ln_mix_w"][i][None, :])
        if i % 2 == 0:
            lw = {"w_in": big["ev_w_in"][j], "w_out": big["ev_w_out"][j]}
            for nm in ("dn_conv_w", "dn_a_log", "dn_dt_bias", "dn_norm_w", "lru_conv_w", "lru_conv_b", "lru_wa",
                       "lru_ba", "lru_wx", "lru_bx", "lru_lambda"):
                lw[nm] = small[nm][j]
            h = _even_mixer(hn, h, lw)
        else:
            h = _odd_mixer(hn, h, {"w_in": big["od_w_in"][j], "w_out": big["od_w_out"][j]})
        h, hn = _make_norm("ln_mlp")(h, small["ln_mlp_w"][i][None, :])
        h = _make_ffn("ffn")(hn, big["w_up"][i], big["w_down"][i], h)
        h, hn = _make_norm("ln_ple")(h, small["ln_ple_w"][i][None, :])
        h = _make_ple("ple")(hn, big["w_ple_gate"][i], p[i], big["w_ple_proj"][i], h)
    return _make_loss("loss_head")(h, small["ln_final_w"][None, :], target)


def _all_gather(x, name):
    r, c = x.shape

    def body(x_ref, out_ref, send_sems, recv_sems, local_sem):
        mx, my, mc = lax.axis_index("x"), lax.axis_index("y"), lax.axis_index("c")
        me, sibling = (mx, my, mc), (mx, my, 1 - mc)
        chips = [(1 - mx, my), (mx, 1 - my), (1 - mx, 1 - my)]

        def slot(px, py, pc):
            return out_ref.at[4 * px + 2 * py + pc]

        def copy(k, block, to, src=None):
            return pltpu.make_async_remote_copy(
                src_ref=slot(*block) if src is None else src, dst_ref=slot(*block),
                send_sem=send_sems.at[k], recv_sem=recv_sems.at[k], device_id=to, device_id_type=MESH)

        mine = pltpu.make_async_copy(x_ref, slot(*me), local_sem)
        mine.start()
        first = [copy(0, me, sibling, src=x_ref)]
        first += [copy(1 + j, me, (*chip, mc), src=x_ref) for j, chip in enumerate(chips)]
        for cp in first:
            cp.start()
        passed = [copy(4 + j, (*chip, mc), sibling) for j, chip in enumerate(chips)]
        for j, chip in enumerate(chips):
            copy(1 + j, (*chip, mc), me).wait_recv()
            passed[j].start()
        copy(0, sibling, me).wait_recv()
        for j, chip in enumerate(chips):
            copy(4 + j, (*chip, 1 - mc), me).wait_recv()
        for cp in first + passed:
            cp.wait_send()
        mine.wait()

    return pl.pallas_call(
        body, name=name,
        out_shape=jax.ShapeDtypeStruct((N_DEV, r, c), x.dtype),
        in_specs=[pl.BlockSpec(memory_space=pl.ANY)],
        out_specs=pl.BlockSpec(memory_space=pl.ANY),
        scratch_shapes=[pltpu.SemaphoreType.DMA((7,)), pltpu.SemaphoreType.DMA((7,)), pltpu.SemaphoreType.DMA],
    )(x)


def _gather_layers(x, name):
    n, r, c = x.shape

    def body(x_ref, *rest):
        outs, (send_sems, recv_sems, local_sems) = rest[:n], rest[n:]
        mx, my, mc = lax.axis_index("x"), lax.axis_index("y"), lax.axis_index("c")
        me, sibling = (mx, my, mc), (mx, my, 1 - mc)
        chips = [(1 - mx, my), (mx, 1 - my), (1 - mx, 1 - my)]

        def slot(l, px, py, pc):
            return outs[l].at[4 * px + 2 * py + pc]

        def copy(k, l, block, to, from_shard=False):
            return pltpu.make_async_remote_copy(
                src_ref=x_ref.at[l] if from_shard else slot(l, *block), dst_ref=slot(l, *block),
                send_sem=send_sems.at[k, l], recv_sem=recv_sems.at[k, l], device_id=to, device_id_type=MESH)

        mine = [pltpu.make_async_copy(x_ref.at[l], slot(l, *me), local_sems.at[l]) for l in range(n)]
        for cp in mine:
            cp.start()
        sent = [copy(0, l, me, sibling, True) for l in range(n)]
        sent += [copy(1 + j, l, me, (*chip, mc), True) for j, chip in enumerate(chips) for l in range(n)]
        for cp in sent:
            cp.start()
        for j, chip in enumerate(chips):
            for l in range(n):
                copy(1 + j, l, (*chip, mc), me).wait_recv()
                passed = copy(4 + j, l, (*chip, mc), sibling)
                passed.start()
                sent.append(passed)
        for l in range(n):
            copy(0, l, sibling, me).wait_recv()
        for j, chip in enumerate(chips):
            for l in range(n):
                copy(4 + j, l, (*chip, 1 - mc), me).wait_recv()
        for cp in sent:
            cp.wait_send()
        for cp in mine:
            cp.wait()

    return pl.pallas_call(
        body, name=name,
        out_shape=[jax.ShapeDtypeStruct((N_DEV, r, c), x.dtype)] * n,
        in_specs=[pl.BlockSpec(memory_space=pl.ANY)],
        out_specs=[pl.BlockSpec(memory_space=pl.ANY)] * n,
        scratch_shapes=[pltpu.SemaphoreType.DMA((7, n)), pltpu.SemaphoreType.DMA((7, n)),
                        pltpu.SemaphoreType.DMA((n,))],
    )(x)


def _swap_pairs(gs, name):
    n = len(gs)
    _, r, c = gs[0].shape

    def body(*refs):
        g_refs, out_ref, send_sems, recv_sems = refs[:n], refs[n], refs[n + 1], refs[n + 2]
        mx, my, mc = lax.axis_index("x"), lax.axis_index("y"), lax.axis_index("c")

        def copy(q, l):
            return pltpu.make_async_remote_copy(
                src_ref=g_refs[l].at[2 * q + (1 - mc)], dst_ref=out_ref.at[q, l],
                send_sem=send_sems.at[q, l], recv_sem=recv_sems.at[q, l],
                device_id=(mx, my, 1 - mc), device_id_type=MESH)

        copies = [copy(q, l) for q in range(4) for l in range(n)]
        for cp in copies:
            cp.start()
        for cp in copies:
            cp.wait_recv()
        for cp in copies:
            cp.wait_send()

    return pl.pallas_call(
        body, name=name,
        out_shape=jax.ShapeDtypeStruct((4, n, r, c), gs[0].dtype),
        in_specs=[pl.BlockSpec(memory_space=pl.ANY)] * n,
        out_specs=pl.BlockSpec(memory_space=pl.ANY),
        scratch_shapes=[pltpu.SemaphoreType.DMA((4, n)), pltpu.SemaphoreType.DMA((4, n))],
    )(*gs)


def _pair_sum(g, recv, layer, side, name):
    _, r, c = g.shape
    tr = _tile(r, max(SUBLANES, (256 * 1024) // c))

    def body(side_ref, g_ref, r_ref, o_ref):
        del side_ref
        o_ref[...] = (g_ref[...].astype(F32) + r_ref[...].astype(F32)).astype(o_ref.dtype)

    return pl.pallas_call(
        body, name=name,
        grid_spec=pltpu.PrefetchScalarGridSpec(
            num_scalar_prefetch=1, grid=(4, r // tr),
            in_specs=[pl.BlockSpec((None, None, tr, c), lambda q, t, side_ref: (q, side_ref[0], t, 0)),
                      pl.BlockSpec((None, None, tr, c), lambda q, t, side_ref: (q, layer, t, 0))],
            out_specs=pl.BlockSpec((None, tr, c), lambda q, t, side_ref: (q, t, 0))),
        out_shape=jax.ShapeDtypeStruct((4, r, c), g.dtype),
        compiler_params=_params(("parallel", "parallel")),
    )(side, g.reshape(4, 2, r, c), recv)


def _deliver_chips(ps, name):
    n = len(ps)
    _, r, c = ps[0].shape

    def body(*refs):
        p_refs, out_ref, send_sems, recv_sems, local_sems = refs[:n], refs[n], refs[n + 1], refs[n + 2], refs[n + 3]
        mx, my, mc = lax.axis_index("x"), lax.axis_index("y"), lax.axis_index("c")
        q_me = 2 * mx + my
        mine = [pltpu.make_async_copy(p_refs[l].at[q_me], out_ref.at[q_me, l], local_sems.at[l]) for l in range(n)]
        for cp in mine:
            cp.start()
        sent, expected = [], []
        for k in range(1, 4):
            fx, fy = (k >> 1) & 1, k & 1
            px = mx + fx - 2 * mx * fx
            py = my + fy - 2 * my * fy
            q_peer = 2 * px + py
            for l in range(n):
                sent.append(pltpu.make_async_remote_copy(
                    src_ref=p_refs[l].at[q_peer], dst_ref=out_ref.at[q_me, l],
                    send_sem=send_sems.at[k - 1, l], recv_sem=recv_sems.at[k - 1, l],
                    device_id=(px, py, mc), device_id_type=MESH))
                expected.append(pltpu.make_async_remote_copy(
                    src_ref=p_refs[l].at[q_peer], dst_ref=out_ref.at[q_peer, l],
                    send_sem=send_sems.at[k - 1, l], recv_sem=recv_sems.at[k - 1, l],
                    device_id=(px, py, mc), device_id_type=MESH))
        for cp in sent:
            cp.start()
        for cp in expected:
            cp.wait_recv()
        for cp in sent:
            cp.wait_send()
        for cp in mine:
            cp.wait()

    return pl.pallas_call(
        body, name=name,
        out_shape=jax.ShapeDtypeStruct((4, n, r, c), ps[0].dtype),
        in_specs=[pl.BlockSpec(memory_space=pl.ANY)] * n,
        out_specs=pl.BlockSpec(memory_space=pl.ANY),
        scratch_shapes=[pltpu.SemaphoreType.DMA((3, n)), pltpu.SemaphoreType.DMA((3, n)),
                        pltpu.SemaphoreType.DMA((n,))],
    )(*ps)


def _slot_sum(slots, name):
    ns, r, c = slots.shape
    tr = _tile(r, 256)

    def body(s_ref, o_ref):
        acc = s_ref[0].astype(F32)
        for s in range(1, ns):
            acc = acc + s_ref[s].astype(F32)
        o_ref[...] = acc

    return pl.pallas_call(
        body, name=name, grid=(r // tr,),
        in_specs=[pl.BlockSpec((ns, tr, c), lambda i: (0, i, 0))],
        out_specs=pl.BlockSpec((tr, c), lambda i: (i, 0)),
        out_shape=jax.ShapeDtypeStruct((r, c), F32),
        compiler_params=_params(("parallel",)),
    )(slots)


def _adamw(slots, w, m, v, name):
    ns, r, c = slots.shape
    tr = _tile(r, max(SUBLANES, (128 * 1024) // c))

    def body(s_ref, w_ref, m_ref, v_ref, g_out, d_out, m_out, v_out):
        g = s_ref[0].astype(F32)
        for s in range(1, ns):
            g = g + s_ref[s].astype(F32)
        mn = ADAM_B1 * m_ref[...] + (1.0 - ADAM_B1) * g
        vn = ADAM_B2 * v_ref[...] + (1.0 - ADAM_B2) * (g * g)
        m_hat = mn / (1.0 - ADAM_B1 ** ADAM_STEP)
        v_hat = vn / (1.0 - ADAM_B2 ** ADAM_STEP)
        g_out[...] = g
        d_out[...] = -ADAM_LR * (m_hat / (jnp.sqrt(v_hat) + ADAM_EPS) + ADAM_WD * w_ref[...])
        m_out[...] = mn
        v_out[...] = vn

    blk = pl.BlockSpec((tr, c), lambda i: (i, 0))
    return pl.pallas_call(
        body, name=name, grid=(r // tr,),
        in_specs=[pl.BlockSpec((ns, tr, c), lambda i: (0, i, 0)), blk, blk, blk],
        out_specs=[blk] * 4,
        out_shape=[jax.ShapeDtypeStruct((r, c), F32)] * 4,
        compiler_params=_params(("parallel",)),
    )(slots, w, m, v)


def _pack(arrays, dtype, row_multiple=SUBLANES):
    flat = jnp.concatenate([a.astype(dtype).reshape(-1) for a in arrays])
    unit = row_multiple * PACK_COLS
    total = -(-flat.shape[0] // unit) * unit
    if total != flat.shape[0]:
        flat = jnp.pad(flat, (0, total - flat.shape[0]))
    return flat.reshape(-1, PACK_COLS)


def _unpack(buf, shapes):
    flat = buf.reshape(-1)
    out, o = [], 0
    for s in shapes:
        n = int(np.prod(s))
        out.append(flat[o:o + n].reshape(s))
        o += n
    return out


BIG = ("w_up", "w_down", "w_ple_proj", "w_ple_gate", "ev_w_in", "ev_w_out", "od_w_in", "od_w_out")
BIG_COL_SHARDED = {"w_up": True, "w_down": False, "w_ple_proj": True, "w_ple_gate": False,
                   "ev_w_in": True, "ev_w_out": False, "od_w_in": True, "od_w_out": False}
SMALL_SHARDED = ("dn_conv_w", "lru_conv_w")
SMALL_REPLICATED = ("ln_mix_w", "ln_mlp_w", "ln_ple_w", "ln_final_w", "dn_a_log", "dn_dt_bias", "dn_norm_w",
                    "lru_conv_b", "lru_wa", "lru_ba", "lru_wx", "lru_bx", "lru_lambda")
WEIGHTS = ("ln_mix_w", "ln_mlp_w", "ln_ple_w", "w_up", "w_down", "w_ple_proj", "w_ple_gate", "ln_final_w",
           "ev_w_in", "ev_w_out", "dn_conv_w", "dn_a_log", "dn_dt_bias", "dn_norm_w", "lru_conv_w", "lru_conv_b",
           "lru_wa", "lru_ba", "lru_wx", "lru_bx", "lru_lambda", "od_w_in", "od_w_out")


def kernel(x, p, ln_mix_w, ln_mlp_w, ln_ple_w, w_up, w_down, w_ple_proj, w_ple_gate, ln_final_w, ev_w_in, ev_w_out, dn_conv_w, dn_a_log, dn_dt_bias, dn_norm_w, lru_conv_w, lru_conv_b, lru_wa, lru_ba, lru_wx, lru_bx, lru_lambda, od_w_in, od_w_out, loss_target, m_ln_mix_w, m_ln_mlp_w, m_ln_ple_w, m_w_up, m_w_down, m_w_ple_proj, m_w_ple_gate, m_ln_final_w, m_ev_w_in, m_ev_w_out, m_dn_conv_w, m_dn_a_log, m_dn_dt_bias, m_dn_norm_w, m_lru_conv_w, m_lru_conv_b, m_lru_wa, m_lru_ba, m_lru_wx, m_lru_bx, m_lru_lambda, m_od_w_in, m_od_w_out, v_ln_mix_w, v_ln_mlp_w, v_ln_ple_w, v_w_up, v_w_down, v_w_ple_proj, v_w_ple_gate, v_ln_final_w, v_ev_w_in, v_ev_w_out, v_dn_conv_w, v_dn_a_log, v_dn_dt_bias, v_dn_norm_w, v_lru_conv_w, v_lru_conv_b, v_lru_wa, v_lru_ba, v_lru_wx, v_lru_bx, v_lru_lambda, v_od_w_in, v_od_w_out):
    w = dict(ln_mix_w=ln_mix_w, ln_mlp_w=ln_mlp_w, ln_ple_w=ln_ple_w, w_up=w_up, w_down=w_down,
             w_ple_proj=w_ple_proj, w_ple_gate=w_ple_gate, ln_final_w=ln_final_w, ev_w_in=ev_w_in,
             ev_w_out=ev_w_out, dn_conv_w=dn_conv_w, dn_a_log=dn_a_log, dn_dt_bias=dn_dt_bias,
             dn_norm_w=dn_norm_w, lru_conv_w=lru_conv_w, lru_conv_b=lru_conv_b, lru_wa=lru_wa, lru_ba=lru_ba,
             lru_wx=lru_wx, lru_bx=lru_bx, lru_lambda=lru_lambda, od_w_in=od_w_in, od_w_out=od_w_out)
    m = dict(ln_mix_w=m_ln_mix_w, ln_mlp_w=m_ln_mlp_w, ln_ple_w=m_ln_ple_w, w_up=m_w_up, w_down=m_w_down,
             w_ple_proj=m_w_ple_proj, w_ple_gate=m_w_ple_gate, ln_final_w=m_ln_final_w, ev_w_in=m_ev_w_in,
             ev_w_out=m_ev_w_out, dn_conv_w=m_dn_conv_w, dn_a_log=m_dn_a_log, dn_dt_bias=m_dn_dt_bias,
             dn_norm_w=m_dn_norm_w, lru_conv_w=m_lru_conv_w, lru_conv_b=m_lru_conv_b, lru_wa=m_lru_wa,
             lru_ba=m_lru_ba, lru_wx=m_lru_wx, lru_bx=m_lru_bx, lru_lambda=m_lru_lambda, od_w_in=m_od_w_in,
             od_w_out=m_od_w_out)
    v = dict(ln_mix_w=v_ln_mix_w, ln_mlp_w=v_ln_mlp_w, ln_ple_w=v_ln_ple_w, w_up=v_w_up, w_down=v_w_down,
             w_ple_proj=v_w_ple_proj, w_ple_gate=v_w_ple_gate, ln_final_w=v_ln_final_w, ev_w_in=v_ev_w_in,
             ev_w_out=v_ev_w_out, dn_conv_w=v_dn_conv_w, dn_a_log=v_dn_a_log, dn_dt_bias=v_dn_dt_bias,
             dn_norm_w=v_dn_norm_w, lru_conv_w=v_lru_conv_w, lru_conv_b=v_lru_conv_b, lru_wa=v_lru_wa,
             lru_ba=v_lru_ba, lru_wx=v_lru_wx, lru_bx=v_lru_bx, lru_lambda=v_lru_lambda, od_w_in=v_od_w_in,
             od_w_out=v_od_w_out)
    me = 4 * lax.axis_index("x") + 2 * lax.axis_index("y") + lax.axis_index("c")

    big = {}
    for n in BIG:
        shards = _gather_layers(w[n].astype(BF16), "gather_" + n)
        big[n] = [s if BIG_COL_SHARDED[n] else s.reshape(-1, s.shape[2]) for s in shards]
    conv_shapes = [w[n].shape for n in SMALL_SHARDED]
    conv_g = _all_gather(_pack([w[n] for n in SMALL_SHARDED], F32), "gather_conv")
    conv_dev = [_unpack(conv_g[s], conv_shapes) for s in range(N_DEV)]
    small = {n: w[n] for n in SMALL_REPLICATED}
    for i, n in enumerate(SMALL_SHARDED):
        small[n] = jnp.concatenate([conv_dev[s][i] for s in range(N_DEV)], axis=-1)

    loss_local, (g_big, g_small, g_x) = jax.value_and_grad(_local_loss, argnums=(0, 1, 2))(
        big, small, x[0], p[:, 0], loss_target[0])
    loss = lax.psum(loss_local, ("x", "y", "c"))

    side = lax.axis_index("c").astype(jnp.int32).reshape(1)
    out = {}
    for n in BIG:
        nl, r, c = w[n].shape
        gs = [g.reshape(N_DEV, r, c) for g in g_big[n]]
        recv = _swap_pairs(gs, "swap_" + n)
        sums = [_pair_sum(g, recv, l, side, "pairsum_" + n) for l, g in enumerate(gs)]
        got = _deliver_chips(sums, "deliver_" + n)
        res = _adamw(got.reshape(4, nl * r, c), w[n].reshape(nl * r, c), m[n].reshape(nl * r, c),
                     v[n].reshape(nl * r, c), "adamw_" + n)
        for kind, buf in zip(("grad", "delta", "new_m", "new_v"), res):
            out[kind, n] = buf.reshape(nl, r, c)

    small_names = SMALL_REPLICATED + SMALL_SHARDED
    small_shapes = [small[n].shape for n in small_names]
    all_small = _all_gather(_pack([g_small[n] for n in small_names], F32), "gather_small_grads")
    total = dict(zip(small_names, _unpack(_slot_sum(all_small, "sum_small_grads"), small_shapes)))
    for n in SMALL_SHARDED:
        width = w[n].shape[-1]
        total[n] = lax.dynamic_slice_in_dim(total[n], me * width, width, axis=-1)
    own_shapes = [w[n].shape for n in small_names]
    res_small = _adamw(_pack([total[n] for n in small_names], F32)[None], _pack([w[n] for n in small_names], F32),
                       _pack([m[n] for n in small_names], F32), _pack([v[n] for n in small_names], F32), "adamw_small")
    for kind, buf in zip(("grad", "delta", "new_m", "new_v"), res_small):
        for n, a in zip(small_names, _unpack(buf, own_shapes)):
            out[kind, n] = a

    return (loss, g_x[None], *[out["grad", n] for n in WEIGHTS], *[out["delta", n] for n in WEIGHTS],
            *[out["new_m", n] for n in WEIGHTS], *[out["new_v", n] for n in WEIGHTS])
```

```python
import functools

import numpy as np
import jax
import jax.numpy as jnp
from jax import lax
from jax.experimental import pallas as pl
from jax.experimental.pallas import tpu as pltpu

F32 = jnp.float32
BF16 = jnp.bfloat16
N_DEV = 8
LANES = 128
SUBLANES = 8
VMEM_LIMIT = 56 * 1024 * 1024
PACK_COLS = 1024
NORM_EPS = 1e-6
GN_EPS = 1e-5
DN_CHUNK = 64
RET_CHUNK = 64
HEAD = 128
RET_DK = 128
RET_DV = 256
SWA_BLOCK = 128
SWA_BRANCHES = ((128, 1), (512, 4), (2048, 16))
LRU_C = 8.0
CONV_W = 4
ADAM_LR, ADAM_B1, ADAM_B2, ADAM_EPS, ADAM_WD, ADAM_STEP = 0.001, 0.9, 0.999, 1e-08, 0.01, 10
NEG = -1e30
MESH = pl.DeviceIdType.MESH


def _params(sem):
    return pltpu.CompilerParams(dimension_semantics=sem, vmem_limit_bytes=VMEM_LIMIT)


def _tile(n, cap):
    for t in (2048, 1024, 896, 768, 640, 512, 384, 256, 128, 64, 32, 16, 8):
        if t <= cap and n % t == 0:
            return t
    return n


def _bdot(a, b, dims):
    return lax.dot_general(a.astype(BF16), b.astype(BF16), (dims, ((), ())), preferred_element_type=F32)


_NN = ((1,), (0,))
_NT = ((1,), (1,))
_TN = ((0,), (0,))


def _hdot(a, b):
    return lax.dot_general(a, b, (_NN, ((), ())), precision=lax.Precision.HIGH, preferred_element_type=F32)


def _sigmoid(x):
    return jax.nn.sigmoid(x)


def _silu(x):
    return x * _sigmoid(x)


def _softplus(x):
    return jnp.maximum(x, 0.0) + jnp.log(1.0 + jnp.exp(-jnp.abs(x)))


def _gelu(x):
    return 0.5 * x * (1.0 + jnp.tanh(0.7978845608028654 * (x + 0.044715 * (x * x * x))))


def _mm_call(a, b, *, ta=False, tb=False, extras=(), epilogue=None, out_dtypes=(F32,), b_slots=False,
             out_slots=False, name):
    m, k = (a.shape[1], a.shape[0]) if ta else a.shape
    ne, no = len(extras), len(out_dtypes)
    cap_n = 512 if ne + no > 2 else 1024
    shard = b.shape[2] if b_slots else None
    if b_slots:
        n = b.shape[1] if tb else N_DEV * shard
    else:
        n = b.shape[0] if tb else b.shape[1]
    if out_slots:
        shard = n // N_DEV
    tm = _tile(m, 1024)
    tn = _tile(shard if (out_slots or (b_slots and not tb)) else n, cap_n)
    cap_k = 2048 if (a.dtype == BF16 and b.dtype == BF16) else 1024
    tk = _tile(shard if (b_slots and tb) else k, cap_k)
    nk = k // tk
    dims = ((0,) if ta else (1,), (1,) if tb else (0,))

    def body(*refs):
        a_ref, b_ref = refs[0], refs[1]
        ex = refs[2:2 + ne]
        outs = refs[2 + ne:2 + ne + no]
        acc = refs[-1]
        kk = pl.program_id(2)

        def finish(total):
            res = (total,) if epilogue is None else epilogue(total, *[e[...] for e in ex])
            for o, r in zip(outs, res):
                o[...] = r.astype(o.dtype)

        if nk == 1:
            finish(_bdot(a_ref[...], b_ref[...], dims))
        else:
            @pl.when(kk == 0)
            def _():
                acc[...] = _bdot(a_ref[...], b_ref[...], dims)

            @pl.when(kk > 0)
            def _():
                acc[...] += _bdot(a_ref[...], b_ref[...], dims)

            @pl.when(kk == nk - 1)
            def _():
                finish(acc[...])

    a_spec = pl.BlockSpec((tk, tm), lambda i, j, kk: (kk, i)) if ta else pl.BlockSpec((tm, tk), lambda i, j, kk: (i, kk))
    if b_slots and tb:
        per = shard // tk
        b_spec = pl.BlockSpec((None, tn, tk), lambda i, j, kk: (kk // per, j, kk % per))
    elif b_slots:
        per = shard // tn
        b_spec = pl.BlockSpec((None, tk, tn), lambda i, j, kk: (j // per, kk, j % per))
    elif tb:
        b_spec = pl.BlockSpec((tn, tk), lambda i, j, kk: (j, kk))
    else:
        b_spec = pl.BlockSpec((tk, tn), lambda i, j, kk: (kk, j))
    mn_spec = pl.BlockSpec((tm, tn), lambda i, j, kk: (i, j))
    if out_slots:
        per_o = shard // tn
        out_specs = [pl.BlockSpec((None, tm, tn), lambda i, j, kk: (j // per_o, i, j % per_o))]
        out_shape = [jax.ShapeDtypeStruct((N_DEV, m, shard), out_dtypes[0])]
    else:
        out_specs = [mn_spec] * no
        out_shape = [jax.ShapeDtypeStruct((m, n), d) for d in out_dtypes]
    return pl.pallas_call(
        body, name=name, grid=(m // tm, n // tn, nk),
        in_specs=[a_spec, b_spec] + [mn_spec] * ne,
        out_specs=out_specs,
        out_shape=out_shape,
        scratch_shapes=[pltpu.VMEM((tm, tn), F32)],
        compiler_params=_params(("parallel", "parallel", "arbitrary")),
    )(a, b, *extras)


def _make_mm(name, out_dtype, slots=False):
    @jax.custom_vjp
    def op(a, w):
        return _mm_call(a, w, out_dtypes=(out_dtype,), b_slots=slots, name=name + "_f")[0]

    def fwd(a, w):
        return _mm_call(a, w, out_dtypes=(out_dtype,), b_slots=slots, name=name + "_f")[0], (a, w)

    def bwd(res, dy):
        a, w = res
        dy = dy.astype(BF16)
        da = _mm_call(dy, w, tb=True, out_dtypes=(a.dtype,), b_slots=slots, name=name + "_da")[0]
        dw = _mm_call(a, dy, ta=True, out_dtypes=(w.dtype,), out_slots=slots, name=name + "_dw")[0]
        return da, dw

    op.defvjp(fwd, bwd)
    return op


def _make_mm_res(name):
    def call(a, w, h):
        return _mm_call(a, w, extras=(h,), epilogue=lambda acc, hv: (hv + acc,), out_dtypes=(F32,), name=name + "_f")[0]

    @jax.custom_vjp
    def op(a, w, h):
        return call(a, w, h)

    def fwd(a, w, h):
        return call(a, w, h), (a, w)

    def bwd(res, dy):
        a, w = res
        da = _mm_call(dy, w, tb=True, out_dtypes=(a.dtype,), name=name + "_da")[0]
        dw = _mm_call(a, dy, ta=True, out_dtypes=(w.dtype,), name=name + "_dw")[0]
        return da, dw, dy

    op.defvjp(fwd, bwd)
    return op


def _make_ffn(name):
    def forward(hn, w_up, w_down, h):
        def ep(acc):
            r = jnp.maximum(acc, 0.0)
            return acc, r * r
        u, act = _mm_call(hn, w_up, epilogue=ep, out_dtypes=(BF16, BF16), b_slots=True, name=name + "_up")
        out = _mm_call(act, w_down, extras=(h,), epilogue=lambda acc, hv: (hv + acc,), out_dtypes=(F32,), name=name + "_down")[0]
        return out, (hn, w_up, w_down, u, act)

    @jax.custom_vjp
    def op(hn, w_up, w_down, h):
        return forward(hn, w_up, w_down, h)[0]

    def bwd(res, dy):
        hn, w_up, w_down, u, act = res
        dyb = dy.astype(BF16)
        d_wdown = _mm_call(act, dyb, ta=True, out_dtypes=(w_down.dtype,), name=name + "_dwdown")[0]
        d_u = _mm_call(dyb, w_down, tb=True, extras=(u,),
                       epilogue=lambda acc, uv: (acc * (2.0 * jnp.maximum(uv.astype(F32), 0.0)),),
                       out_dtypes=(BF16,), name=name + "_du")[0]
        d_wup = _mm_call(hn, d_u, ta=True, out_dtypes=(w_up.dtype,), out_slots=True, name=name + "_dwup")[0]
        d_hn = _mm_call(d_u, w_up, tb=True, out_dtypes=(hn.dtype,), b_slots=True, name=name + "_dhn")[0]
        return d_hn, d_wup, d_wdown, dy

    op.defvjp(forward, bwd)
    return op


def _make_ple(name):
    def forward(hn, w_gate, p, w_proj, h):
        pp = _mm_call(p, w_proj, out_dtypes=(F32,), b_slots=True, name=name + "_proj")[0]
        out, gp = _mm_call(hn, w_gate, extras=(h, pp),
                           epilogue=lambda acc, hv, ppv: (hv + _sigmoid(acc) * ppv, acc),
                           out_dtypes=(F32, F32), name=name + "_gate")
        return out, (hn, w_gate, p, w_proj, gp, pp)

    @jax.custom_vjp
    def op(hn, w_gate, p, w_proj, h):
        return forward(hn, w_gate, p, w_proj, h)[0]

    def bwd(res, dy):
        hn, w_gate, p, w_proj, gp, pp = res

        def gate_grads(g, dyv, gpv, ppv):
            s = _sigmoid(gpv)
            return (dyv * ppv * s * (1.0 - s)).astype(BF16), (dyv * s).astype(BF16)

        t, d = dy.shape
        d_gp, d_pp = _rowmap_call(gate_grads, t, 1, [dy, gp, pp], [], [(d, BF16), (d, BF16)], name + "_dgate")
        d_wgate = _mm_call(hn, d_gp, ta=True, out_dtypes=(w_gate.dtype,), name=name + "_dwgate")[0]
        d_wproj = _mm_call(p, d_pp, ta=True, out_dtypes=(w_proj.dtype,), out_slots=True, name=name + "_dwproj")[0]
        d_hn = _mm_call(d_gp, w_gate, tb=True, out_dtypes=(hn.dtype,), name=name + "_dhn")[0]
        return d_hn, d_wgate, jnp.zeros_like(p), d_wproj, dy

    op.defvjp(forward, bwd)
    return op


def _row_tile(t, widths):
    return _tile(t, max(SUBLANES, (256 * 1024) // max(widths)))


def _rowmap_specs(t, g, rows, bcs, tt):
    row_specs = [pl.BlockSpec((tt, r.shape[1] // g), lambda gg, i: (i, gg)) for r in rows]
    bc_specs = []
    for b, per_group in bcs:
        if per_group:
            bc_specs.append(pl.BlockSpec((b.shape[0], b.shape[1] // g), lambda gg, i: (0, gg)))
        else:
            bc_specs.append(pl.BlockSpec(b.shape, lambda gg, i: (0, 0)))
    return row_specs, bc_specs


def _rowmap_call(fn, t, g, rows, bcs, outs, name):
    widths = [r.shape[1] // g for r in rows] + [c // g for c, _ in outs]
    tt = _row_tile(t, widths)
    nr, nb = len(rows), len(bcs)
    row_specs, bc_specs = _rowmap_specs(t, g, rows, bcs, tt)

    def body(*refs):
        vals = [r[...] for r in refs[:nr + nb]]
        res = fn(pl.program_id(0), *vals)
        for o, r in zip(refs[nr + nb:], res):
            o[...] = r.astype(o.dtype)

    return pl.pallas_call(
        body, name=name, grid=(g, t // tt),
        in_specs=row_specs + bc_specs,
        out_specs=[pl.BlockSpec((tt, c // g), lambda gg, i: (i, gg)) for c, _ in outs],
        out_shape=[jax.ShapeDtypeStruct((t, c), d) for c, d in outs],
        compiler_params=_params(("parallel", "parallel")),
    )(*rows, *[b for b, _ in bcs])


def _rowmap_bwd_call(fn, t, g, rows, bcs, cots, name, add0=None):
    widths = [r.shape[1] // g for r in rows] + [c.shape[1] // g for c in cots]
    tt = _row_tile(t, widths)
    nr, nb, nc = len(rows), len(bcs), len(cots)
    na = 0 if add0 is None else 1
    row_specs, bc_specs = _rowmap_specs(t, g, rows, bcs, tt)
    cot_specs = [pl.BlockSpec((tt, c.shape[1] // g), lambda gg, i: (i, gg)) for c in cots]
    add_specs = [] if add0 is None else [row_specs[0]]
    shared = [not per_group for _, per_group in bcs]

    def body(*refs):
        ins = refs[:nr + nb]
        cot_refs = refs[nr + nb:nr + nb + nc]
        add_refs = refs[nr + nb + nc:nr + nb + nc + na]
        d_rows = refs[nr + nb + nc + na:nr + nb + nc + na + nr]
        d_bcs = refs[nr + nb + nc + na + nr:]
        gg, i = pl.program_id(0), pl.program_id(1)
        vals = [r[...] for r in ins]
        _, vjp = jax.vjp(lambda *v: tuple(fn(gg, *v)), *vals)
        grads = vjp(tuple(c[...] for c in cot_refs))
        for j, (o, gr) in enumerate(zip(d_rows, grads[:nr])):
            if j == 0 and na:
                gr = gr + add_refs[0][...]
            o[...] = gr.astype(o.dtype)
        for o, gr, sh in zip(d_bcs, grads[nr:], shared):
            first = jnp.logical_and(i == 0, gg == 0) if sh else i == 0

            @pl.when(first)
            def _():
                o[...] = jnp.zeros_like(o)

            o[...] += gr.astype(o.dtype)

    res = pl.pallas_call(
        body, name=name, grid=(g, t // tt),
        in_specs=row_specs + bc_specs + cot_specs + add_specs,
        out_specs=row_specs + bc_specs,
        out_shape=[jax.ShapeDtypeStruct(r.shape, r.dtype) for r in rows]
        + [jax.ShapeDtypeStruct(b.shape, F32) for b, _ in bcs],
        compiler_params=_params(("arbitrary", "arbitrary")),
    )(*rows, *[b for b, _ in bcs], *cots, *([] if add0 is None else [add0]))
    return res[:nr], res[nr:]


def _make_rowmap(fn, g, n_rows, per_group, outs, name):
    def call(*args):
        rows, bcs = list(args[:n_rows]), list(zip(args[n_rows:], per_group))
        return tuple(_rowmap_call(fn, rows[0].shape[0], g, rows, bcs, outs, name + "_f"))

    @jax.custom_vjp
    def op(*args):
        return call(*args)

    def fwd(*args):
        return call(*args), args

    def bwd(args, cots):
        rows, bcs = list(args[:n_rows]), list(zip(args[n_rows:], per_group))
        d_rows, d_bcs = _rowmap_bwd_call(fn, rows[0].shape[0], g, rows, bcs, list(cots), name + "_b")
        return tuple(d_rows) + tuple(d.astype(b.dtype) for d, (b, _) in zip(d_bcs, bcs))

    op.defvjp(fwd, bwd)
    return op


def _rms_fn(g, h, w):
    y = h * lax.rsqrt(jnp.mean(h * h, axis=-1, keepdims=True) + NORM_EPS)
    return ((y * w).astype(BF16),)


def _make_norm(name):
    def call(h, w):
        return _rowmap_call(_rms_fn, h.shape[0], 1, [h], [(w, False)], [(h.shape[1], BF16)], name + "_f")[0]

    @jax.custom_vjp
    def op(h, w):
        return h, call(h, w)

    def fwd(h, w):
        return (h, call(h, w)), (h, w)

    def bwd(res, cots):
        h, w = res
        dh_pass, dhn = cots
        d_rows, d_bcs = _rowmap_bwd_call(_rms_fn, h.shape[0], 1, [h], [(w, False)], [dhn], name + "_b", add0=dh_pass)
        return d_rows[0], d_bcs[0]

    op.defvjp(fwd, bwd)
    return op


def _loss_call(h, w, target, name):
    t, d = h.shape
    tt = _row_tile(t, [d])

    def body(h_ref, w_ref, t_ref, dh_ref, dw_ref, loss_ref):
        i = pl.program_id(0)
        tgt = t_ref[...]

        def lf(hv, wv):
            y = hv * lax.rsqrt(jnp.mean(hv * hv, axis=-1, keepdims=True) + NORM_EPS) * wv
            err = y - tgt
            return 0.5 * jnp.sum(jnp.mean(err * err, axis=-1, keepdims=True))

        lv, (dh, dw) = jax.value_and_grad(lf, argnums=(0, 1))(h_ref[...], w_ref[...])
        dh_ref[...] = dh

        @pl.when(i == 0)
        def _():
            dw_ref[...] = jnp.zeros_like(dw_ref)
            loss_ref[...] = jnp.zeros_like(loss_ref)

        dw_ref[...] += dw
        loss_ref[...] += jnp.full(loss_ref.shape, lv, F32)

    row = pl.BlockSpec((tt, d), lambda i: (i, 0))
    return pl.pallas_call(
        body, name=name, grid=(t // tt,),
        in_specs=[row, pl.BlockSpec((1, d), lambda i: (0, 0)), row],
        out_specs=[row, pl.BlockSpec((1, d), lambda i: (0, 0)), pl.BlockSpec((SUBLANES, LANES), lambda i: (0, 0))],
        out_shape=[jax.ShapeDtypeStruct((t, d), F32), jax.ShapeDtypeStruct((1, d), F32),
                   jax.ShapeDtypeStruct((SUBLANES, LANES), F32)],
        compiler_params=_params(("arbitrary",)),
    )(h, w, target)


def _make_loss(name):
    @jax.custom_vjp
    def op(h, w, target):
        return _loss_call(h, w, target, name)[2][0, 0]

    def fwd(h, w, target):
        dh, dw, lv = _loss_call(h, w, target, name)
        return lv[0, 0], (dh, dw, target)

    def bwd(res, ct):
        dh, dw, target = res
        return dh * ct, dw * ct, jnp.zeros_like(target)

    op.defvjp(fwd, bwd)
    return op


def _shift_down(cur, halo, s, first):
    if s == 0:
        return cur
    r = pltpu.roll(cur, s, 0)
    p = jnp.where(first, 0.0, pltpu.roll(halo, s, 0))
    rows = lax.broadcasted_iota(jnp.int32, p.shape, 0)
    head = jnp.where(rows < s, p, r[:SUBLANES])
    return jnp.concatenate([head, r[SUBLANES:]], axis=0)


def _shift_up(cur, halo, s, last):
    if s == 0:
        return cur
    n = cur.shape[0]
    r = pltpu.roll(cur, n - s, 0)
    p = jnp.where(last, 0.0, pltpu.roll(halo, SUBLANES - s, 0))
    rows = lax.broadcasted_iota(jnp.int32, p.shape, 0)
    tail = jnp.where(rows >= SUBLANES - s, p, r[n - SUBLANES:])
    return jnp.concatenate([r[:n - SUBLANES], tail], axis=0)


def _conv_specs(t, c):
    tt, cw = _tile(t, 512), _tile(c, 512)
    per = tt // SUBLANES
    nblk = t // SUBLANES
    cur = pl.BlockSpec((tt, cw), lambda j, i: (i, j))
    prev = pl.BlockSpec((SUBLANES, cw), lambda j, i: (jnp.maximum(i * per - 1, 0), j))
    nxt = pl.BlockSpec((SUBLANES, cw), lambda j, i: (jnp.minimum((i + 1) * per, nblk - 1), j))
    wsp = pl.BlockSpec((CONV_W, cw), lambda j, i: (0, j))
    bsp = pl.BlockSpec((1, cw), lambda j, i: (0, j))
    return tt, cw, cur, prev, nxt, wsp, bsp


def _conv_call(x, w, b, name):
    t, c = x.shape
    tt, cw, cur, prev, nxt, wsp, bsp = _conv_specs(t, c)

    def body(x_ref, p_ref, w_ref, b_ref, y_ref):
        first = pl.program_id(1) == 0
        xv, pv = x_ref[...], p_ref[...]
        y = jnp.zeros_like(xv) + b_ref[...]
        for j in range(CONV_W):
            y = y + w_ref[j:j + 1, :] * _shift_down(xv, pv, CONV_W - 1 - j, first)
        y_ref[...] = y

    return pl.pallas_call(
        body, name=name, grid=(c // cw, t // tt),
        in_specs=[cur, prev, wsp, bsp], out_specs=cur,
        out_shape=jax.ShapeDtypeStruct((t, c), F32),
        compiler_params=_params(("parallel", "parallel")),
    )(x, x, w, b)


def _conv_bwd_call(x, w, dy, name):
    t, c = x.shape
    tt, cw, cur, prev, nxt, wsp, bsp = _conv_specs(t, c)
    nt = t // tt

    def body(x_ref, p_ref, w_ref, dy_ref, n_ref, dx_ref, dw_ref, db_ref):
        i = pl.program_id(1)
        first, last = i == 0, i == nt - 1
        xv, pv, dyv, nv = x_ref[...], p_ref[...], dy_ref[...], n_ref[...]

        @pl.when(first)
        def _():
            dw_ref[...] = jnp.zeros_like(dw_ref)
            db_ref[...] = jnp.zeros_like(db_ref)

        dx = jnp.zeros_like(xv)
        for j in range(CONV_W):
            s = CONV_W - 1 - j
            dx = dx + w_ref[j:j + 1, :] * _shift_up(dyv, nv, s, last)
            dw_ref[j:j + 1, :] += jnp.sum(dyv * _shift_down(xv, pv, s, first), axis=0, keepdims=True)
        dx_ref[...] = dx
        db_ref[...] += jnp.sum(dyv, axis=0, keepdims=True)

    return pl.pallas_call(
        body, name=name, grid=(c // cw, nt),
        in_specs=[cur, prev, wsp, cur, nxt], out_specs=[cur, wsp, bsp],
        out_shape=[jax.ShapeDtypeStruct((t, c), F32), jax.ShapeDtypeStruct((CONV_W, c), F32),
                   jax.ShapeDtypeStruct((1, c), F32)],
        compiler_params=_params(("arbitrary", "arbitrary")),
    )(x, x, w, dy, dy)


def _make_conv(name):
    @jax.custom_vjp
    def op(x, w, b):
        return _conv_call(x, w, b, name + "_f")

    def fwd(x, w, b):
        return _conv_call(x, w, b, name + "_f"), (x, w)

    def bwd(res, dy):
        x, w = res
        return tuple(_conv_bwd_call(x, w, dy, name + "_b"))

    op.defvjp(fwd, bwd)
    return op


def _lru_call(a, u, name):
    t, nb, ln = a.shape
    tt = _tile(t, 1024)
    blk = pl.BlockSpec((tt, nb, ln), lambda i: (i, 0, 0))

    def body(a_ref, u_ref, h_ref, carry):
        @pl.when(pl.program_id(0) == 0)
        def _():
            carry[...] = jnp.zeros_like(carry)

        def step(k, h):
            h = a_ref[k] * h + u_ref[k]
            h_ref[k] = h
            return h

        carry[...] = lax.fori_loop(0, tt, step, carry[...], unroll=8)

    return pl.pallas_call(
        body, name=name, grid=(t // tt,), in_specs=[blk, blk], out_specs=blk,
        out_shape=jax.ShapeDtypeStruct(a.shape, F32), scratch_shapes=[pltpu.VMEM((nb, ln), F32)],
        compiler_params=_params(("arbitrary",)),
    )(a, u)


def _lru_bwd_call(a, hs, dy, name):
    t, nb, ln = a.shape
    tt = _tile(t, 1024)
    nt = t // tt
    blk = pl.BlockSpec((tt, nb, ln), lambda i: (nt - 1 - i, 0, 0))
    prev = pl.BlockSpec((1, nb, ln), lambda i: (jnp.maximum((nt - 1 - i) * tt - 1, 0), 0, 0))

    def body(a_ref, h_ref, hp_ref, dy_ref, da_ref, du_ref, carry):
        i = pl.program_id(0)

        @pl.when(i == 0)
        def _():
            carry[...] = jnp.zeros_like(carry)

        h_before = jnp.where(i == nt - 1, 0.0, hp_ref[0])

        def step(k, c):
            r = tt - 1 - k
            dh = dy_ref[r] + c
            du_ref[r] = dh
            da_ref[r] = dh * h_ref[jnp.maximum(r - 1, 0)]
            return a_ref[r] * dh

        carry[...] = lax.fori_loop(0, tt, step, carry[...], unroll=8)
        da_ref[0] = du_ref[0] * h_before

    return pl.pallas_call(
        body, name=name, grid=(nt,), in_specs=[blk, blk, prev, blk], out_specs=[blk, blk],
        out_shape=[jax.ShapeDtypeStruct(a.shape, F32), jax.ShapeDtypeStruct(a.shape, F32)],
        scratch_shapes=[pltpu.VMEM((nb, ln), F32)],
        compiler_params=_params(("arbitrary",)),
    )(a, hs, hs, dy)


def _make_lru(name):
    @jax.custom_vjp
    def op(a, u):
        return _lru_call(a, u, name + "_f")

    def fwd(a, u):
        hs = _lru_call(a, u, name + "_f")
        return hs, (a, hs)

    def bwd(res, dy):
        a, hs = res
        return tuple(_lru_bwd_call(a, hs, dy, name + "_b"))

    op.defvjp(fwd, bwd)
    return op


def _scan_specs(ins, const, heads, hp, chunk, rev_n):
    def tmap(n_of):
        return lambda hg, n: (n_of(n), hg)
    n_of = (lambda n: rev_n - 1 - n) if rev_n else (lambda n: n)
    in_specs = [pl.BlockSpec((chunk, hp * (x.shape[1] // heads)), tmap(n_of)) for x in ins]
    c_spec = pl.BlockSpec((1, hp * (const.shape[1] // heads)), lambda hg, n: (0, hg))
    return in_specs, c_spec, n_of


def _scan_call(chunk_fn, ins, const, heads, hp, chunk, state_shape, out_width, name):
    t = ins[0].shape[0]
    nc = t // chunk
    ni = len(ins)
    in_specs, c_spec, _ = _scan_specs(ins, const, heads, hp, chunk, 0)
    ws = [x.shape[1] // heads for x in ins]
    cw = const.shape[1] // heads
    dk, dv = state_shape

    def body(*refs):
        in_refs, c_ref, o_ref, s_ref, state = refs[:ni], refs[ni], refs[ni + 1], refs[ni + 2], refs[ni + 3]

        @pl.when(pl.program_id(1) == 0)
        def _():
            state[...] = jnp.zeros_like(state)

        vals = [[r[:, k * w:(k + 1) * w] for r, w in zip(in_refs, ws)] for k in range(hp)]
        consts = [c_ref[:, k * cw:(k + 1) * cw] for k in range(hp)]
        s0 = [state[k] for k in range(hp)]
        outs, s1 = chunk_fn(vals, consts, s0)
        for k in range(hp):
            s_ref[0, k] = s0[k]
            o_ref[:, k * out_width:(k + 1) * out_width] = outs[k]
            state[k] = s1[k]

    return pl.pallas_call(
        body, name=name, grid=(heads // hp, nc),
        in_specs=in_specs + [c_spec],
        out_specs=[pl.BlockSpec((chunk, hp * out_width), lambda hg, n: (n, hg)),
                   pl.BlockSpec((1, hp, dk, dv), lambda hg, n: (n, hg, 0, 0))],
        out_shape=[jax.ShapeDtypeStruct((t, heads * out_width), F32),
                   jax.ShapeDtypeStruct((nc, heads, dk, dv), F32)],
        scratch_shapes=[pltpu.VMEM((hp, dk, dv), F32)],
        compiler_params=_params(("parallel", "arbitrary")),
    )(*ins, const)


def _scan_bwd_call(chunk_fn, ins, const, states, d_out, heads, hp, chunk, state_shape, out_width, name):
    t = ins[0].shape[0]
    nc = t // chunk
    ni = len(ins)
    in_specs, c_spec, n_of = _scan_specs(ins, const, heads, hp, chunk, nc)
    ws = [x.shape[1] // heads for x in ins]
    cw = const.shape[1] // heads
    dk, dv = state_shape

    def body(*refs):
        in_refs, c_ref, s_ref, do_ref = refs[:ni], refs[ni], refs[ni + 1], refs[ni + 2]
        d_refs, dstate = refs[ni + 3:ni + 3 + ni], refs[-1]

        @pl.when(pl.program_id(1) == 0)
        def _():
            dstate[...] = jnp.zeros_like(dstate)

        vals = [[r[:, k * w:(k + 1) * w] for r, w in zip(in_refs, ws)] for k in range(hp)]
        consts = [c_ref[:, k * cw:(k + 1) * cw] for k in range(hp)]
        s0 = [s_ref[0, k] for k in range(hp)]
        d_o = [do_ref[:, k * out_width:(k + 1) * out_width] for k in range(hp)]
        d_s1 = [dstate[k] for k in range(hp)]
        _, vjp = jax.vjp(lambda vv, ss: chunk_fn(vv, consts, ss), vals, s0)
        d_vals, d_s0 = vjp((d_o, d_s1))
        for k in range(hp):
            for r, w, gr in zip(d_refs, ws, d_vals[k]):
                r[:, k * w:(k + 1) * w] = gr
            dstate[k] = d_s0[k]

    return pl.pallas_call(
        body, name=name, grid=(heads // hp, nc),
        in_specs=in_specs + [c_spec,
                             pl.BlockSpec((1, hp, dk, dv), lambda hg, n: (n_of(n), hg, 0, 0)),
                             pl.BlockSpec((chunk, hp * out_width), lambda hg, n: (n_of(n), hg))],
        out_specs=in_specs,
        out_shape=[jax.ShapeDtypeStruct(x.shape, F32) for x in ins],
        scratch_shapes=[pltpu.VMEM((hp, dk, dv), F32)],
        compiler_params=_params(("parallel", "arbitrary")),
    )(*ins, const, states, d_out)


def _make_scan(chunk_fn, const, heads, hp, chunk, state_shape, out_width, name):
    def call(*ins):
        return _scan_call(chunk_fn, list(ins), const, heads, hp, chunk, state_shape, out_width, name + "_f")

    @jax.custom_vjp
    def op(*ins):
        return call(*ins)[0]

    def fwd(*ins):
        o, states = call(*ins)
        return o, (ins, states)

    def bwd(res, d_out):
        ins, states = res
        return tuple(_scan_bwd_call(chunk_fn, list(ins), const, states, d_out, heads, hp, chunk, state_shape,
                                    out_width, name + "_b"))

    op.defvjp(fwd, bwd)
    return op


def _tri(c):
    ri = lax.broadcasted_iota(jnp.int32, (c, c), 0)
    ci = lax.broadcasted_iota(jnp.int32, (c, c), 1)
    return ri, ci


def _each(fn, *lists):
    return [fn(*a) for a in zip(*lists)]


@jax.custom_vjp
def _neumann_inverses(xs):
    c = xs[0].shape[0]
    ri, ci = _tri(c)
    eye = jnp.where(ri == ci, 1.0, 0.0)
    invs = [eye + x for x in xs]
    xps = list(xs)
    for _ in range(max(1, int(np.ceil(np.log2(c))) - 1)):
        xps = [_hdot(xp, xp) for xp in xps]
        invs = [inv + _hdot(inv, xp) for inv, xp in zip(invs, xps)]
    return invs


def _neumann_inverses_fwd(xs):
    invs = _neumann_inverses(xs)
    return invs, invs


def _neumann_inverses_bwd(invs, ds):
    hi = lax.Precision.HIGH
    ts = [lax.dot_general(d, inv, (_NT, ((), ())), precision=hi, preferred_element_type=F32)
          for d, inv in zip(ds, invs)]
    return ([lax.dot_general(inv, t, (_TN, ((), ())), precision=hi, preferred_element_type=F32)
             for inv, t in zip(invs, ts)],)


_neumann_inverses.defvjp(_neumann_inverses_fwd, _neumann_inverses_bwd)


def _dn_chunk(vals, consts, ss):
    del consts
    qs, ks, vs, gbs, bbs = (list(x) for x in zip(*vals))
    c = qs[0].shape[0]
    ri, ci = _tri(c)
    causal, strict = ri >= ci, ri > ci
    tri_f = causal.astype(F32)
    gc_b = [_hdot(tri_f, gb) for gb in gbs]
    gcol = [jnp.mean(x, axis=1, keepdims=True) for x in gc_b]
    grow = [jnp.mean(x.T, axis=0, keepdims=True) for x in gc_b]
    bcol = [jnp.mean(bb, axis=1, keepdims=True) for bb in bbs]
    decay = _each(lambda a, b: jnp.where(causal, jnp.exp(jnp.where(causal, a - b, 0.0)), 0.0), gcol, grow)
    kb = _each(lambda k, b: k * b, ks, bcol)
    m = _each(lambda a, k: _bdot(a, k, _NT), kb, ks)
    invs = _neumann_inverses(_each(lambda mm, d: -jnp.where(strict, mm * d, 0.0), m, decay))
    eg = [jnp.exp(a) for a in gcol]
    u = _each(lambda inv, v, b: _hdot(inv, v * b), invs, vs, bcol)
    w = _each(lambda inv, a, e: _hdot(inv, a * e), invs, kb, eg)
    qk = _each(lambda q, k, d: _bdot(q, k, _NT) * d, qs, ks, decay)
    g_last = [jnp.sum(jnp.mean(gb, axis=1, keepdims=True), axis=0, keepdims=True) for gb in gbs]
    k_dec = _each(lambda k, gl, a: k * jnp.exp(gl - a), ks, g_last, gcol)
    ws = _each(lambda w_, s: _bdot(w_, s, _NN), w, ss)
    v_new = _each(lambda u_, x: u_ - x, u, ws)
    o_state = _each(lambda q, e, s: _bdot(q * e, s, _NN), qs, eg, ss)
    o_intra = _each(lambda a, vn: _bdot(a, vn, _NN), qk, v_new)
    s_add = _each(lambda kd, vn: _bdot(kd, vn, _TN), k_dec, v_new)
    outs = _each(lambda a, b: a + b, o_state, o_intra)
    s_new = _each(lambda s, gl, a: s * jnp.exp(gl) + a, ss, g_last, s_add)
    return outs, s_new


def _ret_chunk(vals, consts, ss):
    qs, ks, vs = (list(x) for x in zip(*vals))
    c = qs[0].shape[0]
    ri, ci = _tri(c)
    rel = (ri - ci).astype(F32)
    idx = lax.broadcasted_iota(jnp.int32, (c, 1), 0).astype(F32)
    lg = [jnp.mean(x, axis=1, keepdims=True) for x in consts]
    dmask = [jnp.where(rel >= 0, jnp.exp(jnp.maximum(rel, 0.0) * a), 0.0) for a in lg]
    ksc = [k * (RET_DK ** -0.5) for k in ks]
    qk = _each(lambda q, k, d: _bdot(q, k, _NT) * d, qs, ksc, dmask)
    intra = _each(lambda a, v: _bdot(a, v, _NN), qk, vs)
    inter = _each(lambda q, a, s: _bdot(q * jnp.exp((idx + 1.0) * a), s, _NN), qs, lg, ss)
    s_add = _each(lambda k, a, v: _bdot(k * jnp.exp((c - 1.0 - idx) * a), v, _TN), ksc, lg, vs)
    outs = _each(lambda a, b: a + b, intra, inter)
    s_new = _each(lambda s, a, x: s * jnp.exp(c * a) + x, ss, lg, s_add)
    return outs, s_new


def _attn_block(args, slopes, first):
    qs, kps, kcs, vps, vcs = (list(x) for x in zip(*args))
    b = qs[0].shape[0]
    ri, ci = _tri(b)
    rel_c = ri - ci
    rel_p = rel_c + b
    ok_c = rel_c >= 0
    ok_p = jnp.logical_and(rel_p <= b, jnp.logical_not(first))
    rel_cf, rel_pf = rel_c.astype(F32), rel_p.astype(F32)
    qsc = [q * (HEAD ** -0.5) for q in qs]
    s_c = _each(lambda q, k, sl: jnp.where(ok_c, _bdot(q, k, _NT) - sl * rel_cf, NEG), qsc, kcs, slopes)
    s_p = _each(lambda q, k, sl: jnp.where(ok_p, _bdot(q, k, _NT) - sl * rel_pf, NEG), qsc, kps, slopes)
    mx = _each(lambda a, c: lax.stop_gradient(jnp.maximum(jnp.max(a, axis=1, keepdims=True),
                                                          jnp.max(c, axis=1, keepdims=True))), s_c, s_p)
    p_c = _each(lambda a, m: jnp.exp(a - m), s_c, mx)
    p_p = _each(lambda a, m: jnp.exp(a - m), s_p, mx)
    den = _each(lambda a, c: jnp.sum(a, axis=1, keepdims=True) + jnp.sum(c, axis=1, keepdims=True), p_c, p_p)
    o_c = _each(lambda a, v: _bdot(a, v, _NN), p_c, vcs)
    o_p = _each(lambda a, v: _bdot(a, v, _NN), p_p, vps)
    outs = _each(lambda a, c, d: (a + c) / d, o_c, o_p, den)
    lses = _each(lambda m, d, o: jnp.broadcast_to(m + jnp.log(d), o.shape), mx, den, outs)
    return outs, lses


ATTN_GROUP = 4


def _attn_plan(t, hw, dil):
    nh = hw // HEAD
    hb = 1 if dil > 1 else max(h for h in (4, 2, 1) if nh % h == 0)
    units = [(r, h) for r in range(dil) for h in range(hb)]
    groups = [units[i:i + ATTN_GROUP] for i in range(0, len(units), ATTN_GROUP)]
    return hb, SWA_BLOCK * dil, groups


class _unit:
    def __init__(self, ref, r, h, dil):
        rows = pl.ds(0, SWA_BLOCK) if dil == 1 else pl.ds(r, SWA_BLOCK, stride=dil)
        self.ref, self.idx = ref, (rows, pl.ds(h * HEAD, HEAD))

    def __getitem__(self, _):
        return self.ref[self.idx]

    def __setitem__(self, _, value):
        self.ref[self.idx] = value


def _attn_call(q, k, v, slopes, dil, name):
    t, hw = q.shape
    hb, rows, groups = _attn_plan(t, hw, dil)
    wide = hb * HEAD
    cur = pl.BlockSpec((rows, wide), lambda j, n: (n, j))
    prev = pl.BlockSpec((rows, wide), lambda j, n: (jnp.maximum(n - 1, 0), j))
    ssp = pl.BlockSpec((1, wide), lambda j, n: (0, j))

    def body(q_ref, kp_ref, kc_ref, vp_ref, vc_ref, s_ref, o_ref, l_ref):
        first = pl.program_id(1) == 0
        for grp in groups:
            args = [[_unit(x, r, h, dil)[...] for x in (q_ref, kp_ref, kc_ref, vp_ref, vc_ref)] for r, h in grp]
            outs, lses = _attn_block(args, [s_ref[:, h * HEAD:(h + 1) * HEAD] for _, h in grp], first)
            for (r, h), o, lse in zip(grp, outs, lses):
                _unit(o_ref, r, h, dil)[...] = o
                _unit(l_ref, r, h, dil)[...] = lse

    return pl.pallas_call(
        body, name=name, grid=(hw // wide, t // rows),
        in_specs=[cur, prev, cur, prev, cur, ssp], out_specs=[cur, cur],
        out_shape=[jax.ShapeDtypeStruct((t, hw), F32), jax.ShapeDtypeStruct((t, hw), F32)],
        compiler_params=_params(("parallel", "parallel")),
    )(q, k, k, v, v, slopes)


def _attn_bwd_call(q, k, v, slopes, d_o, d_l, dil, name):
    t, hw = q.shape
    hb, rows, groups = _attn_plan(t, hw, dil)
    wide = hb * HEAD
    nb = t // rows
    cur = pl.BlockSpec((rows, wide), lambda j, n: (jnp.minimum(n, nb - 1), j))
    prev = pl.BlockSpec((rows, wide), lambda j, n: (jnp.clip(n - 1, 0, nb - 1), j))
    ssp = pl.BlockSpec((1, wide), lambda j, n: (0, j))

    def body(q_ref, kp_ref, kc_ref, vp_ref, vc_ref, s_ref, do_ref, dl_ref, dq_ref, dk_ref, dv_ref, ck, cv):
        n = pl.program_id(1)

        @pl.when(n == 0)
        def _():
            ck[...] = jnp.zeros_like(ck)
            cv[...] = jnp.zeros_like(cv)

        @pl.when(n < nb)
        def _():
            first = n == 0
            for grp in groups:
                args = [[_unit(x, r, h, dil)[...] for x in (q_ref, kp_ref, kc_ref, vp_ref, vc_ref)] for r, h in grp]
                svs = [s_ref[:, h * HEAD:(h + 1) * HEAD] for _, h in grp]
                cots = ([_unit(do_ref, r, h, dil)[...] for r, h in grp], [_unit(dl_ref, r, h, dil)[...] for r, h in grp])
                carry = [(_unit(ck, r, h, dil)[...], _unit(cv, r, h, dil)[...]) for r, h in grp]
                _, vjp = jax.vjp(lambda a, svs=svs: _attn_block(a, svs, first), args)
                grads = vjp(cots)[0]
                for (r, h), (dq, dkp, dkc, dvp, dvc), (c_k, c_v) in zip(grp, grads, carry):
                    _unit(dq_ref, r, h, dil)[...] = dq
                    _unit(dk_ref, r, h, dil)[...] = c_k + dkp
                    _unit(dv_ref, r, h, dil)[...] = c_v + dvp
                    _unit(ck, r, h, dil)[...] = dkc
                    _unit(cv, r, h, dil)[...] = dvc

        @pl.when(n == nb)
        def _():
            dk_ref[...] = ck[...]
            dv_ref[...] = cv[...]

    return pl.pallas_call(
        body, name=name, grid=(hw // wide, nb + 1),
        in_specs=[cur, prev, cur, prev, cur, ssp, cur, cur], out_specs=[cur, prev, prev],
        out_shape=[jax.ShapeDtypeStruct((t, hw), F32)] * 3,
        scratch_shapes=[pltpu.VMEM((rows, wide), F32), pltpu.VMEM((rows, wide), F32)],
        compiler_params=_params(("parallel", "arbitrary")),
    )(q, k, k, v, v, slopes, d_o, d_l)


def _make_attn(slopes, dil, name):
    @jax.custom_vjp
    def op(q, k, v):
        return tuple(_attn_call(q, k, v, slopes, dil, name + "_f"))

    def fwd(q, k, v):
        return tuple(_attn_call(q, k, v, slopes, dil, name + "_f")), (q, k, v)

    def bwd(res, cots):
        q, k, v = res
        return tuple(_attn_bwd_call(q, k, v, slopes, cots[0], cots[1], dil, name + "_b"))

    op.defvjp(fwd, bwd)
    return op


def _dn_pre_fn(g, cq, ck, cv):
    sq, sk, sv = _silu(cq), _silu(ck), _silu(cv)
    qn = sq * lax.rsqrt(jnp.sum(sq * sq, axis=-1, keepdims=True) + 1e-6) * (HEAD ** -0.5)
    kn = sk * lax.rsqrt(jnp.sum(sk * sk, axis=-1, keepdims=True) + 1e-6)
    return qn, kn, sv


def _make_dn_gates_fn(heads):
    def fn(g, ba, a_log, dt_bias):
        lane = lax.broadcasted_iota(jnp.int32, ba.shape, 1)
        lane1 = lax.broadcasted_iota(jnp.int32, a_log.shape, 1)
        betas, gs = [], []
        for h in range(heads):
            b_raw = jnp.sum(jnp.where(lane == h, ba, 0.0), axis=1, keepdims=True)
            a_raw = jnp.sum(jnp.where(lane == heads + h, ba, 0.0), axis=1, keepdims=True)
            al = jnp.sum(jnp.where(lane1 == h, a_log, 0.0), axis=1, keepdims=True)
            dt = jnp.sum(jnp.where(lane1 == h, dt_bias, 0.0), axis=1, keepdims=True)
            beta = _sigmoid(b_raw)
            gl = -jnp.exp(al) * _softplus(a_raw + dt)
            betas.append(jnp.broadcast_to(beta, ba.shape))
            gs.append(jnp.broadcast_to(gl, ba.shape))
        return jnp.concatenate(betas, axis=1), jnp.concatenate(gs, axis=1)
    return fn


def _dn_post_fn(g, o, z, w):
    y = o * lax.rsqrt(jnp.mean(o * o, axis=-1, keepdims=True) + NORM_EPS) * w
    return ((y * _silu(z)).astype(BF16),)


def _lru_pre_fn(g, xc, wa, wx, ba, bx, lam):
    r = _sigmoid(_bdot(xc, wa, _NN) + ba)
    i = _sigmoid(_bdot(xc, wx, _NN) + bx)
    log_a = -LRU_C * r * _softplus(-lam)
    a = jnp.exp(log_a)
    u = jnp.sqrt(1.0 - jnp.exp(2.0 * log_a)) * (i * xc)
    return a, u


def _lru_post_fn(g, hs, yr):
    return ((hs * _gelu(yr)).astype(BF16),)


def _merge_fn(g, o1, o2, o3, l1, l2, l3):
    m = lax.stop_gradient(jnp.maximum(jnp.maximum(l1, l2), l3))
    e1, e2, e3 = jnp.exp(l1 - m), jnp.exp(l2 - m), jnp.exp(l3 - m)
    return (((e1 * o1 + e2 * o2 + e3 * o3) / (e1 + e2 + e3)).astype(BF16),)


def _ret_post_fn(g, o, gate):
    mu = jnp.mean(o, axis=-1, keepdims=True)
    xc = o - mu
    y = xc * lax.rsqrt(jnp.mean(xc * xc, axis=-1, keepdims=True) + GN_EPS)
    return ((y * _silu(gate)).astype(BF16),)


def _pad_lanes(v):
    return jnp.pad(v, (0, LANES - v.shape[0]))[None, :]


def _even_layout(half):
    heads = half // HEAD
    qkv = 3 * half
    segs = [(0, qkv, qkv), (qkv, half, half), (qkv + half, 2 * heads, LANES),
            (qkv + half + 2 * heads, half, half), (qkv + 2 * half + 2 * heads, half, half)]
    return segs


def _pad_ev_w_in(w, half):
    parts = []
    for start, width, padded in _even_layout(half):
        part = w[:, start:start + width]
        if padded != width:
            part = jnp.pad(part, ((0, 0), (0, padded - width)))
        parts.append(part)
    return jnp.concatenate(parts, axis=1)


def _even_mixer(hn, h, lw):
    t, d = h.shape
    half = d // 2
    heads = half // HEAD
    hp = 4 if heads % 4 == 0 else 1
    w_in = lw["w_in"].transpose(1, 0, 2).reshape(d, -1)
    proj = _make_mm("ev_in", F32)(hn, _pad_ev_w_in(w_in, half))
    o0 = 0
    segs = []
    for _, _, padded in _even_layout(half):
        segs.append(proj[:, o0:o0 + padded])
        o0 += padded
    qkv, z, ba, xr, yr = segs
    c = _make_conv("dn_conv")(qkv, lw["dn_conv_w"], jnp.zeros((1, 3 * half), F32))
    q, k, v = _make_rowmap(_dn_pre_fn, heads, 3, [], [(half, F32)] * 3, "dn_pre")(
        c[:, :half], c[:, half:2 * half], c[:, 2 * half:])
    beta_b, g_b = _make_rowmap(_make_dn_gates_fn(heads), 1, 1, [False, False], [(half, F32)] * 2, "dn_gates")(
        ba, _pad_lanes(lw["dn_a_log"]), _pad_lanes(lw["dn_dt_bias"]))
    o = _make_scan(_dn_chunk, jnp.zeros((1, heads * LANES), F32), heads, hp, DN_CHUNK, (HEAD, HEAD), HEAD, "dn_core")(
        q, k, v, g_b, beta_b)
    ya = _make_rowmap(_dn_post_fn, heads, 2, [False], [(half, BF16)], "dn_post")(o, z, lw["dn_norm_w"][None, :])[0]
    nblk = lw["lru_wa"].shape[0]
    xc = _make_conv("lru_conv")(xr, lw["lru_conv_w"], lw["lru_conv_b"][None, :])
    wa = lw["lru_wa"].transpose(1, 0, 2).reshape(HEAD, nblk * HEAD)
    wx = lw["lru_wx"].transpose(1, 0, 2).reshape(HEAD, nblk * HEAD)
    a, u = _make_rowmap(_lru_pre_fn, nblk, 1, [True] * 5, [(half, F32)] * 2, "lru_pre")(
        xc, wa, wx, lw["lru_ba"][None, :], lw["lru_bx"][None, :], lw["lru_lambda"][None, :])
    hs = _make_lru("lru_scan")(a.reshape(t, nblk, HEAD), u.reshape(t, nblk, HEAD)).reshape(t, half)
    yb = _make_rowmap(_lru_post_fn, nblk, 2, [], [(half, BF16)], "lru_post")(hs, yr)[0]
    return _make_mm_res("ev_out")(jnp.concatenate([ya, yb], axis=1), lw["w_out"], h)


def _odd_mixer(hn, h, lw):
    t, d = h.shape
    half = d // 2
    heads = half // HEAD
    rheads = half // RET_DV
    rqk = rheads * RET_DK
    proj = _make_mm("od_in", F32, slots=True)(hn, lw["w_in"])
    cq, ck, cv = proj[:, :half], proj[:, half:2 * half], proj[:, 2 * half:3 * half]
    o1 = 3 * half
    rq, rk = proj[:, o1:o1 + rqk], proj[:, o1 + rqk:o1 + 2 * rqk]
    rv, rg = proj[:, o1 + 2 * rqk:o1 + 2 * rqk + half], proj[:, o1 + 2 * rqk + half:]
    slopes = np.exp2(-8.0 * np.arange(1, heads + 1, dtype=np.float64) / heads)
    outs, lses = [], []
    for window, dil in SWA_BRANCHES:
        assert window // dil == SWA_BLOCK and (t // dil) % SWA_BLOCK == 0
        sl = jnp.asarray(np.repeat(slopes * dil, HEAD)[None, :], F32)
        o_i, l_i = _make_attn(sl, dil, "swa_d%d" % dil)(cq, ck, cv)
        outs.append(o_i)
        lses.append(l_i)
    yc = _make_rowmap(_merge_fn, heads, 6, [], [(half, BF16)], "swa_merge")(*outs, *lses)[0]
    lg = np.log1p(-np.exp2(-5.0 - np.arange(rheads, dtype=np.float64)))
    lg_b = jnp.asarray(np.repeat(lg, LANES)[None, :], F32)
    hp = 4 if rheads % 4 == 0 else 1
    o_r = _make_scan(_ret_chunk, lg_b, rheads, hp, RET_CHUNK, (RET_DK, RET_DV), RET_DV, "ret_core")(rq, rk, rv)
    yd = _make_rowmap(_ret_post_fn, rheads, 2, [], [(half, BF16)], "ret_post")(o_r, rg)[0]
    return _make_mm_res("od_out")(jnp.concatenate([yc, yd], axis=1), lw["w_out"], h)


def _local_loss(big, small, x, p, target):
    depth = len(big["w_up"])
    h = x
    for i in range(depth):
        j = i // 2
        h, hn = _make_norm("ln_mix")(h, small["ln_mix_w"][i][None, :])
        if i % 2 == 0:
            lw = {"w_in": big["ev_w_in"][j], "w_out": big["ev_w_out"][j]}
            for nm in ("dn_conv_w", "dn_a_log", "dn_dt_bias", "dn_norm_w", "lru_conv_w", "lru_conv_b", "lru_wa",
                       "lru_ba", "lru_wx", "lru_bx", "lru_lambda"):
                lw[nm] = small[nm][j]
            h = _even_mixer(hn, h, lw)
        else:
            h = _odd_mixer(hn, h, {"w_in": big["od_w_in"][j], "w_out": big["od_w_out"][j]})
        h, hn = _make_norm("ln_mlp")(h, small["ln_mlp_w"][i][None, :])
        h = _make_ffn("ffn")(hn, big["w_up"][i], big["w_down"][i], h)
        h, hn = _make_norm("ln_ple")(h, small["ln_ple_w"][i][None, :])
        h = _make_ple("ple")(hn, big["w_ple_gate"][i], p[i], big["w_ple_proj"][i], h)
    return _make_loss("loss_head")(h, small["ln_final_w"][None, :], target)


def _all_gather(x, name):
    r, c = x.shape

    def body(x_ref, out_ref, send_sems, recv_sems, local_sem):
        mx, my, mc = lax.axis_index("x"), lax.axis_index("y"), lax.axis_index("c")
        me, sibling = (mx, my, mc), (mx, my, 1 - mc)
        chips = [(1 - mx, my), (mx, 1 - my), (1 - mx, 1 - my)]

        def slot(px, py, pc):
            return out_ref.at[4 * px + 2 * py + pc]

        def copy(k, block, to, src=None):
            return pltpu.make_async_remote_copy(
                src_ref=slot(*block) if src is None else src, dst_ref=slot(*block),
                send_sem=send_sems.at[k], recv_sem=recv_sems.at[k], device_id=to, device_id_type=MESH)

        mine = pltpu.make_async_copy(x_ref, slot(*me), local_sem)
        mine.start()
        first = [copy(0, me, sibling, src=x_ref)]
        first += [copy(1 + j, me, (*chip, mc), src=x_ref) for j, chip in enumerate(chips)]
        for cp in first:
            cp.start()
        passed = [copy(4 + j, (*chip, mc), sibling) for j, chip in enumerate(chips)]
        for j, chip in enumerate(chips):
            copy(1 + j, (*chip, mc), me).wait_recv()
            passed[j].start()
        copy(0, sibling, me).wait_recv()
        for j, chip in enumerate(chips):
            copy(4 + j, (*chip, 1 - mc), me).wait_recv()
        for cp in first + passed:
            cp.wait_send()
        mine.wait()

    return pl.pallas_call(
        body, name=name,
        out_shape=jax.ShapeDtypeStruct((N_DEV, r, c), x.dtype),
        in_specs=[pl.BlockSpec(memory_space=pl.ANY)],
        out_specs=pl.BlockSpec(memory_space=pl.ANY),
        scratch_shapes=[pltpu.SemaphoreType.DMA((7,)), pltpu.SemaphoreType.DMA((7,)), pltpu.SemaphoreType.DMA],
    )(x)


def _gather_layers(x, name):
    n, r, c = x.shape

    def body(x_ref, *rest):
        outs, (send_sems, recv_sems, local_sems) = rest[:n], rest[n:]
        mx, my, mc = lax.axis_index("x"), lax.axis_index("y"), lax.axis_index("c")
        me, sibling = (mx, my, mc), (mx, my, 1 - mc)
        chips = [(1 - mx, my), (mx, 1 - my), (1 - mx, 1 - my)]

        def slot(l, px, py, pc):
            return outs[l].at[4 * px + 2 * py + pc]

        def copy(k, l, block, to, from_shard=False):
            return pltpu.make_async_remote_copy(
                src_ref=x_ref.at[l] if from_shard else slot(l, *block), dst_ref=slot(l, *block),
                send_sem=send_sems.at[k, l], recv_sem=recv_sems.at[k, l], device_id=to, device_id_type=MESH)

        mine = [pltpu.make_async_copy(x_ref.at[l], slot(l, *me), local_sems.at[l]) for l in range(n)]
        for cp in mine:
            cp.start()
        sent = [copy(0, l, me, sibling, True) for l in range(n)]
        sent += [copy(1 + j, l, me, (*chip, mc), True) for j, chip in enumerate(chips) for l in range(n)]
        for cp in sent:
            cp.start()
        for j, chip in enumerate(chips):
            for l in range(n):
                copy(1 + j, l, (*chip, mc), me).wait_recv()
                passed = copy(4 + j, l, (*chip, mc), sibling)
                passed.start()
                sent.append(passed)
        for l in range(n):
            copy(0, l, sibling, me).wait_recv()
        for j, chip in enumerate(chips):
            for l in range(n):
                copy(4 + j, l, (*chip, 1 - mc), me).wait_recv()
        for cp in sent:
            cp.wait_send()
        for cp in mine:
            cp.wait()

    return pl.pallas_call(
        body, name=name,
        out_shape=[jax.ShapeDtypeStruct((N_DEV, r, c), x.dtype)] * n,
        in_specs=[pl.BlockSpec(memory_space=pl.ANY)],
        out_specs=[pl.BlockSpec(memory_space=pl.ANY)] * n,
        scratch_shapes=[pltpu.SemaphoreType.DMA((7, n)), pltpu.SemaphoreType.DMA((7, n)),
                        pltpu.SemaphoreType.DMA((n,))],
    )(x)


def _swap_pairs(gs, name):
    n = len(gs)
    _, r, c = gs[0].shape

    def body(*refs):
        g_refs, out_ref, send_sems, recv_sems = refs[:n], refs[n], refs[n + 1], refs[n + 2]
        mx, my, mc = lax.axis_index("x"), lax.axis_index("y"), lax.axis_index("c")

        def copy(q, l):
            return pltpu.make_async_remote_copy(
                src_ref=g_refs[l].at[2 * q + (1 - mc)], dst_ref=out_ref.at[q, l],
                send_sem=send_sems.at[q, l], recv_sem=recv_sems.at[q, l],
                device_id=(mx, my, 1 - mc), device_id_type=MESH)

        copies = [copy(q, l) for q in range(4) for l in range(n)]
        for cp in copies:
            cp.start()
        for cp in copies:
            cp.wait_recv()
        for cp in copies:
            cp.wait_send()

    return pl.pallas_call(
        body, name=name,
        out_shape=jax.ShapeDtypeStruct((4, n, r, c), gs[0].dtype),
        in_specs=[pl.BlockSpec(memory_space=pl.ANY)] * n,
        out_specs=pl.BlockSpec(memory_space=pl.ANY),
        scratch_shapes=[pltpu.SemaphoreType.DMA((4, n)), pltpu.SemaphoreType.DMA((4, n))],
    )(*gs)


def _pair_sum(g, recv, layer, side, name):
    _, r, c = g.shape
    tr = _tile(r, max(SUBLANES, (256 * 1024) // c))

    def body(side_ref, g_ref, r_ref, o_ref):
        del side_ref
        o_ref[...] = (g_ref[...].astype(F32) + r_ref[...].astype(F32)).astype(o_ref.dtype)

    return pl.pallas_call(
        body, name=name,
        grid_spec=pltpu.PrefetchScalarGridSpec(
            num_scalar_prefetch=1, grid=(4, r // tr),
            in_specs=[pl.BlockSpec((None, None, tr, c), lambda q, t, side_ref: (q, side_ref[0], t, 0)),
                      pl.BlockSpec((None, None, tr, c), lambda q, t, side_ref: (q, layer, t, 0))],
            out_specs=pl.BlockSpec((None, tr, c), lambda q, t, side_ref: (q, t, 0))),
        out_shape=jax.ShapeDtypeStruct((4, r, c), g.dtype),
        compiler_params=_params(("parallel", "parallel")),
    )(side, g.reshape(4, 2, r, c), recv)


def _deliver_chips(ps, name):
    n = len(ps)
    _, r, c = ps[0].shape

    def body(*refs):
        p_refs, out_ref, send_sems, recv_sems, local_sems = refs[:n], refs[n], refs[n + 1], refs[n + 2], refs[n + 3]
        mx, my, mc = lax.axis_index("x"), lax.axis_index("y"), lax.axis_index("c")
        q_me = 2 * mx + my
        mine = [pltpu.make_async_copy(p_refs[l].at[q_me], out_ref.at[q_me, l], local_sems.at[l]) for l in range(n)]
        for cp in mine:
            cp.start()
        sent, expected = [], []
        for k in range(1, 4):
            fx, fy = (k >> 1) & 1, k & 1
            px = mx + fx - 2 * mx * fx
            py = my + fy - 2 * my * fy
            q_peer = 2 * px + py
            for l in range(n):
                sent.append(pltpu.make_async_remote_copy(
                    src_ref=p_refs[l].at[q_peer], dst_ref=out_ref.at[q_me, l],
                    send_sem=send_sems.at[k - 1, l], recv_sem=recv_sems.at[k - 1, l],
                    device_id=(px, py, mc), device_id_type=MESH))
                expected.append(pltpu.make_async_remote_copy(
                    src_ref=p_refs[l].at[q_peer], dst_ref=out_ref.at[q_peer, l],
                    send_sem=send_sems.at[k - 1, l], recv_sem=recv_sems.at[k - 1, l],
                    device_id=(px, py, mc), device_id_type=MESH))
        for cp in sent:
            cp.start()
        for cp in expected:
            cp.wait_recv()
        for cp in sent:
            cp.wait_send()
        for cp in mine:
            cp.wait()

    return pl.pallas_call(
        body, name=name,
        out_shape=jax.ShapeDtypeStruct((4, n, r, c), ps[0].dtype),
        in_specs=[pl.BlockSpec(memory_space=pl.ANY)] * n,
        out_specs=pl.BlockSpec(memory_space=pl.ANY),
        scratch_shapes=[pltpu.SemaphoreType.DMA((3, n)), pltpu.SemaphoreType.DMA((3, n)),
                        pltpu.SemaphoreType.DMA((n,))],
    )(*ps)


def _slot_sum(slots, name):
    ns, r, c = slots.shape
    tr = _tile(r, 256)

    def body(s_ref, o_ref):
        acc = s_ref[0].astype(F32)
        for s in range(1, ns):
            acc = acc + s_ref[s].astype(F32)
        o_ref[...] = acc

    return pl.pallas_call(
        body, name=name, grid=(r // tr,),
        in_specs=[pl.BlockSpec((ns, tr, c), lambda i: (0, i, 0))],
        out_specs=pl.BlockSpec((tr, c), lambda i: (i, 0)),
        out_shape=jax.ShapeDtypeStruct((r, c), F32),
        compiler_params=_params(("parallel",)),
    )(slots)


def _adamw(slots, w, m, v, name):
    ns, r, c = slots.shape
    tr = _tile(r, max(SUBLANES, (128 * 1024) // c))

    def body(s_ref, w_ref, m_ref, v_ref, g_out, d_out, m_out, v_out):
        g = s_ref[0].astype(F32)
        for s in range(1, ns):
            g = g + s_ref[s].astype(F32)
        mn = ADAM_B1 * m_ref[...] + (1.0 - ADAM_B1) * g
        vn = ADAM_B2 * v_ref[...] + (1.0 - ADAM_B2) * (g * g)
        m_hat = mn / (1.0 - ADAM_B1 ** ADAM_STEP)
        v_hat = vn / (1.0 - ADAM_B2 ** ADAM_STEP)
        g_out[...] = g
        d_out[...] = -ADAM_LR * (m_hat / (jnp.sqrt(v_hat) + ADAM_EPS) + ADAM_WD * w_ref[...])
        m_out[...] = mn
        v_out[...] = vn

    blk = pl.BlockSpec((tr, c), lambda i: (i, 0))
    return pl.pallas_call(
        body, name=name, grid=(r // tr,),
        in_specs=[pl.BlockSpec((ns, tr, c), lambda i: (0, i, 0)), blk, blk, blk],
        out_specs=[blk] * 4,
        out_shape=[jax.ShapeDtypeStruct((r, c), F32)] * 4,
        compiler_params=_params(("parallel",)),
    )(slots, w, m, v)


def _pack(arrays, dtype, row_multiple=SUBLANES):
    flat = jnp.concatenate([a.astype(dtype).reshape(-1) for a in arrays])
    unit = row_multiple * PACK_COLS
    total = -(-flat.shape[0] // unit) * unit
    if total != flat.shape[0]:
        flat = jnp.pad(flat, (0, total - flat.shape[0]))
    return flat.reshape(-1, PACK_COLS)


def _unpack(buf, shapes):
    flat = buf.reshape(-1)
    out, o = [], 0
    for s in shapes:
        n = int(np.prod(s))
        out.append(flat[o:o + n].reshape(s))
        o += n
    return out


BIG = ("w_up", "w_down", "w_ple_proj", "w_ple_gate", "ev_w_in", "ev_w_out", "od_w_in", "od_w_out")
BIG_COL_SHARDED = {"w_up": True, "w_down": False, "w_ple_proj": True, "w_ple_gate": False,
                   "ev_w_in": True, "ev_w_out": False, "od_w_in": True, "od_w_out": False}
SMALL_SHARDED = ("dn_conv_w", "lru_conv_w")
SMALL_REPLICATED = ("ln_mix_w", "ln_mlp_w", "ln_ple_w", "ln_final_w", "dn_a_log", "dn_dt_bias", "dn_norm_w",
                    "lru_conv_b", "lru_wa", "lru_ba", "lru_wx", "lru_bx", "lru_lambda")
WEIGHTS = ("ln_mix_w", "ln_mlp_w", "ln_ple_w", "w_up", "w_down", "w_ple_proj", "w_ple_gate", "ln_final_w",
           "ev_w_in", "ev_w_out", "dn_conv_w", "dn_a_log", "dn_dt_bias", "dn_norm_w", "lru_conv_w", "lru_conv_b",
           "lru_wa", "lru_ba", "lru_wx", "lru_bx", "lru_lambda", "od_w_in", "od_w_out")


def kernel(x, p, ln_mix_w, ln_mlp_w, ln_ple_w, w_up, w_down, w_ple_proj, w_ple_gate, ln_final_w, ev_w_in, ev_w_out, dn_conv_w, dn_a_log, dn_dt_bias, dn_norm_w, lru_conv_w, lru_conv_b, lru_wa, lru_ba, lru_wx, lru_bx, lru_lambda, od_w_in, od_w_out, loss_target, m_ln_mix_w, m_ln_mlp_w, m_ln_ple_w, m_w_up, m_w_down, m_w_ple_proj, m_w_ple_gate, m_ln_final_w, m_ev_w_in, m_ev_w_out, m_dn_conv_w, m_dn_a_log, m_dn_dt_bias, m_dn_norm_w, m_lru_conv_w, m_lru_conv_b, m_lru_wa, m_lru_ba, m_lru_wx, m_lru_bx, m_lru_lambda, m_od_w_in, m_od_w_out, v_ln_mix_w, v_ln_mlp_w, v_ln_ple_w, v_w_up, v_w_down, v_w_ple_proj, v_w_ple_gate, v_ln_final_w, v_ev_w_in, v_ev_w_out, v_dn_conv_w, v_dn_a_log, v_dn_dt_bias, v_dn_norm_w, v_lru_conv_w, v_lru_conv_b, v_lru_wa, v_lru_ba, v_lru_wx, v_lru_bx, v_lru_lambda, v_od_w_in, v_od_w_out):
    w = dict(ln_mix_w=ln_mix_w, ln_mlp_w=ln_mlp_w, ln_ple_w=ln_ple_w, w_up=w_up, w_down=w_down,
             w_ple_proj=w_ple_proj, w_ple_gate=w_ple_gate, ln_final_w=ln_final_w, ev_w_in=ev_w_in,
             ev_w_out=ev_w_out, dn_conv_w=dn_conv_w, dn_a_log=dn_a_log, dn_dt_bias=dn_dt_bias,
             dn_norm_w=dn_norm_w, lru_conv_w=lru_conv_w, lru_conv_b=lru_conv_b, lru_wa=lru_wa, lru_ba=lru_ba,
             lru_wx=lru_wx, lru_bx=lru_bx, lru_lambda=lru_lambda, od_w_in=od_w_in, od_w_out=od_w_out)
    m = dict(ln_mix_w=m_ln_mix_w, ln_mlp_w=m_ln_mlp_w, ln_ple_w=m_ln_ple_w, w_up=m_w_up, w_down=m_w_down,
             w_ple_proj=m_w_ple_proj, w_ple_gate=m_w_ple_gate, ln_final_w=m_ln_final_w, ev_w_in=m_ev_w_in,
             ev_w_out=m_ev_w_out, dn_conv_w=m_dn_conv_w, dn_a_log=m_dn_a_log, dn_dt_bias=m_dn_dt_bias,
             dn_norm_w=m_dn_norm_w, lru_conv_w=m_lru_conv_w, lru_conv_b=m_lru_conv_b, lru_wa=m_lru_wa,
             lru_ba=m_lru_ba, lru_wx=m_lru_wx, lru_bx=m_lru_bx, lru_lambda=m_lru_lambda, od_w_in=m_od_w_in,
             od_w_out=m_od_w_out)
    v = dict(ln_mix_w=v_ln_mix_w, ln_mlp_w=v_ln_mlp_w, ln_ple_w=v_ln_ple_w, w_up=v_w_up, w_down=v_w_down,
             w_ple_proj=v_w_ple_proj, w_ple_gate=v_w_ple_gate, ln_final_w=v_ln_final_w, ev_w_in=v_ev_w_in,
             ev_w_out=v_ev_w_out, dn_conv_w=v_dn_conv_w, dn_a_log=v_dn_a_log, dn_dt_bias=v_dn_dt_bias,
             dn_norm_w=v_dn_norm_w, lru_conv_w=v_lru_conv_w, lru_conv_b=v_lru_conv_b, lru_wa=v_lru_wa,
             lru_ba=v_lru_ba, lru_wx=v_lru_wx, lru_bx=v_lru_bx, lru_lambda=v_lru_lambda, od_w_in=v_od_w_in,
             od_w_out=v_od_w_out)
    me = 4 * lax.axis_index("x") + 2 * lax.axis_index("y") + lax.axis_index("c")

    big = {}
    for n in BIG:
        shards = _gather_layers(w[n].astype(BF16), "gather_" + n)
        big[n] = [s if BIG_COL_SHARDED[n] else s.reshape(-1, s.shape[2]) for s in shards]
    conv_shapes = [w[n].shape for n in SMALL_SHARDED]
    conv_g = _all_gather(_pack([w[n] for n in SMALL_SHARDED], F32), "gather_conv")
    conv_dev = [_unpack(conv_g[s], conv_shapes) for s in range(N_DEV)]
    small = {n: w[n] for n in SMALL_REPLICATED}
    for i, n in enumerate(SMALL_SHARDED):
        small[n] = jnp.concatenate([conv_dev[s][i] for s in range(N_DEV)], axis=-1)

    loss_local, (g_big, g_small, g_x) = jax.value_and_grad(_local_loss, argnums=(0, 1, 2))(
        big, small, x[0], p[:, 0], loss_target[0])
    loss = lax.psum(loss_local, ("x", "y", "c"))

    side = lax.axis_index("c").astype(jnp.int32).reshape(1)
    out = {}
    for n in BIG:
        nl, r, c = w[n].shape
        gs = [g.reshape(N_DEV, r, c) for g in g_big[n]]
        recv = _swap_pairs(gs, "swap_" + n)
        sums = [_pair_sum(g, recv, l, side, "pairsum_" + n) for l, g in enumerate(gs)]
        got = _deliver_chips(sums, "deliver_" + n)
        res = _adamw(got.reshape(4, nl * r, c), w[n].reshape(nl * r, c), m[n].reshape(nl * r, c),
                     v[n].reshape(nl * r, c), "adamw_" + n)
        for kind, buf in zip(("grad", "delta", "new_m", "new_v"), res):
            out[kind, n] = buf.reshape(nl, r, c)

    small_names = SMALL_REPLICATED + SMALL_SHARDED
    small_shapes = [small[n].shape for n in small_names]
    all_small = _all_gather(_pack([g_small[n] for n in small_names], F32), "gather_small_grads")
    total = dict(zip(small_names, _unpack(_slot_sum(all_small, "sum_small_grads"), small_shapes)))
    for n in SMALL_SHARDED:
        width = w[n].shape[-1]
        total[n] = lax.dynamic_slice_in_dim(total[n], me * width, width, axis=-1)
    own_shapes = [w[n].shape for n in small_names]
    res_small = _adamw(_pack([total[n] for n in small_names], F32)[None], _pack([w[n] for n in small_names], F32),
                       _pack([m[n] for n in small_names], F32), _pack([v[n] for n in small_names], F32), "adamw_small")
    for kind, buf in zip(("grad", "delta", "new_m", "new_v"), res_small):
        for n, a in zip(small_names, _unpack(buf, own_shapes)):
            out[kind, n] = a

    return (loss, g_x[None], *[out["grad", n] for n in WEIGHTS], *[out["delta", n] for n in WEIGHTS],
            *[out["new_m", n] for n in WEIGHTS], *[out["new_v", n] for n in WEIGHTS])
```

```python
import functools

import numpy as np
import jax
import jax.numpy as jnp
from jax import lax
from jax.experimental import pallas as pl
from jax.experimental.pallas import tpu as pltpu

F32 = jnp.float32
BF16 = jnp.bfloat16
N_DEV = 8
LANES = 128
SUBLANES = 8
VMEM_LIMIT = 56 * 1024 * 1024
PACK_COLS = 1024
NORM_EPS = 1e-6
GN_EPS = 1e-5
DN_CHUNK = 64
RET_CHUNK = 64
HEAD = 128
RET_DK = 128
RET_DV = 256
SWA_BLOCK = 128
SWA_BRANCHES = ((128, 1), (512, 4), (2048, 16))
LRU_C = 8.0
CONV_W = 4
ADAM_LR, ADAM_B1, ADAM_B2, ADAM_EPS, ADAM_WD, ADAM_STEP = 0.001, 0.9, 0.999, 1e-08, 0.01, 10
NEG = -1e30
MESH = pl.DeviceIdType.MESH


def _params(sem):
    return pltpu.CompilerParams(dimension_semantics=sem, vmem_limit_bytes=VMEM_LIMIT)


def _tile(n, cap):
    for t in (2048, 1024, 896, 768, 640, 512, 384, 256, 128, 64, 32, 16, 8):
        if t <= cap and n % t == 0:
            return t
    return n


def _bdot(a, b, dims):
    return lax.dot_general(a.astype(BF16), b.astype(BF16), (dims, ((), ())), preferred_element_type=F32)


_NN = ((1,), (0,))
_NT = ((1,), (1,))
_TN = ((0,), (0,))


def _hdot(a, b):
    return lax.dot_general(a, b, (_NN, ((), ())), precision=lax.Precision.HIGH, preferred_element_type=F32)


def _sigmoid(x):
    return jax.nn.sigmoid(x)


def _silu(x):
    return x * _sigmoid(x)


def _softplus(x):
    return jnp.maximum(x, 0.0) + jnp.log(1.0 + jnp.exp(-jnp.abs(x)))


def _gelu(x):
    return 0.5 * x * (1.0 + jnp.tanh(0.7978845608028654 * (x + 0.044715 * (x * x * x))))


def _mm_call(a, b, *, ta=False, tb=False, extras=(), epilogue=None, out_dtypes=(F32,), b_slots=False,
             out_slots=False, rider=None, name):
    m, k = (a.shape[1], a.shape[0]) if ta else a.shape
    ne, no = len(extras), len(out_dtypes)
    cap_n = 512 if ne + no > 2 else 1024
    shard = b.shape[2] if b_slots else None
    if b_slots:
        n = b.shape[1] if tb else N_DEV * shard
    else:
        n = b.shape[0] if tb else b.shape[1]
    if out_slots:
        shard = n // N_DEV
    tm = _tile(m, 1024)
    tn = _tile(shard if (out_slots or (b_slots and not tb)) else n, cap_n)
    cap_k = 2048 if (a.dtype == BF16 and b.dtype == BF16) else 1024
    tk = _tile(shard if (b_slots and tb) else k, cap_k)
    nk = k // tk
    dims = ((0,) if ta else (1,), (1,) if tb else (0,))

    gm, gn = m // tm, n // tn
    nri = 0 if rider is None else len(rider.inputs)
    nro = 0 if rider is None else len(rider.out_shapes)

    def body(*refs):
        a_ref, b_ref = refs[0], refs[1]
        ex = refs[2:2 + ne]
        r_in = refs[2 + ne:2 + ne + nri]
        outs = refs[2 + ne + nri:2 + ne + nri + no]
        r_out = refs[2 + ne + nri + no:2 + ne + nri + no + nro]
        acc = refs[2 + ne + nri + no + nro]
        r_sems = refs[3 + ne + nri + no + nro:]
        kk = pl.program_id(2)
        if rider is not None:
            at_i, at_j = pl.program_id(0), pl.program_id(1)

            @pl.when(jnp.logical_and(jnp.logical_and(at_i == 0, at_j == 0), kk == 0))
            def _():
                rider.start(r_in, r_out, r_sems)

        def finish(total):
            res = (total,) if epilogue is None else epilogue(total, *[e[...] for e in ex])
            for o, r in zip(outs, res):
                o[...] = r.astype(o.dtype)

        if nk == 1:
            finish(_bdot(a_ref[...], b_ref[...], dims))
        else:
            @pl.when(kk == 0)
            def _():
                acc[...] = _bdot(a_ref[...], b_ref[...], dims)

            @pl.when(kk > 0)
            def _():
                acc[...] += _bdot(a_ref[...], b_ref[...], dims)

            @pl.when(kk == nk - 1)
            def _():
                finish(acc[...])

        if rider is not None:
            @pl.when(jnp.logical_and(jnp.logical_and(at_i == gm - 1, at_j == gn - 1), kk == nk - 1))
            def _():
                rider.finish(r_in, r_out, r_sems)

    a_spec = pl.BlockSpec((tk, tm), lambda i, j, kk: (kk, i)) if ta else pl.BlockSpec((tm, tk), lambda i, j, kk: (i, kk))
    if b_slots and tb:
        per = shard // tk
        b_spec = pl.BlockSpec((None, tn, tk), lambda i, j, kk: (kk // per, j, kk % per))
    elif b_slots:
        per = shard // tn
        b_spec = pl.BlockSpec((None, tk, tn), lambda i, j, kk: (j // per, kk, j % per))
    elif tb:
        b_spec = pl.BlockSpec((tn, tk), lambda i, j, kk: (j, kk))
    else:
        b_spec = pl.BlockSpec((tk, tn), lambda i, j, kk: (kk, j))
    mn_spec = pl.BlockSpec((tm, tn), lambda i, j, kk: (i, j))
    if out_slots:
        per_o = shard // tn
        out_specs = [pl.BlockSpec((None, tm, tn), lambda i, j, kk: (j // per_o, i, j % per_o))]
        out_shape = [jax.ShapeDtypeStruct((N_DEV, m, shard), out_dtypes[0])]
    else:
        out_specs = [mn_spec] * no
        out_shape = [jax.ShapeDtypeStruct((m, n), d) for d in out_dtypes]
    hbm = pl.BlockSpec(memory_space=pl.ANY)
    r_inputs = [] if rider is None else list(rider.inputs)
    return pl.pallas_call(
        body, name=name, grid=(gm, gn, nk),
        in_specs=[a_spec, b_spec] + [mn_spec] * ne + [hbm] * nri,
        out_specs=out_specs + [hbm] * nro,
        out_shape=out_shape + ([] if rider is None else list(rider.out_shapes)),
        scratch_shapes=[pltpu.VMEM((tm, tn), F32)] + ([] if rider is None else list(rider.scratch)),
        compiler_params=_params(("arbitrary", "arbitrary", "arbitrary") if rider is not None
                                else ("parallel", "parallel", "arbitrary")),
    )(a, b, *extras, *r_inputs)


def _make_mm(name, out_dtype, slots=False):
    @jax.custom_vjp
    def op(a, w):
        return _mm_call(a, w, out_dtypes=(out_dtype,), b_slots=slots, name=name + "_f")[0]

    def fwd(a, w):
        return _mm_call(a, w, out_dtypes=(out_dtype,), b_slots=slots, name=name + "_f")[0], (a, w)

    def bwd(res, dy):
        a, w = res
        dy = dy.astype(BF16)
        da = _mm_call(dy, w, tb=True, out_dtypes=(a.dtype,), b_slots=slots, name=name + "_da")[0]
        dw = _mm_call(a, dy, ta=True, out_dtypes=(w.dtype,), out_slots=slots, name=name + "_dw")[0]
        return da, dw

    op.defvjp(fwd, bwd)
    return op


def _make_mm_res(name):
    def call(a, w, h):
        return _mm_call(a, w, extras=(h,), epilogue=lambda acc, hv: (hv + acc,), out_dtypes=(F32,), name=name + "_f")[0]

    @jax.custom_vjp
    def op(a, w, h):
        return call(a, w, h)

    def fwd(a, w, h):
        return call(a, w, h), (a, w)

    def bwd(res, dy):
        a, w = res
        da = _mm_call(dy, w, tb=True, out_dtypes=(a.dtype,), name=name + "_da")[0]
        dw = _mm_call(a, dy, ta=True, out_dtypes=(w.dtype,), name=name + "_dw")[0]
        return da, dw, dy

    op.defvjp(fwd, bwd)
    return op


def _make_ffn(name):
    def forward(hn, w_up, w_down, h):
        def ep(acc):
            r = jnp.maximum(acc, 0.0)
            return acc, r * r
        wu = _gather_layers(w_up.astype(BF16)[None], name + "_gather_up")[0]
        u, act, wd = _mm_call(hn, wu, epilogue=ep, out_dtypes=(BF16, BF16), b_slots=True,
                              rider=_GatherRider(w_down.astype(BF16)), name=name + "_up")
        wd = wd.reshape(-1, wd.shape[2])
        out = _mm_call(act, wd, extras=(h,), epilogue=lambda acc, hv: (hv + acc,), out_dtypes=(F32,), name=name + "_down")[0]
        return out, (hn, wu, wd, u, act)

    @jax.custom_vjp
    def op(hn, w_up, w_down, h):
        return forward(hn, w_up, w_down, h)[0]

    def bwd(res, dy):
        hn, wu, wd, u, act = res
        dyb = dy.astype(BF16)
        d_wdown = _mm_call(act, dyb, ta=True, out_dtypes=(BF16,), name=name + "_dwdown")[0]
        ps_down = _pair_sums(d_wdown.reshape(N_DEV, -1, d_wdown.shape[1]), name + "_wdown")
        d_u, got_down = _mm_call(dyb, wd, tb=True, extras=(u,),
                                 epilogue=lambda acc, uv: (acc * (2.0 * jnp.maximum(uv.astype(F32), 0.0)),),
                                 out_dtypes=(BF16,), rider=_DeliverRider(ps_down), name=name + "_du")
        d_wup = _mm_call(hn, d_u, ta=True, out_dtypes=(BF16,), out_slots=True, name=name + "_dwup")[0]
        ps_up = _pair_sums(d_wup, name + "_wup")
        d_hn, got_up = _mm_call(d_u, wu, tb=True, out_dtypes=(hn.dtype,), b_slots=True,
                                rider=_DeliverRider(ps_up), name=name + "_dhn")
        return d_hn, _slot_sum(got_up, name + "_sum_wup"), _slot_sum(got_down, name + "_sum_wdown"), dy

    op.defvjp(forward, bwd)
    return op


def _make_ple(name):
    def forward(hn, w_gate, p, w_proj, h):
        pp = _mm_call(p, w_proj, out_dtypes=(F32,), b_slots=True, name=name + "_proj")[0]
        out, gp = _mm_call(hn, w_gate, extras=(h, pp),
                           epilogue=lambda acc, hv, ppv: (hv + _sigmoid(acc) * ppv, acc),
                           out_dtypes=(F32, F32), name=name + "_gate")
        return out, (hn, w_gate, p, w_proj, gp, pp)

    @jax.custom_vjp
    def op(hn, w_gate, p, w_proj, h):
        return forward(hn, w_gate, p, w_proj, h)[0]

    def bwd(res, dy):
        hn, w_gate, p, w_proj, gp, pp = res

        def gate_grads(g, dyv, gpv, ppv):
            s = _sigmoid(gpv)
            return (dyv * ppv * s * (1.0 - s)).astype(BF16), (dyv * s).astype(BF16)

        t, d = dy.shape
        d_gp, d_pp = _rowmap_call(gate_grads, t, 1, [dy, gp, pp], [], [(d, BF16), (d, BF16)], name + "_dgate")
        d_wgate = _mm_call(hn, d_gp, ta=True, out_dtypes=(w_gate.dtype,), name=name + "_dwgate")[0]
        d_wproj = _mm_call(p, d_pp, ta=True, out_dtypes=(w_proj.dtype,), out_slots=True, name=name + "_dwproj")[0]
        d_hn = _mm_call(d_gp, w_gate, tb=True, out_dtypes=(hn.dtype,), name=name + "_dhn")[0]
        return d_hn, d_wgate, jnp.zeros_like(p), d_wproj, dy

    op.defvjp(forward, bwd)
    return op


def _row_tile(t, widths):
    return _tile(t, max(SUBLANES, (256 * 1024) // max(widths)))


def _rowmap_specs(t, g, rows, bcs, tt):
    row_specs = [pl.BlockSpec((tt, r.shape[1] // g), lambda gg, i: (i, gg)) for r in rows]
    bc_specs = []
    for b, per_group in bcs:
        if per_group:
            bc_specs.append(pl.BlockSpec((b.shape[0], b.shape[1] // g), lambda gg, i: (0, gg)))
        else:
            bc_specs.append(pl.BlockSpec(b.shape, lambda gg, i: (0, 0)))
    return row_specs, bc_specs


def _rowmap_call(fn, t, g, rows, bcs, outs, name):
    widths = [r.shape[1] // g for r in rows] + [c // g for c, _ in outs]
    tt = _row_tile(t, widths)
    nr, nb = len(rows), len(bcs)
    row_specs, bc_specs = _rowmap_specs(t, g, rows, bcs, tt)

    def body(*refs):
        vals = [r[...] for r in refs[:nr + nb]]
        res = fn(pl.program_id(0), *vals)
        for o, r in zip(refs[nr + nb:], res):
            o[...] = r.astype(o.dtype)

    return pl.pallas_call(
        body, name=name, grid=(g, t // tt),
        in_specs=row_specs + bc_specs,
        out_specs=[pl.BlockSpec((tt, c // g), lambda gg, i: (i, gg)) for c, _ in outs],
        out_shape=[jax.ShapeDtypeStruct((t, c), d) for c, d in outs],
        compiler_params=_params(("parallel", "parallel")),
    )(*rows, *[b for b, _ in bcs])


def _rowmap_bwd_call(fn, t, g, rows, bcs, cots, name, add0=None):
    widths = [r.shape[1] // g for r in rows] + [c.shape[1] // g for c in cots]
    tt = _row_tile(t, widths)
    nr, nb, nc = len(rows), len(bcs), len(cots)
    na = 0 if add0 is None else 1
    row_specs, bc_specs = _rowmap_specs(t, g, rows, bcs, tt)
    cot_specs = [pl.BlockSpec((tt, c.shape[1] // g), lambda gg, i: (i, gg)) for c in cots]
    add_specs = [] if add0 is None else [row_specs[0]]
    shared = [not per_group for _, per_group in bcs]

    def body(*refs):
        ins = refs[:nr + nb]
        cot_refs = refs[nr + nb:nr + nb + nc]
        add_refs = refs[nr + nb + nc:nr + nb + nc + na]
        d_rows = refs[nr + nb + nc + na:nr + nb + nc + na + nr]
        d_bcs = refs[nr + nb + nc + na + nr:]
        gg, i = pl.program_id(0), pl.program_id(1)
        vals = [r[...] for r in ins]
        _, vjp = jax.vjp(lambda *v: tuple(fn(gg, *v)), *vals)
        grads = vjp(tuple(c[...] for c in cot_refs))
        for j, (o, gr) in enumerate(zip(d_rows, grads[:nr])):
            if j == 0 and na:
                gr = gr + add_refs[0][...]
            o[...] = gr.astype(o.dtype)
        for o, gr, sh in zip(d_bcs, grads[nr:], shared):
            first = jnp.logical_and(i == 0, gg == 0) if sh else i == 0

            @pl.when(first)
            def _():
                o[...] = jnp.zeros_like(o)

            o[...] += gr.astype(o.dtype)

    res = pl.pallas_call(
        body, name=name, grid=(g, t // tt),
        in_specs=row_specs + bc_specs + cot_specs + add_specs,
        out_specs=row_specs + bc_specs,
        out_shape=[jax.ShapeDtypeStruct(r.shape, r.dtype) for r in rows]
        + [jax.ShapeDtypeStruct(b.shape, F32) for b, _ in bcs],
        compiler_params=_params(("arbitrary", "arbitrary")),
    )(*rows, *[b for b, _ in bcs], *cots, *([] if add0 is None else [add0]))
    return res[:nr], res[nr:]


def _make_rowmap(fn, g, n_rows, per_group, outs, name):
    def call(*args):
        rows, bcs = list(args[:n_rows]), list(zip(args[n_rows:], per_group))
        return tuple(_rowmap_call(fn, rows[0].shape[0], g, rows, bcs, outs, name + "_f"))

    @jax.custom_vjp
    def op(*args):
        return call(*args)

    def fwd(*args):
        return call(*args), args

    def bwd(args, cots):
        rows, bcs = list(args[:n_rows]), list(zip(args[n_rows:], per_group))
        d_rows, d_bcs = _rowmap_bwd_call(fn, rows[0].shape[0], g, rows, bcs, list(cots), name + "_b")
        return tuple(d_rows) + tuple(d.astype(b.dtype) for d, (b, _) in zip(d_bcs, bcs))

    op.defvjp(fwd, bwd)
    return op


def _rms_fn(g, h, w):
    y = h * lax.rsqrt(jnp.mean(h * h, axis=-1, keepdims=True) + NORM_EPS)
    return ((y * w).astype(BF16),)


def _make_norm(name):
    def call(h, w):
        return _rowmap_call(_rms_fn, h.shape[0], 1, [h], [(w, False)], [(h.shape[1], BF16)], name + "_f")[0]

    @jax.custom_vjp
    def op(h, w):
        return h, call(h, w)

    def fwd(h, w):
        return (h, call(h, w)), (h, w)

    def bwd(res, cots):
        h, w = res
        dh_pass, dhn = cots
        d_rows, d_bcs = _rowmap_bwd_call(_rms_fn, h.shape[0], 1, [h], [(w, False)], [dhn], name + "_b", add0=dh_pass)
        return d_rows[0], d_bcs[0]

    op.defvjp(fwd, bwd)
    return op


def _loss_call(h, w, target, name):
    t, d = h.shape
    tt = _row_tile(t, [d])

    def body(h_ref, w_ref, t_ref, dh_ref, dw_ref, loss_ref):
        i = pl.program_id(0)
        tgt = t_ref[...]

        def lf(hv, wv):
            y = hv * lax.rsqrt(jnp.mean(hv * hv, axis=-1, keepdims=True) + NORM_EPS) * wv
            err = y - tgt
            return 0.5 * jnp.sum(jnp.mean(err * err, axis=-1, keepdims=True))

        lv, (dh, dw) = jax.value_and_grad(lf, argnums=(0, 1))(h_ref[...], w_ref[...])
        dh_ref[...] = dh

        @pl.when(i == 0)
        def _():
            dw_ref[...] = jnp.zeros_like(dw_ref)
            loss_ref[...] = jnp.zeros_like(loss_ref)

        dw_ref[...] += dw
        loss_ref[...] += jnp.full(loss_ref.shape, lv, F32)

    row = pl.BlockSpec((tt, d), lambda i: (i, 0))
    return pl.pallas_call(
        body, name=name, grid=(t // tt,),
        in_specs=[row, pl.BlockSpec((1, d), lambda i: (0, 0)), row],
        out_specs=[row, pl.BlockSpec((1, d), lambda i: (0, 0)), pl.BlockSpec((SUBLANES, LANES), lambda i: (0, 0))],
        out_shape=[jax.ShapeDtypeStruct((t, d), F32), jax.ShapeDtypeStruct((1, d), F32),
                   jax.ShapeDtypeStruct((SUBLANES, LANES), F32)],
        compiler_params=_params(("arbitrary",)),
    )(h, w, target)


def _make_loss(name):
    @jax.custom_vjp
    def op(h, w, target):
        return _loss_call(h, w, target, name)[2][0, 0]

    def fwd(h, w, target):
        dh, dw, lv = _loss_call(h, w, target, name)
        return lv[0, 0], (dh, dw, target)

    def bwd(res, ct):
        dh, dw, target = res
        return dh * ct, dw * ct, jnp.zeros_like(target)

    op.defvjp(fwd, bwd)
    return op


def _shift_down(cur, halo, s, first):
    if s == 0:
        return cur
    r = pltpu.roll(cur, s, 0)
    p = jnp.where(first, 0.0, pltpu.roll(halo, s, 0))
    rows = lax.broadcasted_iota(jnp.int32, p.shape, 0)
    head = jnp.where(rows < s, p, r[:SUBLANES])
    return jnp.concatenate([head, r[SUBLANES:]], axis=0)


def _shift_up(cur, halo, s, last):
    if s == 0:
        return cur
    n = cur.shape[0]
    r = pltpu.roll(cur, n - s, 0)
    p = jnp.where(last, 0.0, pltpu.roll(halo, SUBLANES - s, 0))
    rows = lax.broadcasted_iota(jnp.int32, p.shape, 0)
    tail = jnp.where(rows >= SUBLANES - s, p, r[n - SUBLANES:])
    return jnp.concatenate([r[:n - SUBLANES], tail], axis=0)


def _conv_specs(t, c):
    tt, cw = _tile(t, 512), _tile(c, 512)
    per = tt // SUBLANES
    nblk = t // SUBLANES
    cur = pl.BlockSpec((tt, cw), lambda j, i: (i, j))
    prev = pl.BlockSpec((SUBLANES, cw), lambda j, i: (jnp.maximum(i * per - 1, 0), j))
    nxt = pl.BlockSpec((SUBLANES, cw), lambda j, i: (jnp.minimum((i + 1) * per, nblk - 1), j))
    wsp = pl.BlockSpec((CONV_W, cw), lambda j, i: (0, j))
    bsp = pl.BlockSpec((1, cw), lambda j, i: (0, j))
    return tt, cw, cur, prev, nxt, wsp, bsp


def _conv_call(x, w, b, name):
    t, c = x.shape
    tt, cw, cur, prev, nxt, wsp, bsp = _conv_specs(t, c)

    def body(x_ref, p_ref, w_ref, b_ref, y_ref):
        first = pl.program_id(1) == 0
        xv, pv = x_ref[...], p_ref[...]
        y = jnp.zeros_like(xv) + b_ref[...]
        for j in range(CONV_W):
            y = y + w_ref[j:j + 1, :] * _shift_down(xv, pv, CONV_W - 1 - j, first)
        y_ref[...] = y

    return pl.pallas_call(
        body, name=name, grid=(c // cw, t // tt),
        in_specs=[cur, prev, wsp, bsp], out_specs=cur,
        out_shape=jax.ShapeDtypeStruct((t, c), F32),
        compiler_params=_params(("parallel", "parallel")),
    )(x, x, w, b)


def _conv_bwd_call(x, w, dy, name):
    t, c = x.shape
    tt, cw, cur, prev, nxt, wsp, bsp = _conv_specs(t, c)
    nt = t // tt

    def body(x_ref, p_ref, w_ref, dy_ref, n_ref, dx_ref, dw_ref, db_ref):
        i = pl.program_id(1)
        first, last = i == 0, i == nt - 1
        xv, pv, dyv, nv = x_ref[...], p_ref[...], dy_ref[...], n_ref[...]

        @pl.when(first)
        def _():
            dw_ref[...] = jnp.zeros_like(dw_ref)
            db_ref[...] = jnp.zeros_like(db_ref)

        dx = jnp.zeros_like(xv)
        for j in range(CONV_W):
            s = CONV_W - 1 - j
            dx = dx + w_ref[j:j + 1, :] * _shift_up(dyv, nv, s, last)
            dw_ref[j:j + 1, :] += jnp.sum(dyv * _shift_down(xv, pv, s, first), axis=0, keepdims=True)
        dx_ref[...] = dx
        db_ref[...] += jnp.sum(dyv, axis=0, keepdims=True)

    return pl.pallas_call(
        body, name=name, grid=(c // cw, nt),
        in_specs=[cur, prev, wsp, cur, nxt], out_specs=[cur, wsp, bsp],
        out_shape=[jax.ShapeDtypeStruct((t, c), F32), jax.ShapeDtypeStruct((CONV_W, c), F32),
                   jax.ShapeDtypeStruct((1, c), F32)],
        compiler_params=_params(("arbitrary", "arbitrary")),
    )(x, x, w, dy, dy)


def _make_conv(name):
    @jax.custom_vjp
    def op(x, w, b):
        return _conv_call(x, w, b, name + "_f")

    def fwd(x, w, b):
        return _conv_call(x, w, b, name + "_f"), (x, w)

    def bwd(res, dy):
        x, w = res
        return tuple(_conv_bwd_call(x, w, dy, name + "_b"))

    op.defvjp(fwd, bwd)
    return op


def _lru_call(a, u, name):
    t, nb, ln = a.shape
    tt = _tile(t, 1024)
    blk = pl.BlockSpec((tt, nb, ln), lambda i: (i, 0, 0))

    def body(a_ref, u_ref, h_ref, carry):
        @pl.when(pl.program_id(0) == 0)
        def _():
            carry[...] = jnp.zeros_like(carry)

        def step(k, h):
            h = a_ref[k] * h + u_ref[k]
            h_ref[k] = h
            return h

        carry[...] = lax.fori_loop(0, tt, step, carry[...], unroll=8)

    return pl.pallas_call(
        body, name=name, grid=(t // tt,), in_specs=[blk, blk], out_specs=blk,
        out_shape=jax.ShapeDtypeStruct(a.shape, F32), scratch_shapes=[pltpu.VMEM((nb, ln), F32)],
        compiler_params=_params(("arbitrary",)),
    )(a, u)


def _lru_bwd_call(a, hs, dy, name):
    t, nb, ln = a.shape
    tt = _tile(t, 1024)
    nt = t // tt
    blk = pl.BlockSpec((tt, nb, ln), lambda i: (nt - 1 - i, 0, 0))
    prev = pl.BlockSpec((1, nb, ln), lambda i: (jnp.maximum((nt - 1 - i) * tt - 1, 0), 0, 0))

    def body(a_ref, h_ref, hp_ref, dy_ref, da_ref, du_ref, carry):
        i = pl.program_id(0)

        @pl.when(i == 0)
        def _():
            carry[...] = jnp.zeros_like(carry)

        h_before = jnp.where(i == nt - 1, 0.0, hp_ref[0])

        def step(k, c):
            r = tt - 1 - k
            dh = dy_ref[r] + c
            du_ref[r] = dh
            da_ref[r] = dh * h_ref[jnp.maximum(r - 1, 0)]
            return a_ref[r] * dh

        carry[...] = lax.fori_loop(0, tt, step, carry[...], unroll=8)
        da_ref[0] = du_ref[0] * h_before

    return pl.pallas_call(
        body, name=name, grid=(nt,), in_specs=[blk, blk, prev, blk], out_specs=[blk, blk],
        out_shape=[jax.ShapeDtypeStruct(a.shape, F32), jax.ShapeDtypeStruct(a.shape, F32)],
        scratch_shapes=[pltpu.VMEM((nb, ln), F32)],
        compiler_params=_params(("arbitrary",)),
    )(a, hs, hs, dy)


def _make_lru(name):
    @jax.custom_vjp
    def op(a, u):
        return _lru_call(a, u, name + "_f")

    def fwd(a, u):
        hs = _lru_call(a, u, name + "_f")
        return hs, (a, hs)

    def bwd(res, dy):
        a, hs = res
        return tuple(_lru_bwd_call(a, hs, dy, name + "_b"))

    op.defvjp(fwd, bwd)
    return op


def _scan_specs(ins, const, heads, hp, chunk, rev_n):
    def tmap(n_of):
        return lambda hg, n: (n_of(n), hg)
    n_of = (lambda n: rev_n - 1 - n) if rev_n else (lambda n: n)
    in_specs = [pl.BlockSpec((chunk, hp * (x.shape[1] // heads)), tmap(n_of)) for x in ins]
    c_spec = pl.BlockSpec((1, hp * (const.shape[1] // heads)), lambda hg, n: (0, hg))
    return in_specs, c_spec, n_of


def _scan_call(chunk_fn, ins, const, heads, hp, chunk, state_shape, out_width, name):
    t = ins[0].shape[0]
    nc = t // chunk
    ni = len(ins)
    in_specs, c_spec, _ = _scan_specs(ins, const, heads, hp, chunk, 0)
    ws = [x.shape[1] // heads for x in ins]
    cw = const.shape[1] // heads
    dk, dv = state_shape

    def body(*refs):
        in_refs, c_ref, o_ref, s_ref, state = refs[:ni], refs[ni], refs[ni + 1], refs[ni + 2], refs[ni + 3]

        @pl.when(pl.program_id(1) == 0)
        def _():
            state[...] = jnp.zeros_like(state)

        vals = [[r[:, k * w:(k + 1) * w] for r, w in zip(in_refs, ws)] for k in range(hp)]
        consts = [c_ref[:, k * cw:(k + 1) * cw] for k in range(hp)]
        s0 = [state[k] for k in range(hp)]
        outs, s1 = chunk_fn(vals, consts, s0)
        for k in range(hp):
            s_ref[0, k] = s0[k]
            o_ref[:, k * out_width:(k + 1) * out_width] = outs[k]
            state[k] = s1[k]

    return pl.pallas_call(
        body, name=name, grid=(heads // hp, nc),
        in_specs=in_specs + [c_spec],
        out_specs=[pl.BlockSpec((chunk, hp * out_width), lambda hg, n: (n, hg)),
                   pl.BlockSpec((1, hp, dk, dv), lambda hg, n: (n, hg, 0, 0))],
        out_shape=[jax.ShapeDtypeStruct((t, heads * out_width), F32),
                   jax.ShapeDtypeStruct((nc, heads, dk, dv), F32)],
        scratch_shapes=[pltpu.VMEM((hp, dk, dv), F32)],
        compiler_params=_params(("parallel", "arbitrary")),
    )(*ins, const)


def _scan_bwd_call(chunk_fn, ins, const, states, d_out, heads, hp, chunk, state_shape, out_width, name):
    t = ins[0].shape[0]
    nc = t // chunk
    ni = len(ins)
    in_specs, c_spec, n_of = _scan_specs(ins, const, heads, hp, chunk, nc)
    ws = [x.shape[1] // heads for x in ins]
    cw = const.shape[1] // heads
    dk, dv = state_shape

    def body(*refs):
        in_refs, c_ref, s_ref, do_ref = refs[:ni], refs[ni], refs[ni + 1], refs[ni + 2]
        d_refs, dstate = refs[ni + 3:ni + 3 + ni], refs[-1]

        @pl.when(pl.program_id(1) == 0)
        def _():
            dstate[...] = jnp.zeros_like(dstate)

        vals = [[r[:, k * w:(k + 1) * w] for r, w in zip(in_refs, ws)] for k in range(hp)]
        consts = [c_ref[:, k * cw:(k + 1) * cw] for k in range(hp)]
        s0 = [s_ref[0, k] for k in range(hp)]
        d_o = [do_ref[:, k * out_width:(k + 1) * out_width] for k in range(hp)]
        d_s1 = [dstate[k] for k in range(hp)]
        _, vjp = jax.vjp(lambda vv, ss: chunk_fn(vv, consts, ss), vals, s0)
        d_vals, d_s0 = vjp((d_o, d_s1))
        for k in range(hp):
            for r, w, gr in zip(d_refs, ws, d_vals[k]):
                r[:, k * w:(k + 1) * w] = gr
            dstate[k] = d_s0[k]

    return pl.pallas_call(
        body, name=name, grid=(heads // hp, nc),
        in_specs=in_specs + [c_spec,
                             pl.BlockSpec((1, hp, dk, dv), lambda hg, n: (n_of(n), hg, 0, 0)),
                             pl.BlockSpec((chunk, hp * out_width), lambda hg, n: (n_of(n), hg))],
        out_specs=in_specs,
        out_shape=[jax.ShapeDtypeStruct(x.shape, F32) for x in ins],
        scratch_shapes=[pltpu.VMEM((hp, dk, dv), F32)],
        compiler_params=_params(("parallel", "arbitrary")),
    )(*ins, const, states, d_out)


def _make_scan(chunk_fn, const, heads, hp, chunk, state_shape, out_width, name):
    def call(*ins):
        return _scan_call(chunk_fn, list(ins), const, heads, hp, chunk, state_shape, out_width, name + "_f")

    @jax.custom_vjp
    def op(*ins):
        return call(*ins)[0]

    def fwd(*ins):
        o, states = call(*ins)
        return o, (ins, states)

    def bwd(res, d_out):
        ins, states = res
        return tuple(_scan_bwd_call(chunk_fn, list(ins), const, states, d_out, heads, hp, chunk, state_shape,
                                    out_width, name + "_b"))

    op.defvjp(fwd, bwd)
    return op


def _tri(c):
    ri = lax.broadcasted_iota(jnp.int32, (c, c), 0)
    ci = lax.broadcasted_iota(jnp.int32, (c, c), 1)
    return ri, ci


def _each(fn, *lists):
    return [fn(*a) for a in zip(*lists)]


@jax.custom_vjp
def _neumann_inverses(xs):
    c = xs[0].shape[0]
    ri, ci = _tri(c)
    eye = jnp.where(ri == ci, 1.0, 0.0)
    invs = [eye + x for x in xs]
    xps = list(xs)
    for _ in range(max(1, int(np.ceil(np.log2(c))) - 1)):
        xps = [_hdot(xp, xp) for xp in xps]
        invs = [inv + _hdot(inv, xp) for inv, xp in zip(invs, xps)]
    return invs


def _neumann_inverses_fwd(xs):
    invs = _neumann_inverses(xs)
    return invs, invs


def _neumann_inverses_bwd(invs, ds):
    hi = lax.Precision.HIGH
    ts = [lax.dot_general(d, inv, (_NT, ((), ())), precision=hi, preferred_element_type=F32)
          for d, inv in zip(ds, invs)]
    return ([lax.dot_general(inv, t, (_TN, ((), ())), precision=hi, preferred_element_type=F32)
             for inv, t in zip(invs, ts)],)


_neumann_inverses.defvjp(_neumann_inverses_fwd, _neumann_inverses_bwd)


def _dn_chunk(vals, consts, ss):
    del consts
    qs, ks, vs, gbs, bbs = (list(x) for x in zip(*vals))
    c = qs[0].shape[0]
    ri, ci = _tri(c)
    causal, strict = ri >= ci, ri > ci
    tri_f = causal.astype(F32)
    gc_b = [_hdot(tri_f, gb) for gb in gbs]
    gcol = [jnp.mean(x, axis=1, keepdims=True) for x in gc_b]
    grow = [jnp.mean(x.T, axis=0, keepdims=True) for x in gc_b]
    bcol = [jnp.mean(bb, axis=1, keepdims=True) for bb in bbs]
    decay = _each(lambda a, b: jnp.where(causal, jnp.exp(jnp.where(causal, a - b, 0.0)), 0.0), gcol, grow)
    kb = _each(lambda k, b: k * b, ks, bcol)
    m = _each(lambda a, k: _bdot(a, k, _NT), kb, ks)
    invs = _neumann_inverses(_each(lambda mm, d: -jnp.where(strict, mm * d, 0.0), m, decay))
    eg = [jnp.exp(a) for a in gcol]
    u = _each(lambda inv, v, b: _hdot(inv, v * b), invs, vs, bcol)
    w = _each(lambda inv, a, e: _hdot(inv, a * e), invs, kb, eg)
    qk = _each(lambda q, k, d: _bdot(q, k, _NT) * d, qs, ks, decay)
    g_last = [jnp.sum(jnp.mean(gb, axis=1, keepdims=True), axis=0, keepdims=True) for gb in gbs]
    k_dec = _each(lambda k, gl, a: k * jnp.exp(gl - a), ks, g_last, gcol)
    ws = _each(lambda w_, s: _bdot(w_, s, _NN), w, ss)
    v_new = _each(lambda u_, x: u_ - x, u, ws)
    o_state = _each(lambda q, e, s: _bdot(q * e, s, _NN), qs, eg, ss)
    o_intra = _each(lambda a, vn: _bdot(a, vn, _NN), qk, v_new)
    s_add = _each(lambda kd, vn: _bdot(kd, vn, _TN), k_dec, v_new)
    outs = _each(lambda a, b: a + b, o_state, o_intra)
    s_new = _each(lambda s, gl, a: s * jnp.exp(gl) + a, ss, g_last, s_add)
    return outs, s_new


def _ret_chunk(vals, consts, ss):
    qs, ks, vs = (list(x) for x in zip(*vals))
    c = qs[0].shape[0]
    ri, ci = _tri(c)
    rel = (ri - ci).astype(F32)
    idx = lax.broadcasted_iota(jnp.int32, (c, 1), 0).astype(F32)
    lg = [jnp.mean(x, axis=1, keepdims=True) for x in consts]
    dmask = [jnp.where(rel >= 0, jnp.exp(jnp.maximum(rel, 0.0) * a), 0.0) for a in lg]
    ksc = [k * (RET_DK ** -0.5) for k in ks]
    qk = _each(lambda q, k, d: _bdot(q, k, _NT) * d, qs, ksc, dmask)
    intra = _each(lambda a, v: _bdot(a, v, _NN), qk, vs)
    inter = _each(lambda q, a, s: _bdot(q * jnp.exp((idx + 1.0) * a), s, _NN), qs, lg, ss)
    s_add = _each(lambda k, a, v: _bdot(k * jnp.exp((c - 1.0 - idx) * a), v, _TN), ksc, lg, vs)
    outs = _each(lambda a, b: a + b, intra, inter)
    s_new = _each(lambda s, a, x: s * jnp.exp(c * a) + x, ss, lg, s_add)
    return outs, s_new


def _attn_block(args, slopes, first):
    qs, kps, kcs, vps, vcs = (list(x) for x in zip(*args))
    b = qs[0].shape[0]
    ri, ci = _tri(b)
    rel_c = ri - ci
    rel_p = rel_c + b
    ok_c = rel_c >= 0
    ok_p = jnp.logical_and(rel_p <= b, jnp.logical_not(first))
    rel_cf, rel_pf = rel_c.astype(F32), rel_p.astype(F32)
    qsc = [q * (HEAD ** -0.5) for q in qs]
    s_c = _each(lambda q, k, sl: jnp.where(ok_c, _bdot(q, k, _NT) - sl * rel_cf, NEG), qsc, kcs, slopes)
    s_p = _each(lambda q, k, sl: jnp.where(ok_p, _bdot(q, k, _NT) - sl * rel_pf, NEG), qsc, kps, slopes)
    mx = _each(lambda a, c: lax.stop_gradient(jnp.maximum(jnp.max(a, axis=1, keepdims=True),
                                                          jnp.max(c, axis=1, keepdims=True))), s_c, s_p)
    p_c = _each(lambda a, m: jnp.exp(a - m), s_c, mx)
    p_p = _each(lambda a, m: jnp.exp(a - m), s_p, mx)
    den = _each(lambda a, c: jnp.sum(a, axis=1, keepdims=True) + jnp.sum(c, axis=1, keepdims=True), p_c, p_p)
    o_c = _each(lambda a, v: _bdot(a, v, _NN), p_c, vcs)
    o_p = _each(lambda a, v: _bdot(a, v, _NN), p_p, vps)
    outs = _each(lambda a, c, d: (a + c) / d, o_c, o_p, den)
    lses = _each(lambda m, d, o: jnp.broadcast_to(m + jnp.log(d), o.shape), mx, den, outs)
    return outs, lses


ATTN_GROUP = 4


def _attn_plan(t, hw, dil):
    nh = hw // HEAD
    hb = 1 if dil > 1 else max(h for h in (4, 2, 1) if nh % h == 0)
    units = [(r, h) for r in range(dil) for h in range(hb)]
    groups = [units[i:i + ATTN_GROUP] for i in range(0, len(units), ATTN_GROUP)]
    return hb, SWA_BLOCK * dil, groups


class _unit:
    def __init__(self, ref, r, h, dil):
        rows = pl.ds(0, SWA_BLOCK) if dil == 1 else pl.ds(r, SWA_BLOCK, stride=dil)
        self.ref, self.idx = ref, (rows, pl.ds(h * HEAD, HEAD))

    def __getitem__(self, _):
        return self.ref[self.idx]

    def __setitem__(self, _, value):
        self.ref[self.idx] = value


def _attn_call(q, k, v, slopes, dil, name):
    t, hw = q.shape
    hb, rows, groups = _attn_plan(t, hw, dil)
    wide = hb * HEAD
    cur = pl.BlockSpec((rows, wide), lambda j, n: (n, j))
    prev = pl.BlockSpec((rows, wide), lambda j, n: (jnp.maximum(n - 1, 0), j))
    ssp = pl.BlockSpec((1, wide), lambda j, n: (0, j))

    def body(q_ref, kp_ref, kc_ref, vp_ref, vc_ref, s_ref, o_ref, l_ref):
        first = pl.program_id(1) == 0
        for grp in groups:
            args = [[_unit(x, r, h, dil)[...] for x in (q_ref, kp_ref, kc_ref, vp_ref, vc_ref)] for r, h in grp]
            outs, lses = _attn_block(args, [s_ref[:, h * HEAD:(h + 1) * HEAD] for _, h in grp], first)
            for (r, h), o, lse in zip(grp, outs, lses):
                _unit(o_ref, r, h, dil)[...] = o
                _unit(l_ref, r, h, dil)[...] = lse

    return pl.pallas_call(
        body, name=name, grid=(hw // wide, t // rows),
        in_specs=[cur, prev, cur, prev, cur, ssp], out_specs=[cur, cur],
        out_shape=[jax.ShapeDtypeStruct((t, hw), F32), jax.ShapeDtypeStruct((t, hw), F32)],
        compiler_params=_params(("parallel", "parallel")),
    )(q, k, k, v, v, slopes)


def _attn_bwd_call(q, k, v, slopes, d_o, d_l, dil, name):
    t, hw = q.shape
    hb, rows, groups = _attn_plan(t, hw, dil)
    wide = hb * HEAD
    nb = t // rows
    cur = pl.BlockSpec((rows, wide), lambda j, n: (jnp.minimum(n, nb - 1), j))
    prev = pl.BlockSpec((rows, wide), lambda j, n: (jnp.clip(n - 1, 0, nb - 1), j))
    ssp = pl.BlockSpec((1, wide), lambda j, n: (0, j))

    def body(q_ref, kp_ref, kc_ref, vp_ref, vc_ref, s_ref, do_ref, dl_ref, dq_ref, dk_ref, dv_ref, ck, cv):
        n = pl.program_id(1)

        @pl.when(n == 0)
        def _():
            ck[...] = jnp.zeros_like(ck)
            cv[...] = jnp.zeros_like(cv)

        @pl.when(n < nb)
        def _():
            first = n == 0
            for grp in groups:
                args = [[_unit(x, r, h, dil)[...] for x in (q_ref, kp_ref, kc_ref, vp_ref, vc_ref)] for r, h in grp]
                svs = [s_ref[:, h * HEAD:(h + 1) * HEAD] for _, h in grp]
                cots = ([_unit(do_ref, r, h, dil)[...] for r, h in grp], [_unit(dl_ref, r, h, dil)[...] for r, h in grp])
                carry = [(_unit(ck, r, h, dil)[...], _unit(cv, r, h, dil)[...]) for r, h in grp]
                _, vjp = jax.vjp(lambda a, svs=svs: _attn_block(a, svs, first), args)
                grads = vjp(cots)[0]
                for (r, h), (dq, dkp, dkc, dvp, dvc), (c_k, c_v) in zip(grp, grads, carry):
                    _unit(dq_ref, r, h, dil)[...] = dq
                    _unit(dk_ref, r, h, dil)[...] = c_k + dkp
                    _unit(dv_ref, r, h, dil)[...] = c_v + dvp
                    _unit(ck, r, h, dil)[...] = dkc
                    _unit(cv, r, h, dil)[...] = dvc

        @pl.when(n == nb)
        def _():
            dk_ref[...] = ck[...]
            dv_ref[...] = cv[...]

    return pl.pallas_call(
        body, name=name, grid=(hw // wide, nb + 1),
        in_specs=[cur, prev, cur, prev, cur, ssp, cur, cur], out_specs=[cur, prev, prev],
        out_shape=[jax.ShapeDtypeStruct((t, hw), F32)] * 3,
        scratch_shapes=[pltpu.VMEM((rows, wide), F32), pltpu.VMEM((rows, wide), F32)],
        compiler_params=_params(("parallel", "arbitrary")),
    )(q, k, k, v, v, slopes, d_o, d_l)


def _make_attn(slopes, dil, name):
    @jax.custom_vjp
    def op(q, k, v):
        return tuple(_attn_call(q, k, v, slopes, dil, name + "_f"))

    def fwd(q, k, v):
        return tuple(_attn_call(q, k, v, slopes, dil, name + "_f")), (q, k, v)

    def bwd(res, cots):
        q, k, v = res
        return tuple(_attn_bwd_call(q, k, v, slopes, cots[0], cots[1], dil, name + "_b"))

    op.defvjp(fwd, bwd)
    return op


def _dn_pre_fn(g, cq, ck, cv):
    sq, sk, sv = _silu(cq), _silu(ck), _silu(cv)
    qn = sq * lax.rsqrt(jnp.sum(sq * sq, axis=-1, keepdims=True) + 1e-6) * (HEAD ** -0.5)
    kn = sk * lax.rsqrt(jnp.sum(sk * sk, axis=-1, keepdims=True) + 1e-6)
    return qn, kn, sv


def _make_dn_gates_fn(heads):
    def fn(g, ba, a_log, dt_bias):
        lane = lax.broadcasted_iota(jnp.int32, ba.shape, 1)
        lane1 = lax.broadcasted_iota(jnp.int32, a_log.shape, 1)
        betas, gs = [], []
        for h in range(heads):
            b_raw = jnp.sum(jnp.where(lane == h, ba, 0.0), axis=1, keepdims=True)
            a_raw = jnp.sum(jnp.where(lane == heads + h, ba, 0.0), axis=1, keepdims=True)
            al = jnp.sum(jnp.where(lane1 == h, a_log, 0.0), axis=1, keepdims=True)
            dt = jnp.sum(jnp.where(lane1 == h, dt_bias, 0.0), axis=1, keepdims=True)
            beta = _sigmoid(b_raw)
            gl = -jnp.exp(al) * _softplus(a_raw + dt)
            betas.append(jnp.broadcast_to(beta, ba.shape))
            gs.append(jnp.broadcast_to(gl, ba.shape))
        return jnp.concatenate(betas, axis=1), jnp.concatenate(gs, axis=1)
    return fn


def _dn_post_fn(g, o, z, w):
    y = o * lax.rsqrt(jnp.mean(o * o, axis=-1, keepdims=True) + NORM_EPS) * w
    return ((y * _silu(z)).astype(BF16),)


def _lru_pre_fn(g, xc, wa, wx, ba, bx, lam):
    r = _sigmoid(_bdot(xc, wa, _NN) + ba)
    i = _sigmoid(_bdot(xc, wx, _NN) + bx)
    log_a = -LRU_C * r * _softplus(-lam)
    a = jnp.exp(log_a)
    u = jnp.sqrt(1.0 - jnp.exp(2.0 * log_a)) * (i * xc)
    return a, u


def _lru_post_fn(g, hs, yr):
    return ((hs * _gelu(yr)).astype(BF16),)


def _merge_fn(g, o1, o2, o3, l1, l2, l3):
    m = lax.stop_gradient(jnp.maximum(jnp.maximum(l1, l2), l3))
    e1, e2, e3 = jnp.exp(l1 - m), jnp.exp(l2 - m), jnp.exp(l3 - m)
    return (((e1 * o1 + e2 * o2 + e3 * o3) / (e1 + e2 + e3)).astype(BF16),)


def _ret_post_fn(g, o, gate):
    mu = jnp.mean(o, axis=-1, keepdims=True)
    xc = o - mu
    y = xc * lax.rsqrt(jnp.mean(xc * xc, axis=-1, keepdims=True) + GN_EPS)
    return ((y * _silu(gate)).astype(BF16),)


def _pad_lanes(v):
    return jnp.pad(v, (0, LANES - v.shape[0]))[None, :]


def _even_layout(half):
    heads = half // HEAD
    qkv = 3 * half
    segs = [(0, qkv, qkv), (qkv, half, half), (qkv + half, 2 * heads, LANES),
            (qkv + half + 2 * heads, half, half), (qkv + 2 * half + 2 * heads, half, half)]
    return segs


def _pad_ev_w_in(w, half):
    parts = []
    for start, width, padded in _even_layout(half):
        part = w[:, start:start + width]
        if padded != width:
            part = jnp.pad(part, ((0, 0), (0, padded - width)))
        parts.append(part)
    return jnp.concatenate(parts, axis=1)


def _even_mixer(hn, h, lw):
    t, d = h.shape
    half = d // 2
    heads = half // HEAD
    hp = 4 if heads % 4 == 0 else 1
    w_in = lw["w_in"].transpose(1, 0, 2).reshape(d, -1)
    proj = _make_mm("ev_in", F32)(hn, _pad_ev_w_in(w_in, half))
    o0 = 0
    segs = []
    for _, _, padded in _even_layout(half):
        segs.append(proj[:, o0:o0 + padded])
        o0 += padded
    qkv, z, ba, xr, yr = segs
    c = _make_conv("dn_conv")(qkv, lw["dn_conv_w"], jnp.zeros((1, 3 * half), F32))
    q, k, v = _make_rowmap(_dn_pre_fn, heads, 3, [], [(half, F32)] * 3, "dn_pre")(
        c[:, :half], c[:, half:2 * half], c[:, 2 * half:])
    beta_b, g_b = _make_rowmap(_make_dn_gates_fn(heads), 1, 1, [False, False], [(half, F32)] * 2, "dn_gates")(
        ba, _pad_lanes(lw["dn_a_log"]), _pad_lanes(lw["dn_dt_bias"]))
    o = _make_scan(_dn_chunk, jnp.zeros((1, heads * LANES), F32), heads, hp, DN_CHUNK, (HEAD, HEAD), HEAD, "dn_core")(
        q, k, v, g_b, beta_b)
    ya = _make_rowmap(_dn_post_fn, heads, 2, [False], [(half, BF16)], "dn_post")(o, z, lw["dn_norm_w"][None, :])[0]
    nblk = lw["lru_wa"].shape[0]
    xc = _make_conv("lru_conv")(xr, lw["lru_conv_w"], lw["lru_conv_b"][None, :])
    wa = lw["lru_wa"].transpose(1, 0, 2).reshape(HEAD, nblk * HEAD)
    wx = lw["lru_wx"].transpose(1, 0, 2).reshape(HEAD, nblk * HEAD)
    a, u = _make_rowmap(_lru_pre_fn, nblk, 1, [True] * 5, [(half, F32)] * 2, "lru_pre")(
        xc, wa, wx, lw["lru_ba"][None, :], lw["lru_bx"][None, :], lw["lru_lambda"][None, :])
    hs = _make_lru("lru_scan")(a.reshape(t, nblk, HEAD), u.reshape(t, nblk, HEAD)).reshape(t, half)
    yb = _make_rowmap(_lru_post_fn, nblk, 2, [], [(half, BF16)], "lru_post")(hs, yr)[0]
    return _make_mm_res("ev_out")(jnp.concatenate([ya, yb], axis=1), lw["w_out"], h)


def _odd_mixer(hn, h, lw):
    t, d = h.shape
    half = d // 2
    heads = half // HEAD
    rheads = half // RET_DV
    rqk = rheads * RET_DK
    proj = _make_mm("od_in", F32, slots=True)(hn, lw["w_in"])
    cq, ck, cv = proj[:, :half], proj[:, half:2 * half], proj[:, 2 * half:3 * half]
    o1 = 3 * half
    rq, rk = proj[:, o1:o1 + rqk], proj[:, o1 + rqk:o1 + 2 * rqk]
    rv, rg = proj[:, o1 + 2 * rqk:o1 + 2 * rqk + half], proj[:, o1 + 2 * rqk + half:]
    slopes = np.exp2(-8.0 * np.arange(1, heads + 1, dtype=np.float64) / heads)
    outs, lses = [], []
    for window, dil in SWA_BRANCHES:
        assert window // dil == SWA_BLOCK and (t // dil) % SWA_BLOCK == 0
        sl = jnp.asarray(np.repeat(slopes * dil, HEAD)[None, :], F32)
        o_i, l_i = _make_attn(sl, dil, "swa_d%d" % dil)(cq, ck, cv)
        outs.append(o_i)
        lses.append(l_i)
    yc = _make_rowmap(_merge_fn, heads, 6, [], [(half, BF16)], "swa_merge")(*outs, *lses)[0]
    lg = np.log1p(-np.exp2(-5.0 - np.arange(rheads, dtype=np.float64)))
    lg_b = jnp.asarray(np.repeat(lg, LANES)[None, :], F32)
    hp = 4 if rheads % 4 == 0 else 1
    o_r = _make_scan(_ret_chunk, lg_b, rheads, hp, RET_CHUNK, (RET_DK, RET_DV), RET_DV, "ret_core")(rq, rk, rv)
    yd = _make_rowmap(_ret_post_fn, rheads, 2, [], [(half, BF16)], "ret_post")(o_r, rg)[0]
    return _make_mm_res("od_out")(jnp.concatenate([yc, yd], axis=1), lw["w_out"], h)


def _local_loss(big, ffn, small, x, p, target):
    depth = ffn["w_up"].shape[0]
    h = x
    for i in range(depth):
        j = i // 2
        h, hn = _make_norm("ln_mix")(h, small["ln_mix_w"][i][None, :])
        if i % 2 == 0:
            lw = {"w_in": big["ev_w_in"][j], "w_out": big["ev_w_out"][j]}
            for nm in ("dn_conv_w", "dn_a_log", "dn_dt_bias", "dn_norm_w", "lru_conv_w", "lru_conv_b", "lru_wa",
                       "lru_ba", "lru_wx", "lru_bx", "lru_lambda"):
                lw[nm] = small[nm][j]
            h = _even_mixer(hn, h, lw)
        else:
            h = _odd_mixer(hn, h, {"w_in": big["od_w_in"][j], "w_out": big["od_w_out"][j]})
        h, hn = _make_norm("ln_mlp")(h, small["ln_mlp_w"][i][None, :])
        h = _make_ffn("ffn")(hn, ffn["w_up"][i], ffn["w_down"][i], h)
        h, hn = _make_norm("ln_ple")(h, small["ln_ple_w"][i][None, :])
        h = _make_ple("ple")(hn, big["w_ple_gate"][i], p[i], big["w_ple_proj"][i], h)
    return _make_loss("loss_head")(h, small["ln_final_w"][None, :], target)


def _all_gather(x, name):
    r, c = x.shape

    def body(x_ref, out_ref, send_sems, recv_sems, local_sem):
        mx, my, mc = lax.axis_index("x"), lax.axis_index("y"), lax.axis_index("c")
        me, sibling = (mx, my, mc), (mx, my, 1 - mc)
        chips = [(1 - mx, my), (mx, 1 - my), (1 - mx, 1 - my)]

        def slot(px, py, pc):
            return out_ref.at[4 * px + 2 * py + pc]

        def copy(k, block, to, src=None):
            return pltpu.make_async_remote_copy(
                src_ref=slot(*block) if src is None else src, dst_ref=slot(*block),
                send_sem=send_sems.at[k], recv_sem=recv_sems.at[k], device_id=to, device_id_type=MESH)

        mine = pltpu.make_async_copy(x_ref, slot(*me), local_sem)
        mine.start()
        first = [copy(0, me, sibling, src=x_ref)]
        first += [copy(1 + j, me, (*chip, mc), src=x_ref) for j, chip in enumerate(chips)]
        for cp in first:
            cp.start()
        passed = [copy(4 + j, (*chip, mc), sibling) for j, chip in enumerate(chips)]
        for j, chip in enumerate(chips):
            copy(1 + j, (*chip, mc), me).wait_recv()
            passed[j].start()
        copy(0, sibling, me).wait_recv()
        for j, chip in enumerate(chips):
            copy(4 + j, (*chip, 1 - mc), me).wait_recv()
        for cp in first + passed:
            cp.wait_send()
        mine.wait()

    return pl.pallas_call(
        body, name=name,
        out_shape=jax.ShapeDtypeStruct((N_DEV, r, c), x.dtype),
        in_specs=[pl.BlockSpec(memory_space=pl.ANY)],
        out_specs=pl.BlockSpec(memory_space=pl.ANY),
        scratch_shapes=[pltpu.SemaphoreType.DMA((7,)), pltpu.SemaphoreType.DMA((7,)), pltpu.SemaphoreType.DMA],
    )(x)


def _gather_layers(x, name):
    n, r, c = x.shape

    def body(x_ref, *rest):
        outs, (send_sems, recv_sems, local_sems) = rest[:n], rest[n:]
        mx, my, mc = lax.axis_index("x"), lax.axis_index("y"), lax.axis_index("c")
        me, sibling = (mx, my, mc), (mx, my, 1 - mc)
        chips = [(1 - mx, my), (mx, 1 - my), (1 - mx, 1 - my)]

        def slot(l, px, py, pc):
            return outs[l].at[4 * px + 2 * py + pc]

        def copy(k, l, block, to, from_shard=False):
            return pltpu.make_async_remote_copy(
                src_ref=x_ref.at[l] if from_shard else slot(l, *block), dst_ref=slot(l, *block),
                send_sem=send_sems.at[k, l], recv_sem=recv_sems.at[k, l], device_id=to, device_id_type=MESH)

        mine = [pltpu.make_async_copy(x_ref.at[l], slot(l, *me), local_sems.at[l]) for l in range(n)]
        for cp in mine:
            cp.start()
        sent = [copy(0, l, me, sibling, True) for l in range(n)]
        sent += [copy(1 + j, l, me, (*chip, mc), True) for j, chip in enumerate(chips) for l in range(n)]
        for cp in sent:
            cp.start()
        for j, chip in enumerate(chips):
            for l in range(n):
                copy(1 + j, l, (*chip, mc), me).wait_recv()
                passed = copy(4 + j, l, (*chip, mc), sibling)
                passed.start()
                sent.append(passed)
        for l in range(n):
            copy(0, l, sibling, me).wait_recv()
        for j, chip in enumerate(chips):
            for l in range(n):
                copy(4 + j, l, (*chip, 1 - mc), me).wait_recv()
        for cp in sent:
            cp.wait_send()
        for cp in mine:
            cp.wait()

    return pl.pallas_call(
        body, name=name,
        out_shape=[jax.ShapeDtypeStruct((N_DEV, r, c), x.dtype)] * n,
        in_specs=[pl.BlockSpec(memory_space=pl.ANY)],
        out_specs=[pl.BlockSpec(memory_space=pl.ANY)] * n,
        scratch_shapes=[pltpu.SemaphoreType.DMA((7, n)), pltpu.SemaphoreType.DMA((7, n)),
                        pltpu.SemaphoreType.DMA((n,))],
    )(x)


def _swap_pairs(gs, name):
    n = len(gs)
    _, r, c = gs[0].shape

    def body(*refs):
        g_refs, out_ref, send_sems, recv_sems = refs[:n], refs[n], refs[n + 1], refs[n + 2]
        mx, my, mc = lax.axis_index("x"), lax.axis_index("y"), lax.axis_index("c")

        def copy(q, l):
            return pltpu.make_async_remote_copy(
                src_ref=g_refs[l].at[2 * q + (1 - mc)], dst_ref=out_ref.at[q, l],
                send_sem=send_sems.at[q, l], recv_sem=recv_sems.at[q, l],
                device_id=(mx, my, 1 - mc), device_id_type=MESH)

        copies = [copy(q, l) for q in range(4) for l in range(n)]
        for cp in copies:
            cp.start()
        for cp in copies:
            cp.wait_recv()
        for cp in copies:
            cp.wait_send()

    return pl.pallas_call(
        body, name=name,
        out_shape=jax.ShapeDtypeStruct((4, n, r, c), gs[0].dtype),
        in_specs=[pl.BlockSpec(memory_space=pl.ANY)] * n,
        out_specs=pl.BlockSpec(memory_space=pl.ANY),
        scratch_shapes=[pltpu.SemaphoreType.DMA((4, n)), pltpu.SemaphoreType.DMA((4, n))],
    )(*gs)


def _pair_sum(g, recv, layer, side, name):
    _, r, c = g.shape
    tr = _tile(r, max(SUBLANES, (256 * 1024) // c))

    def body(side_ref, g_ref, r_ref, o_ref):
        del side_ref
        o_ref[...] = (g_ref[...].astype(F32) + r_ref[...].astype(F32)).astype(o_ref.dtype)

    return pl.pallas_call(
        body, name=name,
        grid_spec=pltpu.PrefetchScalarGridSpec(
            num_scalar_prefetch=1, grid=(4, r // tr),
            in_specs=[pl.BlockSpec((None, None, tr, c), lambda q, t, side_ref: (q, side_ref[0], t, 0)),
                      pl.BlockSpec((None, None, tr, c), lambda q, t, side_ref: (q, layer, t, 0))],
            out_specs=pl.BlockSpec((None, tr, c), lambda q, t, side_ref: (q, t, 0))),
        out_shape=jax.ShapeDtypeStruct((4, r, c), g.dtype),
        compiler_params=_params(("parallel", "parallel")),
    )(side, g.reshape(4, 2, r, c), recv)


def _deliver_chips(ps, name):
    n = len(ps)
    _, r, c = ps[0].shape

    def body(*refs):
        p_refs, out_ref, send_sems, recv_sems, local_sems = refs[:n], refs[n], refs[n + 1], refs[n + 2], refs[n + 3]
        mx, my, mc = lax.axis_index("x"), lax.axis_index("y"), lax.axis_index("c")
        q_me = 2 * mx + my
        mine = [pltpu.make_async_copy(p_refs[l].at[q_me], out_ref.at[q_me, l], local_sems.at[l]) for l in range(n)]
        for cp in mine:
            cp.start()
        sent, expected = [], []
        for k in range(1, 4):
            fx, fy = (k >> 1) & 1, k & 1
            px = mx + fx - 2 * mx * fx
            py = my + fy - 2 * my * fy
            q_peer = 2 * px + py
            for l in range(n):
                sent.append(pltpu.make_async_remote_copy(
                    src_ref=p_refs[l].at[q_peer], dst_ref=out_ref.at[q_me, l],
                    send_sem=send_sems.at[k - 1, l], recv_sem=recv_sems.at[k - 1, l],
                    device_id=(px, py, mc), device_id_type=MESH))
                expected.append(pltpu.make_async_remote_copy(
                    src_ref=p_refs[l].at[q_peer], dst_ref=out_ref.at[q_peer, l],
                    send_sem=send_sems.at[k - 1, l], recv_sem=recv_sems.at[k - 1, l],
                    device_id=(px, py, mc), device_id_type=MESH))
        for cp in sent:
            cp.start()
        for cp in expected:
            cp.wait_recv()
        for cp in sent:
            cp.wait_send()
        for cp in mine:
            cp.wait()

    return pl.pallas_call(
        body, name=name,
        out_shape=jax.ShapeDtypeStruct((4, n, r, c), ps[0].dtype),
        in_specs=[pl.BlockSpec(memory_space=pl.ANY)] * n,
        out_specs=pl.BlockSpec(memory_space=pl.ANY),
        scratch_shapes=[pltpu.SemaphoreType.DMA((3, n)), pltpu.SemaphoreType.DMA((3, n)),
                        pltpu.SemaphoreType.DMA((n,))],
    )(*ps)


def _pair_sums(g, name):
    side = lax.axis_index("c").astype(jnp.int32).reshape(1)
    return _pair_sum(g, _swap_pairs([g], name + "_swap"), 0, side, name + "_pairsum")


class _GatherRider:
    def __init__(self, x):
        self.inputs = [x]
        self.out_shapes = [jax.ShapeDtypeStruct((N_DEV,) + x.shape, x.dtype)]
        self.scratch = [pltpu.SemaphoreType.DMA((7,)), pltpu.SemaphoreType.DMA((7,)), pltpu.SemaphoreType.DMA]

    @staticmethod
    def _plan(ins, outs, sems):
        x_ref, out_ref = ins[0], outs[0]
        send_sems, recv_sems, local_sem = sems
        mx, my, mc = lax.axis_index("x"), lax.axis_index("y"), lax.axis_index("c")
        me, sibling = (mx, my, mc), (mx, my, 1 - mc)
        chips = [(1 - mx, my), (mx, 1 - my), (1 - mx, 1 - my)]

        def slot(px, py, pc):
            return out_ref.at[4 * px + 2 * py + pc]

        def copy(k, block, to, from_shard=False):
            return pltpu.make_async_remote_copy(
                src_ref=x_ref if from_shard else slot(*block), dst_ref=slot(*block),
                send_sem=send_sems.at[k], recv_sem=recv_sems.at[k], device_id=to, device_id_type=MESH)

        mine = pltpu.make_async_copy(x_ref, slot(*me), local_sem)
        first = [copy(0, me, sibling, True)] + [copy(1 + j, me, (*chip, mc), True) for j, chip in enumerate(chips)]
        return me, sibling, chips, mc, copy, mine, first

    def start(self, ins, outs, sems):
        _, _, _, _, _, mine, first = self._plan(ins, outs, sems)
        mine.start()
        for cp in first:
            cp.start()

    def finish(self, ins, outs, sems):
        me, sibling, chips, mc, copy, mine, first = self._plan(ins, outs, sems)
        passed = []
        for j, chip in enumerate(chips):
            copy(1 + j, (*chip, mc), me).wait_recv()
            passed.append(copy(4 + j, (*chip, mc), sibling))
            passed[-1].start()
        copy(0, sibling, me).wait_recv()
        for j, chip in enumerate(chips):
            copy(4 + j, (*chip, 1 - mc), me).wait_recv()
        for cp in first + passed:
            cp.wait_send()
        mine.wait()


class _DeliverRider:
    def __init__(self, ps):
        self.inputs = [ps]
        self.out_shapes = [jax.ShapeDtypeStruct(ps.shape, ps.dtype)]
        self.scratch = [pltpu.SemaphoreType.DMA((3,)), pltpu.SemaphoreType.DMA((3,)), pltpu.SemaphoreType.DMA]

    @staticmethod
    def _plan(ins, outs, sems):
        p_ref, out_ref = ins[0], outs[0]
        send_sems, recv_sems, local_sem = sems
        mx, my, mc = lax.axis_index("x"), lax.axis_index("y"), lax.axis_index("c")
        q_me = 2 * mx + my
        mine = pltpu.make_async_copy(p_ref.at[q_me], out_ref.at[q_me], local_sem)
        sent, expected = [], []
        for k in range(1, 4):
            fx, fy = (k >> 1) & 1, k & 1
            px = mx + fx - 2 * mx * fx
            py = my + fy - 2 * my * fy
            q_peer = 2 * px + py
            for dst, into in ((q_me, sent), (q_peer, expected)):
                into.append(pltpu.make_async_remote_copy(
                    src_ref=p_ref.at[q_peer], dst_ref=out_ref.at[dst], send_sem=send_sems.at[k - 1],
                    recv_sem=recv_sems.at[k - 1], device_id=(px, py, mc), device_id_type=MESH))
        return mine, sent, expected

    def start(self, ins, outs, sems):
        mine, sent, _ = self._plan(ins, outs, sems)
        mine.start()
        for cp in sent:
            cp.start()

    def finish(self, ins, outs, sems):
        mine, sent, expected = self._plan(ins, outs, sems)
        for cp in expected:
            cp.wait_recv()
        for cp in sent:
            cp.wait_send()
        mine.wait()


def _slot_sum(slots, name):
    ns, r, c = slots.shape
    tr = _tile(r, 256)

    def body(s_ref, o_ref):
        acc = s_ref[0].astype(F32)
        for s in range(1, ns):
            acc = acc + s_ref[s].astype(F32)
        o_ref[...] = acc

    return pl.pallas_call(
        body, name=name, grid=(r // tr,),
        in_specs=[pl.BlockSpec((ns, tr, c), lambda i: (0, i, 0))],
        out_specs=pl.BlockSpec((tr, c), lambda i: (i, 0)),
        out_shape=jax.ShapeDtypeStruct((r, c), F32),
        compiler_params=_params(("parallel",)),
    )(slots)


def _adamw(slots, w, m, v, name):
    ns, r, c = slots.shape
    tr = _tile(r, max(SUBLANES, (128 * 1024) // c))

    def body(s_ref, w_ref, m_ref, v_ref, g_out, d_out, m_out, v_out):
        g = s_ref[0].astype(F32)
        for s in range(1, ns):
            g = g + s_ref[s].astype(F32)
        mn = ADAM_B1 * m_ref[...] + (1.0 - ADAM_B1) * g
        vn = ADAM_B2 * v_ref[...] + (1.0 - ADAM_B2) * (g * g)
        m_hat = mn / (1.0 - ADAM_B1 ** ADAM_STEP)
        v_hat = vn / (1.0 - ADAM_B2 ** ADAM_STEP)
        g_out[...] = g
        d_out[...] = -ADAM_LR * (m_hat / (jnp.sqrt(v_hat) + ADAM_EPS) + ADAM_WD * w_ref[...])
        m_out[...] = mn
        v_out[...] = vn

    blk = pl.BlockSpec((tr, c), lambda i: (i, 0))
    return pl.pallas_call(
        body, name=name, grid=(r // tr,),
        in_specs=[pl.BlockSpec((ns, tr, c), lambda i: (0, i, 0)), blk, blk, blk],
        out_specs=[blk] * 4,
        out_shape=[jax.ShapeDtypeStruct((r, c), F32)] * 4,
        compiler_params=_params(("parallel",)),
    )(slots, w, m, v)


def _pack(arrays, dtype, row_multiple=SUBLANES):
    flat = jnp.concatenate([a.astype(dtype).reshape(-1) for a in arrays])
    unit = row_multiple * PACK_COLS
    total = -(-flat.shape[0] // unit) * unit
    if total != flat.shape[0]:
        flat = jnp.pad(flat, (0, total - flat.shape[0]))
    return flat.reshape(-1, PACK_COLS)


def _unpack(buf, shapes):
    flat = buf.reshape(-1)
    out, o = [], 0
    for s in shapes:
        n = int(np.prod(s))
        out.append(flat[o:o + n].reshape(s))
        o += n
    return out


FFN = ("w_up", "w_down")
BIG = ("w_ple_proj", "w_ple_gate", "ev_w_in", "ev_w_out", "od_w_in", "od_w_out")
BIG_COL_SHARDED = {"w_ple_proj": True, "w_ple_gate": False, "ev_w_in": True, "ev_w_out": False,
                   "od_w_in": True, "od_w_out": False}
SMALL_SHARDED = ("dn_conv_w", "lru_conv_w")
SMALL_REPLICATED = ("ln_mix_w", "ln_mlp_w", "ln_ple_w", "ln_final_w", "dn_a_log", "dn_dt_bias", "dn_norm_w",
                    "lru_conv_b", "lru_wa", "lru_ba", "lru_wx", "lru_bx", "lru_lambda")
WEIGHTS = ("ln_mix_w", "ln_mlp_w", "ln_ple_w", "w_up", "w_down", "w_ple_proj", "w_ple_gate", "ln_final_w",
           "ev_w_in", "ev_w_out", "dn_conv_w", "dn_a_log", "dn_dt_bias", "dn_norm_w", "lru_conv_w", "lru_conv_b",
           "lru_wa", "lru_ba", "lru_wx", "lru_bx", "lru_lambda", "od_w_in", "od_w_out")


def kernel(x, p, ln_mix_w, ln_mlp_w, ln_ple_w, w_up, w_down, w_ple_proj, w_ple_gate, ln_final_w, ev_w_in, ev_w_out, dn_conv_w, dn_a_log, dn_dt_bias, dn_norm_w, lru_conv_w, lru_conv_b, lru_wa, lru_ba, lru_wx, lru_bx, lru_lambda, od_w_in, od_w_out, loss_target, m_ln_mix_w, m_ln_mlp_w, m_ln_ple_w, m_w_up, m_w_down, m_w_ple_proj, m_w_ple_gate, m_ln_final_w, m_ev_w_in, m_ev_w_out, m_dn_conv_w, m_dn_a_log, m_dn_dt_bias, m_dn_norm_w, m_lru_conv_w, m_lru_conv_b, m_lru_wa, m_lru_ba, m_lru_wx, m_lru_bx, m_lru_lambda, m_od_w_in, m_od_w_out, v_ln_mix_w, v_ln_mlp_w, v_ln_ple_w, v_w_up, v_w_down, v_w_ple_proj, v_w_ple_gate, v_ln_final_w, v_ev_w_in, v_ev_w_out, v_dn_conv_w, v_dn_a_log, v_dn_dt_bias, v_dn_norm_w, v_lru_conv_w, v_lru_conv_b, v_lru_wa, v_lru_ba, v_lru_wx, v_lru_bx, v_lru_lambda, v_od_w_in, v_od_w_out):
    w = dict(ln_mix_w=ln_mix_w, ln_mlp_w=ln_mlp_w, ln_ple_w=ln_ple_w, w_up=w_up, w_down=w_down,
             w_ple_proj=w_ple_proj, w_ple_gate=w_ple_gate, ln_final_w=ln_final_w, ev_w_in=ev_w_in,
             ev_w_out=ev_w_out, dn_conv_w=dn_conv_w, dn_a_log=dn_a_log, dn_dt_bias=dn_dt_bias,
             dn_norm_w=dn_norm_w, lru_conv_w=lru_conv_w, lru_conv_b=lru_conv_b, lru_wa=lru_wa, lru_ba=lru_ba,
             lru_wx=lru_wx, lru_bx=lru_bx, lru_lambda=lru_lambda, od_w_in=od_w_in, od_w_out=od_w_out)
    m = dict(ln_mix_w=m_ln_mix_w, ln_mlp_w=m_ln_mlp_w, ln_ple_w=m_ln_ple_w, w_up=m_w_up, w_down=m_w_down,
             w_ple_proj=m_w_ple_proj, w_ple_gate=m_w_ple_gate, ln_final_w=m_ln_final_w, ev_w_in=m_ev_w_in,
             ev_w_out=m_ev_w_out, dn_conv_w=m_dn_conv_w, dn_a_log=m_dn_a_log, dn_dt_bias=m_dn_dt_bias,
             dn_norm_w=m_dn_norm_w, lru_conv_w=m_lru_conv_w, lru_conv_b=m_lru_conv_b, lru_wa=m_lru_wa,
             lru_ba=m_lru_ba, lru_wx=m_lru_wx, lru_bx=m_lru_bx, lru_lambda=m_lru_lambda, od_w_in=m_od_w_in,
             od_w_out=m_od_w_out)
    v = dict(ln_mix_w=v_ln_mix_w, ln_mlp_w=v_ln_mlp_w, ln_ple_w=v_ln_ple_w, w_up=v_w_up, w_down=v_w_down,
             w_ple_proj=v_w_ple_proj, w_ple_gate=v_w_ple_gate, ln_final_w=v_ln_final_w, ev_w_in=v_ev_w_in,
             ev_w_out=v_ev_w_out, dn_conv_w=v_dn_conv_w, dn_a_log=v_dn_a_log, dn_dt_bias=v_dn_dt_bias,
             dn_norm_w=v_dn_norm_w, lru_conv_w=v_lru_conv_w, lru_conv_b=v_lru_conv_b, lru_wa=v_lru_wa,
             lru_ba=v_lru_ba, lru_wx=v_lru_wx, lru_bx=v_lru_bx, lru_lambda=v_lru_lambda, od_w_in=v_od_w_in,
             od_w_out=v_od_w_out)
    me = 4 * lax.axis_index("x") + 2 * lax.axis_index("y") + lax.axis_index("c")

    big = {}
    for n in BIG:
        shards = _gather_layers(w[n].astype(BF16), "gather_" + n)
        big[n] = [s if BIG_COL_SHARDED[n] else s.reshape(-1, s.shape[2]) for s in shards]
    conv_shapes = [w[n].shape for n in SMALL_SHARDED]
    conv_g = _all_gather(_pack([w[n] for n in SMALL_SHARDED], F32), "gather_conv")
    conv_dev = [_unpack(conv_g[s], conv_shapes) for s in range(N_DEV)]
    small = {n: w[n] for n in SMALL_REPLICATED}
    for i, n in enumerate(SMALL_SHARDED):
        small[n] = jnp.concatenate([conv_dev[s][i] for s in range(N_DEV)], axis=-1)

    loss_local, (g_big, g_ffn, g_small, g_x) = jax.value_and_grad(_local_loss, argnums=(0, 1, 2, 3))(
        big, {n: w[n] for n in FFN}, small, x[0], p[:, 0], loss_target[0])
    loss = lax.psum(loss_local, ("x", "y", "c"))

    side = lax.axis_index("c").astype(jnp.int32).reshape(1)
    out = {}
    for n in BIG:
        nl, r, c = w[n].shape
        gs = [g.reshape(N_DEV, r, c) for g in g_big[n]]
        recv = _swap_pairs(gs, "swap_" + n)
        sums = [_pair_sum(g, recv, l, side, "pairsum_" + n) for l, g in enumerate(gs)]
        got = _deliver_chips(sums, "deliver_" + n)
        res = _adamw(got.reshape(4, nl * r, c), w[n].reshape(nl * r, c), m[n].reshape(nl * r, c),
                     v[n].reshape(nl * r, c), "adamw_" + n)
        for kind, buf in zip(("grad", "delta", "new_m", "new_v"), res):
            out[kind, n] = buf.reshape(nl, r, c)
    for n in FFN:
        nl, r, c = w[n].shape
        res = _adamw(g_ffn[n].reshape(1, nl * r, c), w[n].reshape(nl * r, c), m[n].reshape(nl * r, c),
                     v[n].reshape(nl * r, c), "adamw_" + n)
        for kind, buf in zip(("grad", "delta", "new_m", "new_v"), res):
            out[kind, n] = buf.reshape(nl, r, c)

    small_names = SMALL_REPLICATED + SMALL_SHARDED
    small_shapes = [small[n].shape for n in small_names]
    all_small = _all_gather(_pack([g_small[n] for n in small_names], F32), "gather_small_grads")
    total = dict(zip(small_names, _unpack(_slot_sum(all_small, "sum_small_grads"), small_shapes)))
    for n in SMALL_SHARDED:
        width = w[n].shape[-1]
        total[n] = lax.dynamic_slice_in_dim(total[n], me * width, width, axis=-1)
    own_shapes = [w[n].shape for n in small_names]
    res_small = _adamw(_pack([total[n] for n in small_names], F32)[None], _pack([w[n] for n in small_names], F32),
                       _pack([m[n] for n in small_names], F32), _pack([v[n] for n in small_names], F32), "adamw_small")
    for kind, buf in zip(("grad", "delta", "new_m", "new_v"), res_small):
        for n, a in zip(small_names, _unpack(buf, own_shapes)):
            out[kind, n] = a

    return (loss, g_x[None], *[out["grad", n] for n in WEIGHTS], *[out["delta", n] for n in WEIGHTS],
            *[out["new_m", n] for n in WEIGHTS], *[out["new_v", n] for n in WEIGHTS])
```

```python
import functools

import numpy as np
import jax
import jax.numpy as jnp
from jax import lax
from jax.experimental import pallas as pl
from jax.experimental.pallas import tpu as pltpu

F32 = jnp.float32
BF16 = jnp.bfloat16
N_DEV = 8
LANES = 128
SUBLANES = 8
VMEM_LIMIT = 56 * 1024 * 1024
PACK_COLS = 1024
NORM_EPS = 1e-6
GN_EPS = 1e-5
DN_CHUNK = 64
RET_CHUNK = 64
HEAD = 128
RET_DK = 128
RET_DV = 256
SWA_BLOCK = 128
SWA_BRANCHES = ((128, 1), (512, 4), (2048, 16))
LRU_C = 8.0
CONV_W = 4
ADAM_LR, ADAM_B1, ADAM_B2, ADAM_EPS, ADAM_WD, ADAM_STEP = 0.001, 0.9, 0.999, 1e-08, 0.01, 10
NEG = -1e30
MESH = pl.DeviceIdType.MESH


def _params(sem):
    return pltpu.CompilerParams(dimension_semantics=sem, vmem_limit_bytes=VMEM_LIMIT)


def _tile(n, cap):
    for t in (2048, 1024, 896, 768, 640, 512, 384, 256, 128, 64, 32, 16, 8):
        if t <= cap and n % t == 0:
            return t
    return n


def _bdot(a, b, dims):
    return lax.dot_general(a.astype(BF16), b.astype(BF16), (dims, ((), ())), preferred_element_type=F32)


_NN = ((1,), (0,))
_NT = ((1,), (1,))
_TN = ((0,), (0,))


def _hdot(a, b):
    return lax.dot_general(a, b, (_NN, ((), ())), precision=lax.Precision.HIGH, preferred_element_type=F32)


def _sigmoid(x):
    return jax.nn.sigmoid(x)


def _silu(x):
    return x * _sigmoid(x)


def _softplus(x):
    return jnp.maximum(x, 0.0) + jnp.log(1.0 + jnp.exp(-jnp.abs(x)))


def _gelu(x):
    return 0.5 * x * (1.0 + jnp.tanh(0.7978845608028654 * (x + 0.044715 * (x * x * x))))


def _mm_call(a, b, *, ta=False, tb=False, extras=(), epilogue=None, out_dtypes=(F32,), b_slots=False,
             out_slots=False, rider=None, name):
    m, k = (a.shape[1], a.shape[0]) if ta else a.shape
    ne, no = len(extras), len(out_dtypes)
    cap_n = 512 if ne + no > 2 else 1024
    shard = b.shape[2] if b_slots else None
    if b_slots:
        n = b.shape[1] if tb else N_DEV * shard
    else:
        n = b.shape[0] if tb else b.shape[1]
    if out_slots:
        shard = n // N_DEV
    tm = _tile(m, 1024)
    tn = _tile(shard if (out_slots or (b_slots and not tb)) else n, cap_n)
    cap_k = 2048 if (a.dtype == BF16 and b.dtype == BF16) else 1024
    tk = _tile(shard if (b_slots and tb) else k, cap_k)
    nk = k // tk
    dims = ((0,) if ta else (1,), (1,) if tb else (0,))

    gm, gn = m // tm, n // tn
    nri = 0 if rider is None else len(rider.inputs)
    nro = 0 if rider is None else len(rider.out_shapes)

    def body(*refs):
        a_ref, b_ref = refs[0], refs[1]
        ex = refs[2:2 + ne]
        r_in = refs[2 + ne:2 + ne + nri]
        outs = refs[2 + ne + nri:2 + ne + nri + no]
        r_out = refs[2 + ne + nri + no:2 + ne + nri + no + nro]
        acc = refs[2 + ne + nri + no + nro]
        r_sems = refs[3 + ne + nri + no + nro:]
        kk = pl.program_id(2)
        if rider is not None:
            at_i, at_j = pl.program_id(0), pl.program_id(1)

            @pl.when(jnp.logical_and(jnp.logical_and(at_i == 0, at_j == 0), kk == 0))
            def _():
                rider.start(r_in, r_out, r_sems)

        def finish(total):
            res = (total,) if epilogue is None else epilogue(total, *[e[...] for e in ex])
            for o, r in zip(outs, res):
                o[...] = r.astype(o.dtype)

        if nk == 1:
            finish(_bdot(a_ref[...], b_ref[...], dims))
        else:
            @pl.when(kk == 0)
            def _():
                acc[...] = _bdot(a_ref[...], b_ref[...], dims)

            @pl.when(kk > 0)
            def _():
                acc[...] += _bdot(a_ref[...], b_ref[...], dims)

            @pl.when(kk == nk - 1)
            def _():
                finish(acc[...])

        if rider is not None:
            @pl.when(jnp.logical_and(jnp.logical_and(at_i == gm - 1, at_j == gn - 1), kk == nk - 1))
            def _():
                rider.finish(r_in, r_out, r_sems)

    a_spec = pl.BlockSpec((tk, tm), lambda i, j, kk: (kk, i)) if ta else pl.BlockSpec((tm, tk), lambda i, j, kk: (i, kk))
    if b_slots and tb:
        per = shard // tk
        b_spec = pl.BlockSpec((None, tn, tk), lambda i, j, kk: (kk // per, j, kk % per))
    elif b_slots:
        per = shard // tn
        b_spec = pl.BlockSpec((None, tk, tn), lambda i, j, kk: (j // per, kk, j % per))
    elif tb:
        b_spec = pl.BlockSpec((tn, tk), lambda i, j, kk: (j, kk))
    else:
        b_spec = pl.BlockSpec((tk, tn), lambda i, j, kk: (kk, j))
    mn_spec = pl.BlockSpec((tm, tn), lambda i, j, kk: (i, j))
    if out_slots:
        per_o = shard // tn
        out_specs = [pl.BlockSpec((None, tm, tn), lambda i, j, kk: (j // per_o, i, j % per_o))]
        out_shape = [jax.ShapeDtypeStruct((N_DEV, m, shard), out_dtypes[0])]
    else:
        out_specs = [mn_spec] * no
        out_shape = [jax.ShapeDtypeStruct((m, n), d) for d in out_dtypes]
    hbm = pl.BlockSpec(memory_space=pl.ANY)
    r_inputs = [] if rider is None else list(rider.inputs)
    return pl.pallas_call(
        body, name=name, grid=(gm, gn, nk),
        in_specs=[a_spec, b_spec] + [mn_spec] * ne + [hbm] * nri,
        out_specs=out_specs + [hbm] * nro,
        out_shape=out_shape + ([] if rider is None else list(rider.out_shapes)),
        scratch_shapes=[pltpu.VMEM((tm, tn), F32)] + ([] if rider is None else list(rider.scratch)),
        compiler_params=_params(("arbitrary", "arbitrary", "arbitrary") if rider is not None
                                else ("parallel", "parallel", "arbitrary")),
    )(a, b, *extras, *r_inputs)


def _reduced(got, name):
    return _slot_sum(got, name + "_sum")


def _make_mm(name, out_dtype, slots=False, to_slots=None):
    @jax.custom_vjp
    def op(a, w, shard):
        return _mm_call(a, w, out_dtypes=(out_dtype,), b_slots=slots, name=name + "_f")[0]

    def fwd(a, w, shard):
        return _mm_call(a, w, out_dtypes=(out_dtype,), b_slots=slots, name=name + "_f")[0], (a, w)

    def bwd(res, dy):
        a, w = res
        dy = dy.astype(BF16)
        dw = _mm_call(a, dy, ta=True, out_dtypes=(BF16,), out_slots=slots, name=name + "_dw")[0]
        sums = _pair_sums(dw if to_slots is None else to_slots(dw), name + "_w")
        da, got = _mm_call(dy, w, tb=True, out_dtypes=(a.dtype,), b_slots=slots, rider=_DeliverRider(sums),
                           name=name + "_da")
        return da, jnp.zeros_like(w), _reduced(got, name + "_w")

    op.defvjp(fwd, bwd)
    return op


def _make_mm_res(name):
    def call(a, w, h):
        return _mm_call(a, w, extras=(h,), epilogue=lambda acc, hv: (hv + acc,), out_dtypes=(F32,), name=name + "_f")[0]

    @jax.custom_vjp
    def op(a, w, shard, h):
        return call(a, w, h)

    def fwd(a, w, shard, h):
        return call(a, w, h), (a, w)

    def bwd(res, dy):
        a, w = res
        dyb = dy.astype(BF16)
        dw = _mm_call(a, dyb, ta=True, out_dtypes=(BF16,), name=name + "_dw")[0]
        sums = _pair_sums(dw.reshape(N_DEV, -1, dw.shape[1]), name + "_w")
        da, got = _mm_call(dyb, w, tb=True, out_dtypes=(a.dtype,), rider=_DeliverRider(sums), name=name + "_da")
        return da, jnp.zeros_like(w), _reduced(got, name + "_w"), dy

    op.defvjp(fwd, bwd)
    return op


def _make_ffn(name):
    def forward(hn, w_up, w_down, h):
        def ep(acc):
            r = jnp.maximum(acc, 0.0)
            return acc, r * r
        wu = _gather_layers(w_up.astype(BF16)[None], name + "_gather_up")[0]
        u, act, wd = _mm_call(hn, wu, epilogue=ep, out_dtypes=(BF16, BF16), b_slots=True,
                              rider=_GatherRider(w_down.astype(BF16)), name=name + "_up")
        wd = wd.reshape(-1, wd.shape[2])
        out = _mm_call(act, wd, extras=(h,), epilogue=lambda acc, hv: (hv + acc,), out_dtypes=(F32,), name=name + "_down")[0]
        return out, (hn, wu, wd, u, act)

    @jax.custom_vjp
    def op(hn, w_up, w_down, h):
        return forward(hn, w_up, w_down, h)[0]

    def bwd(res, dy):
        hn, wu, wd, u, act = res
        dyb = dy.astype(BF16)
        d_wdown = _mm_call(act, dyb, ta=True, out_dtypes=(BF16,), name=name + "_dwdown")[0]
        ps_down = _pair_sums(d_wdown.reshape(N_DEV, -1, d_wdown.shape[1]), name + "_wdown")
        d_u, got_down = _mm_call(dyb, wd, tb=True, extras=(u,),
                                 epilogue=lambda acc, uv: (acc * (2.0 * jnp.maximum(uv.astype(F32), 0.0)),),
                                 out_dtypes=(BF16,), rider=_DeliverRider(ps_down), name=name + "_du")
        d_wup = _mm_call(hn, d_u, ta=True, out_dtypes=(BF16,), out_slots=True, name=name + "_dwup")[0]
        ps_up = _pair_sums(d_wup, name + "_wup")
        d_hn, got_up = _mm_call(d_u, wu, tb=True, out_dtypes=(hn.dtype,), b_slots=True,
                                rider=_DeliverRider(ps_up), name=name + "_dhn")
        return d_hn, _slot_sum(got_up, name + "_sum_wup"), _slot_sum(got_down, name + "_sum_wdown"), dy

    op.defvjp(forward, bwd)
    return op


def _make_ple(name):
    def forward(hn, w_gate, gate_shard, p, w_proj, proj_shard, h):
        pp = _mm_call(p, w_proj, out_dtypes=(F32,), b_slots=True, name=name + "_proj")[0]
        out, gp = _mm_call(hn, w_gate, extras=(h, pp),
                           epilogue=lambda acc, hv, ppv: (hv + _sigmoid(acc) * ppv, acc),
                           out_dtypes=(F32, F32), name=name + "_gate")
        return out, (hn, w_gate, p, w_proj, gp, pp)

    @jax.custom_vjp
    def op(hn, w_gate, gate_shard, p, w_proj, proj_shard, h):
        return forward(hn, w_gate, gate_shard, p, w_proj, proj_shard, h)[0]

    def bwd(res, dy):
        hn, w_gate, p, w_proj, gp, pp = res

        def gate_grads(g, dyv, gpv, ppv):
            s = _sigmoid(gpv)
            return (dyv * ppv * s * (1.0 - s)).astype(BF16), (dyv * s).astype(BF16)

        t, d = dy.shape
        d_gp, d_pp = _rowmap_call(gate_grads, t, 1, [dy, gp, pp], [], [(d, BF16), (d, BF16)], name + "_dgate")
        d_wproj = _mm_call(p, d_pp, ta=True, out_dtypes=(BF16,), out_slots=True, name=name + "_dwproj")[0]
        ps_proj = _pair_sums(d_wproj, name + "_wproj")
        d_wgate, got_proj = _mm_call(hn, d_gp, ta=True, out_dtypes=(BF16,), rider=_DeliverRider(ps_proj),
                                     name=name + "_dwgate")
        ps_gate = _pair_sums(d_wgate.reshape(N_DEV, -1, d_wgate.shape[1]), name + "_wgate")
        d_hn, got_gate = _mm_call(d_gp, w_gate, tb=True, out_dtypes=(hn.dtype,), rider=_DeliverRider(ps_gate),
                                  name=name + "_dhn")
        return (d_hn, jnp.zeros_like(w_gate), _reduced(got_gate, name + "_wgate"), jnp.zeros_like(p),
                jnp.zeros_like(w_proj), _reduced(got_proj, name + "_wproj"), dy)

    op.defvjp(forward, bwd)
    return op


def _row_tile(t, widths):
    return _tile(t, max(SUBLANES, (256 * 1024) // max(widths)))


def _rowmap_specs(t, g, rows, bcs, tt):
    row_specs = [pl.BlockSpec((tt, r.shape[1] // g), lambda gg, i: (i, gg)) for r in rows]
    bc_specs = []
    for b, per_group in bcs:
        if per_group:
            bc_specs.append(pl.BlockSpec((b.shape[0], b.shape[1] // g), lambda gg, i: (0, gg)))
        else:
            bc_specs.append(pl.BlockSpec(b.shape, lambda gg, i: (0, 0)))
    return row_specs, bc_specs


def _rowmap_call(fn, t, g, rows, bcs, outs, name):
    widths = [r.shape[1] // g for r in rows] + [c // g for c, _ in outs]
    tt = _row_tile(t, widths)
    nr, nb = len(rows), len(bcs)
    row_specs, bc_specs = _rowmap_specs(t, g, rows, bcs, tt)

    def body(*refs):
        vals = [r[...] for r in refs[:nr + nb]]
        res = fn(pl.program_id(0), *vals)
        for o, r in zip(refs[nr + nb:], res):
            o[...] = r.astype(o.dtype)

    return pl.pallas_call(
        body, name=name, grid=(g, t // tt),
        in_specs=row_specs + bc_specs,
        out_specs=[pl.BlockSpec((tt, c // g), lambda gg, i: (i, gg)) for c, _ in outs],
        out_shape=[jax.ShapeDtypeStruct((t, c), d) for c, d in outs],
        compiler_params=_params(("parallel", "parallel")),
    )(*rows, *[b for b, _ in bcs])


def _rowmap_bwd_call(fn, t, g, rows, bcs, cots, name, add0=None):
    widths = [r.shape[1] // g for r in rows] + [c.shape[1] // g for c in cots]
    tt = _row_tile(t, widths)
    nr, nb, nc = len(rows), len(bcs), len(cots)
    na = 0 if add0 is None else 1
    row_specs, bc_specs = _rowmap_specs(t, g, rows, bcs, tt)
    cot_specs = [pl.BlockSpec((tt, c.shape[1] // g), lambda gg, i: (i, gg)) for c in cots]
    add_specs = [] if add0 is None else [row_specs[0]]
    shared = [not per_group for _, per_group in bcs]

    def body(*refs):
        ins = refs[:nr + nb]
        cot_refs = refs[nr + nb:nr + nb + nc]
        add_refs = refs[nr + nb + nc:nr + nb + nc + na]
        d_rows = refs[nr + nb + nc + na:nr + nb + nc + na + nr]
        d_bcs = refs[nr + nb + nc + na + nr:]
        gg, i = pl.program_id(0), pl.program_id(1)
        vals = [r[...] for r in ins]
        _, vjp = jax.vjp(lambda *v: tuple(fn(gg, *v)), *vals)
        grads = vjp(tuple(c[...] for c in cot_refs))
        for j, (o, gr) in enumerate(zip(d_rows, grads[:nr])):
            if j == 0 and na:
                gr = gr + add_refs[0][...]
            o[...] = gr.astype(o.dtype)
        for o, gr, sh in zip(d_bcs, grads[nr:], shared):
            first = jnp.logical_and(i == 0, gg == 0) if sh else i == 0

            @pl.when(first)
            def _():
                o[...] = jnp.zeros_like(o)

            o[...] += gr.astype(o.dtype)

    res = pl.pallas_call(
        body, name=name, grid=(g, t // tt),
        in_specs=row_specs + bc_specs + cot_specs + add_specs,
        out_specs=row_specs + bc_specs,
        out_shape=[jax.ShapeDtypeStruct(r.shape, r.dtype) for r in rows]
        + [jax.ShapeDtypeStruct(b.shape, F32) for b, _ in bcs],
        compiler_params=_params(("arbitrary", "arbitrary")),
    )(*rows, *[b for b, _ in bcs], *cots, *([] if add0 is None else [add0]))
    return res[:nr], res[nr:]


def _make_rowmap(fn, g, n_rows, per_group, outs, name):
    def call(*args):
        rows, bcs = list(args[:n_rows]), list(zip(args[n_rows:], per_group))
        return tuple(_rowmap_call(fn, rows[0].shape[0], g, rows, bcs, outs, name + "_f"))

    @jax.custom_vjp
    def op(*args):
        return call(*args)

    def fwd(*args):
        return call(*args), args

    def bwd(args, cots):
        rows, bcs = list(args[:n_rows]), list(zip(args[n_rows:], per_group))
        d_rows, d_bcs = _rowmap_bwd_call(fn, rows[0].shape[0], g, rows, bcs, list(cots), name + "_b")
        return tuple(d_rows) + tuple(d.astype(b.dtype) for d, (b, _) in zip(d_bcs, bcs))

    op.defvjp(fwd, bwd)
    return op


def _rms_fn(g, h, w):
    y = h * lax.rsqrt(jnp.mean(h * h, axis=-1, keepdims=True) + NORM_EPS)
    return ((y * w).astype(BF16),)


def _make_norm(name):
    def call(h, w):
        return _rowmap_call(_rms_fn, h.shape[0], 1, [h], [(w, False)], [(h.shape[1], BF16)], name + "_f")[0]

    @jax.custom_vjp
    def op(h, w):
        return h, call(h, w)

    def fwd(h, w):
        return (h, call(h, w)), (h, w)

    def bwd(res, cots):
        h, w = res
        dh_pass, dhn = cots
        d_rows, d_bcs = _rowmap_bwd_call(_rms_fn, h.shape[0], 1, [h], [(w, False)], [dhn], name + "_b", add0=dh_pass)
        return d_rows[0], d_bcs[0]

    op.defvjp(fwd, bwd)
    return op


def _loss_call(h, w, target, name):
    t, d = h.shape
    tt = _row_tile(t, [d])

    def body(h_ref, w_ref, t_ref, dh_ref, dw_ref, loss_ref):
        i = pl.program_id(0)
        tgt = t_ref[...]

        def lf(hv, wv):
            y = hv * lax.rsqrt(jnp.mean(hv * hv, axis=-1, keepdims=True) + NORM_EPS) * wv
            err = y - tgt
            return 0.5 * jnp.sum(jnp.mean(err * err, axis=-1, keepdims=True))

        lv, (dh, dw) = jax.value_and_grad(lf, argnums=(0, 1))(h_ref[...], w_ref[...])
        dh_ref[...] = dh

        @pl.when(i == 0)
        def _():
            dw_ref[...] = jnp.zeros_like(dw_ref)
            loss_ref[...] = jnp.zeros_like(loss_ref)

        dw_ref[...] += dw
        loss_ref[...] += jnp.full(loss_ref.shape, lv, F32)

    row = pl.BlockSpec((tt, d), lambda i: (i, 0))
    return pl.pallas_call(
        body, name=name, grid=(t // tt,),
        in_specs=[row, pl.BlockSpec((1, d), lambda i: (0, 0)), row],
        out_specs=[row, pl.BlockSpec((1, d), lambda i: (0, 0)), pl.BlockSpec((SUBLANES, LANES), lambda i: (0, 0))],
        out_shape=[jax.ShapeDtypeStruct((t, d), F32), jax.ShapeDtypeStruct((1, d), F32),
                   jax.ShapeDtypeStruct((SUBLANES, LANES), F32)],
        compiler_params=_params(("arbitrary",)),
    )(h, w, target)


def _make_loss(name):
    @jax.custom_vjp
    def op(h, w, target):
        return _loss_call(h, w, target, name)[2][0, 0]

    def fwd(h, w, target):
        dh, dw, lv = _loss_call(h, w, target, name)
        return lv[0, 0], (dh, dw, target)

    def bwd(res, ct):
        dh, dw, target = res
        return dh * ct, dw * ct, jnp.zeros_like(target)

    op.defvjp(fwd, bwd)
    return op


def _shift_down(cur, halo, s, first):
    if s == 0:
        return cur
    r = pltpu.roll(cur, s, 0)
    p = jnp.where(first, 0.0, pltpu.roll(halo, s, 0))
    rows = lax.broadcasted_iota(jnp.int32, p.shape, 0)
    head = jnp.where(rows < s, p, r[:SUBLANES])
    return jnp.concatenate([head, r[SUBLANES:]], axis=0)


def _shift_up(cur, halo, s, last):
    if s == 0:
        return cur
    n = cur.shape[0]
    r = pltpu.roll(cur, n - s, 0)
    p = jnp.where(last, 0.0, pltpu.roll(halo, SUBLANES - s, 0))
    rows = lax.broadcasted_iota(jnp.int32, p.shape, 0)
    tail = jnp.where(rows >= SUBLANES - s, p, r[n - SUBLANES:])
    return jnp.concatenate([r[:n - SUBLANES], tail], axis=0)


def _conv_specs(t, c):
    tt, cw = _tile(t, 512), _tile(c, 512)
    per = tt // SUBLANES
    nblk = t // SUBLANES
    cur = pl.BlockSpec((tt, cw), lambda j, i: (i, j))
    prev = pl.BlockSpec((SUBLANES, cw), lambda j, i: (jnp.maximum(i * per - 1, 0), j))
    nxt = pl.BlockSpec((SUBLANES, cw), lambda j, i: (jnp.minimum((i + 1) * per, nblk - 1), j))
    wsp = pl.BlockSpec((CONV_W, cw), lambda j, i: (0, j))
    bsp = pl.BlockSpec((1, cw), lambda j, i: (0, j))
    return tt, cw, cur, prev, nxt, wsp, bsp


def _conv_call(x, w, b, name):
    t, c = x.shape
    tt, cw, cur, prev, nxt, wsp, bsp = _conv_specs(t, c)

    def body(x_ref, p_ref, w_ref, b_ref, y_ref):
        first = pl.program_id(1) == 0
        xv, pv = x_ref[...], p_ref[...]
        y = jnp.zeros_like(xv) + b_ref[...]
        for j in range(CONV_W):
            y = y + w_ref[j:j + 1, :] * _shift_down(xv, pv, CONV_W - 1 - j, first)
        y_ref[...] = y

    return pl.pallas_call(
        body, name=name, grid=(c // cw, t // tt),
        in_specs=[cur, prev, wsp, bsp], out_specs=cur,
        out_shape=jax.ShapeDtypeStruct((t, c), F32),
        compiler_params=_params(("parallel", "parallel")),
    )(x, x, w, b)


def _conv_bwd_call(x, w, dy, name):
    t, c = x.shape
    tt, cw, cur, prev, nxt, wsp, bsp = _conv_specs(t, c)
    nt = t // tt

    def body(x_ref, p_ref, w_ref, dy_ref, n_ref, dx_ref, dw_ref, db_ref):
        i = pl.program_id(1)
        first, last = i == 0, i == nt - 1
        xv, pv, dyv, nv = x_ref[...], p_ref[...], dy_ref[...], n_ref[...]

        @pl.when(first)
        def _():
            dw_ref[...] = jnp.zeros_like(dw_ref)
            db_ref[...] = jnp.zeros_like(db_ref)

        dx = jnp.zeros_like(xv)
        for j in range(CONV_W):
            s = CONV_W - 1 - j
            dx = dx + w_ref[j:j + 1, :] * _shift_up(dyv, nv, s, last)
            dw_ref[j:j + 1, :] += jnp.sum(dyv * _shift_down(xv, pv, s, first), axis=0, keepdims=True)
        dx_ref[...] = dx
        db_ref[...] += jnp.sum(dyv, axis=0, keepdims=True)

    return pl.pallas_call(
        body, name=name, grid=(c // cw, nt),
        in_specs=[cur, prev, wsp, cur, nxt], out_specs=[cur, wsp, bsp],
        out_shape=[jax.ShapeDtypeStruct((t, c), F32), jax.ShapeDtypeStruct((CONV_W, c), F32),
                   jax.ShapeDtypeStruct((1, c), F32)],
        compiler_params=_params(("arbitrary", "arbitrary")),
    )(x, x, w, dy, dy)


def _make_conv(name):
    @jax.custom_vjp
    def op(x, w, b):
        return _conv_call(x, w, b, name + "_f")

    def fwd(x, w, b):
        return _conv_call(x, w, b, name + "_f"), (x, w)

    def bwd(res, dy):
        x, w = res
        return tuple(_conv_bwd_call(x, w, dy, name + "_b"))

    op.defvjp(fwd, bwd)
    return op


def _lru_call(a, u, name):
    t, nb, ln = a.shape
    tt = _tile(t, 1024)
    blk = pl.BlockSpec((tt, nb, ln), lambda i: (i, 0, 0))

    def body(a_ref, u_ref, h_ref, carry):
        @pl.when(pl.program_id(0) == 0)
        def _():
            carry[...] = jnp.zeros_like(carry)

        def step(k, h):
            h = a_ref[k] * h + u_ref[k]
            h_ref[k] = h
            return h

        carry[...] = lax.fori_loop(0, tt, step, carry[...], unroll=8)

    return pl.pallas_call(
        body, name=name, grid=(t // tt,), in_specs=[blk, blk], out_specs=blk,
        out_shape=jax.ShapeDtypeStruct(a.shape, F32), scratch_shapes=[pltpu.VMEM((nb, ln), F32)],
        compiler_params=_params(("arbitrary",)),
    )(a, u)


def _lru_bwd_call(a, hs, dy, name):
    t, nb, ln = a.shape
    tt = _tile(t, 1024)
    nt = t // tt
    blk = pl.BlockSpec((tt, nb, ln), lambda i: (nt - 1 - i, 0, 0))
    prev = pl.BlockSpec((1, nb, ln), lambda i: (jnp.maximum((nt - 1 - i) * tt - 1, 0), 0, 0))

    def body(a_ref, h_ref, hp_ref, dy_ref, da_ref, du_ref, carry):
        i = pl.program_id(0)

        @pl.when(i == 0)
        def _():
            carry[...] = jnp.zeros_like(carry)

        h_before = jnp.where(i == nt - 1, 0.0, hp_ref[0])

        def step(k, c):
            r = tt - 1 - k
            dh = dy_ref[r] + c
            du_ref[r] = dh
            da_ref[r] = dh * h_ref[jnp.maximum(r - 1, 0)]
            return a_ref[r] * dh

        carry[...] = lax.fori_loop(0, tt, step, carry[...], unroll=8)
        da_ref[0] = du_ref[0] * h_before

    return pl.pallas_call(
        body, name=name, grid=(nt,), in_specs=[blk, blk, prev, blk], out_specs=[blk, blk],
        out_shape=[jax.ShapeDtypeStruct(a.shape, F32), jax.ShapeDtypeStruct(a.shape, F32)],
        scratch_shapes=[pltpu.VMEM((nb, ln), F32)],
        compiler_params=_params(("arbitrary",)),
    )(a, hs, hs, dy)


def _make_lru(name):
    @jax.custom_vjp
    def op(a, u):
        return _lru_call(a, u, name + "_f")

    def fwd(a, u):
        hs = _lru_call(a, u, name + "_f")
        return hs, (a, hs)

    def bwd(res, dy):
        a, hs = res
        return tuple(_lru_bwd_call(a, hs, dy, name + "_b"))

    op.defvjp(fwd, bwd)
    return op


def _scan_specs(ins, const, heads, hp, chunk, rev_n):
    def tmap(n_of):
        return lambda hg, n: (n_of(n), hg)
    n_of = (lambda n: rev_n - 1 - n) if rev_n else (lambda n: n)
    in_specs = [pl.BlockSpec((chunk, hp * (x.shape[1] // heads)), tmap(n_of)) for x in ins]
    c_spec = pl.BlockSpec((1, hp * (const.shape[1] // heads)), lambda hg, n: (0, hg))
    return in_specs, c_spec, n_of


def _scan_call(chunk_fn, ins, const, heads, hp, chunk, state_shape, out_width, name):
    t = ins[0].shape[0]
    nc = t // chunk
    ni = len(ins)
    in_specs, c_spec, _ = _scan_specs(ins, const, heads, hp, chunk, 0)
    ws = [x.shape[1] // heads for x in ins]
    cw = const.shape[1] // heads
    dk, dv = state_shape

    def body(*refs):
        in_refs, c_ref, o_ref, s_ref, state = refs[:ni], refs[ni], refs[ni + 1], refs[ni + 2], refs[ni + 3]

        @pl.when(pl.program_id(1) == 0)
        def _():
            state[...] = jnp.zeros_like(state)

        vals = [[r[:, k * w:(k + 1) * w] for r, w in zip(in_refs, ws)] for k in range(hp)]
        consts = [c_ref[:, k * cw:(k + 1) * cw] for k in range(hp)]
        s0 = [state[k] for k in range(hp)]
        outs, s1 = chunk_fn(vals, consts, s0)
        for k in range(hp):
            s_ref[0, k] = s0[k]
            o_ref[:, k * out_width:(k + 1) * out_width] = outs[k]
            state[k] = s1[k]

    return pl.pallas_call(
        body, name=name, grid=(heads // hp, nc),
        in_specs=in_specs + [c_spec],
        out_specs=[pl.BlockSpec((chunk, hp * out_width), lambda hg, n: (n, hg)),
                   pl.BlockSpec((1, hp, dk, dv), lambda hg, n: (n, hg, 0, 0))],
        out_shape=[jax.ShapeDtypeStruct((t, heads * out_width), F32),
                   jax.ShapeDtypeStruct((nc, heads, dk, dv), F32)],
        scratch_shapes=[pltpu.VMEM((hp, dk, dv), F32)],
        compiler_params=_params(("parallel", "arbitrary")),
    )(*ins, const)


def _scan_bwd_call(chunk_fn, ins, const, states, d_out, heads, hp, chunk, state_shape, out_width, name):
    t = ins[0].shape[0]
    nc = t // chunk
    ni = len(ins)
    in_specs, c_spec, n_of = _scan_specs(ins, const, heads, hp, chunk, nc)
    ws = [x.shape[1] // heads for x in ins]
    cw = const.shape[1] // heads
    dk, dv = state_shape

    def body(*refs):
        in_refs, c_ref, s_ref, do_ref = refs[:ni], refs[ni], refs[ni + 1], refs[ni + 2]
        d_refs, dstate = refs[ni + 3:ni + 3 + ni], refs[-1]

        @pl.when(pl.program_id(1) == 0)
        def _():
            dstate[...] = jnp.zeros_like(dstate)

        vals = [[r[:, k * w:(k + 1) * w] for r, w in zip(in_refs, ws)] for k in range(hp)]
        consts = [c_ref[:, k * cw:(k + 1) * cw] for k in range(hp)]
        s0 = [s_ref[0, k] for k in range(hp)]
        d_o = [do_ref[:, k * out_width:(k + 1) * out_width] for k in range(hp)]
        d_s1 = [dstate[k] for k in range(hp)]
        _, vjp = jax.vjp(lambda vv, ss: chunk_fn(vv, consts, ss), vals, s0)
        d_vals, d_s0 = vjp((d_o, d_s1))
        for k in range(hp):
            for r, w, gr in zip(d_refs, ws, d_vals[k]):
                r[:, k * w:(k + 1) * w] = gr
            dstate[k] = d_s0[k]

    return pl.pallas_call(
        body, name=name, grid=(heads // hp, nc),
        in_specs=in_specs + [c_spec,
                             pl.BlockSpec((1, hp, dk, dv), lambda hg, n: (n_of(n), hg, 0, 0)),
                             pl.BlockSpec((chunk, hp * out_width), lambda hg, n: (n_of(n), hg))],
        out_specs=in_specs,
        out_shape=[jax.ShapeDtypeStruct(x.shape, F32) for x in ins],
        scratch_shapes=[pltpu.VMEM((hp, dk, dv), F32)],
        compiler_params=_params(("parallel", "arbitrary")),
    )(*ins, const, states, d_out)


def _make_scan(chunk_fn, const, heads, hp, chunk, state_shape, out_width, name):
    def call(*ins):
        return _scan_call(chunk_fn, list(ins), const, heads, hp, chunk, state_shape, out_width, name + "_f")

    @jax.custom_vjp
    def op(*ins):
        return call(*ins)[0]

    def fwd(*ins):
        o, states = call(*ins)
        return o, (ins, states)

    def bwd(res, d_out):
        ins, states = res
        return tuple(_scan_bwd_call(chunk_fn, list(ins), const, states, d_out, heads, hp, chunk, state_shape,
                                    out_width, name + "_b"))

    op.defvjp(fwd, bwd)
    return op


def _tri(c):
    ri = lax.broadcasted_iota(jnp.int32, (c, c), 0)
    ci = lax.broadcasted_iota(jnp.int32, (c, c), 1)
    return ri, ci


def _each(fn, *lists):
    return [fn(*a) for a in zip(*lists)]


@jax.custom_vjp
def _neumann_inverses(xs):
    c = xs[0].shape[0]
    ri, ci = _tri(c)
    eye = jnp.where(ri == ci, 1.0, 0.0)
    invs = [eye + x for x in xs]
    xps = list(xs)
    for _ in range(max(1, int(np.ceil(np.log2(c))) - 1)):
        xps = [_hdot(xp, xp) for xp in xps]
        invs = [inv + _hdot(inv, xp) for inv, xp in zip(invs, xps)]
    return invs


def _neumann_inverses_fwd(xs):
    invs = _neumann_inverses(xs)
    return invs, invs


def _neumann_inverses_bwd(invs, ds):
    hi = lax.Precision.HIGH
    ts = [lax.dot_general(d, inv, (_NT, ((), ())), precision=hi, preferred_element_type=F32)
          for d, inv in zip(ds, invs)]
    return ([lax.dot_general(inv, t, (_TN, ((), ())), precision=hi, preferred_element_type=F32)
             for inv, t in zip(invs, ts)],)


_neumann_inverses.defvjp(_neumann_inverses_fwd, _neumann_inverses_bwd)


def _dn_chunk(vals, consts, ss):
    del consts
    qs, ks, vs, gbs, bbs = (list(x) for x in zip(*vals))
    c = qs[0].shape[0]
    ri, ci = _tri(c)
    causal, strict = ri >= ci, ri > ci
    tri_f = causal.astype(F32)
    gc_b = [_hdot(tri_f, gb) for gb in gbs]
    gcol = [jnp.mean(x, axis=1, keepdims=True) for x in gc_b]
    grow = [jnp.mean(x.T, axis=0, keepdims=True) for x in gc_b]
    bcol = [jnp.mean(bb, axis=1, keepdims=True) for bb in bbs]
    decay = _each(lambda a, b: jnp.where(causal, jnp.exp(jnp.where(causal, a - b, 0.0)), 0.0), gcol, grow)
    kb = _each(lambda k, b: k * b, ks, bcol)
    m = _each(lambda a, k: _bdot(a, k, _NT), kb, ks)
    invs = _neumann_inverses(_each(lambda mm, d: -jnp.where(strict, mm * d, 0.0), m, decay))
    eg = [jnp.exp(a) for a in gcol]
    u = _each(lambda inv, v, b: _hdot(inv, v * b), invs, vs, bcol)
    w = _each(lambda inv, a, e: _hdot(inv, a * e), invs, kb, eg)
    qk = _each(lambda q, k, d: _bdot(q, k, _NT) * d, qs, ks, decay)
    g_last = [jnp.sum(jnp.mean(gb, axis=1, keepdims=True), axis=0, keepdims=True) for gb in gbs]
    k_dec = _each(lambda k, gl, a: k * jnp.exp(gl - a), ks, g_last, gcol)
    ws = _each(lambda w_, s: _bdot(w_, s, _NN), w, ss)
    v_new = _each(lambda u_, x: u_ - x, u, ws)
    o_state = _each(lambda q, e, s: _bdot(q * e, s, _NN), qs, eg, ss)
    o_intra = _each(lambda a, vn: _bdot(a, vn, _NN), qk, v_new)
    s_add = _each(lambda kd, vn: _bdot(kd, vn, _TN), k_dec, v_new)
    outs = _each(lambda a, b: a + b, o_state, o_intra)
    s_new = _each(lambda s, gl, a: s * jnp.exp(gl) + a, ss, g_last, s_add)
    return outs, s_new


def _ret_chunk(vals, consts, ss):
    qs, ks, vs = (list(x) for x in zip(*vals))
    c = qs[0].shape[0]
    ri, ci = _tri(c)
    rel = (ri - ci).astype(F32)
    idx = lax.broadcasted_iota(jnp.int32, (c, 1), 0).astype(F32)
    lg = [jnp.mean(x, axis=1, keepdims=True) for x in consts]
    dmask = [jnp.where(rel >= 0, jnp.exp(jnp.maximum(rel, 0.0) * a), 0.0) for a in lg]
    ksc = [k * (RET_DK ** -0.5) for k in ks]
    qk = _each(lambda q, k, d: _bdot(q, k, _NT) * d, qs, ksc, dmask)
    intra = _each(lambda a, v: _bdot(a, v, _NN), qk, vs)
    inter = _each(lambda q, a, s: _bdot(q * jnp.exp((idx + 1.0) * a), s, _NN), qs, lg, ss)
    s_add = _each(lambda k, a, v: _bdot(k * jnp.exp((c - 1.0 - idx) * a), v, _TN), ksc, lg, vs)
    outs = _each(lambda a, b: a + b, intra, inter)
    s_new = _each(lambda s, a, x: s * jnp.exp(c * a) + x, ss, lg, s_add)
    return outs, s_new


def _attn_block(args, slopes, first):
    qs, kps, kcs, vps, vcs = (list(x) for x in zip(*args))
    b = qs[0].shape[0]
    ri, ci = _tri(b)
    rel_c = ri - ci
    rel_p = rel_c + b
    ok_c = rel_c >= 0
    ok_p = jnp.logical_and(rel_p <= b, jnp.logical_not(first))
    rel_cf, rel_pf = rel_c.astype(F32), rel_p.astype(F32)
    qsc = [q * (HEAD ** -0.5) for q in qs]
    s_c = _each(lambda q, k, sl: jnp.where(ok_c, _bdot(q, k, _NT) - sl * rel_cf, NEG), qsc, kcs, slopes)
    s_p = _each(lambda q, k, sl: jnp.where(ok_p, _bdot(q, k, _NT) - sl * rel_pf, NEG), qsc, kps, slopes)
    mx = _each(lambda a, c: lax.stop_gradient(jnp.maximum(jnp.max(a, axis=1, keepdims=True),
                                                          jnp.max(c, axis=1, keepdims=True))), s_c, s_p)
    p_c = _each(lambda a, m: jnp.exp(a - m), s_c, mx)
    p_p = _each(lambda a, m: jnp.exp(a - m), s_p, mx)
    den = _each(lambda a, c: jnp.sum(a, axis=1, keepdims=True) + jnp.sum(c, axis=1, keepdims=True), p_c, p_p)
    o_c = _each(lambda a, v: _bdot(a, v, _NN), p_c, vcs)
    o_p = _each(lambda a, v: _bdot(a, v, _NN), p_p, vps)
    outs = _each(lambda a, c, d: (a + c) / d, o_c, o_p, den)
    lses = _each(lambda m, d, o: jnp.broadcast_to(m + jnp.log(d), o.shape), mx, den, outs)
    return outs, lses


ATTN_GROUP = 4


def _attn_plan(t, hw, dil):
    nh = hw // HEAD
    hb = 1 if dil > 1 else max(h for h in (4, 2, 1) if nh % h == 0)
    units = [(r, h) for r in range(dil) for h in range(hb)]
    groups = [units[i:i + ATTN_GROUP] for i in range(0, len(units), ATTN_GROUP)]
    return hb, SWA_BLOCK * dil, groups


class _unit:
    def __init__(self, ref, r, h, dil):
        rows = pl.ds(0, SWA_BLOCK) if dil == 1 else pl.ds(r, SWA_BLOCK, stride=dil)
        self.ref, self.idx = ref, (rows, pl.ds(h * HEAD, HEAD))

    def __getitem__(self, _):
        return self.ref[self.idx]

    def __setitem__(self, _, value):
        self.ref[self.idx] = value


def _attn_call(q, k, v, slopes, dil, name):
    t, hw = q.shape
    hb, rows, groups = _attn_plan(t, hw, dil)
    wide = hb * HEAD
    cur = pl.BlockSpec((rows, wide), lambda j, n: (n, j))
    prev = pl.BlockSpec((rows, wide), lambda j, n: (jnp.maximum(n - 1, 0), j))
    ssp = pl.BlockSpec((1, wide), lambda j, n: (0, j))

    def body(q_ref, kp_ref, kc_ref, vp_ref, vc_ref, s_ref, o_ref, l_ref):
        first = pl.program_id(1) == 0
        for grp in groups:
            args = [[_unit(x, r, h, dil)[...] for x in (q_ref, kp_ref, kc_ref, vp_ref, vc_ref)] for r, h in grp]
            outs, lses = _attn_block(args, [s_ref[:, h * HEAD:(h + 1) * HEAD] for _, h in grp], first)
            for (r, h), o, lse in zip(grp, outs, lses):
                _unit(o_ref, r, h, dil)[...] = o
                _unit(l_ref, r, h, dil)[...] = lse

    return pl.pallas_call(
        body, name=name, grid=(hw // wide, t // rows),
        in_specs=[cur, prev, cur, prev, cur, ssp], out_specs=[cur, cur],
        out_shape=[jax.ShapeDtypeStruct((t, hw), F32), jax.ShapeDtypeStruct((t, hw), F32)],
        compiler_params=_params(("parallel", "parallel")),
    )(q, k, k, v, v, slopes)


def _attn_bwd_call(q, k, v, slopes, d_o, d_l, dil, name):
    t, hw = q.shape
    hb, rows, groups = _attn_plan(t, hw, dil)
    wide = hb * HEAD
    nb = t // rows
    cur = pl.BlockSpec((rows, wide), lambda j, n: (jnp.minimum(n, nb - 1), j))
    prev = pl.BlockSpec((rows, wide), lambda j, n: (jnp.clip(n - 1, 0, nb - 1), j))
    ssp = pl.BlockSpec((1, wide), lambda j, n: (0, j))

    def body(q_ref, kp_ref, kc_ref, vp_ref, vc_ref, s_ref, do_ref, dl_ref, dq_ref, dk_ref, dv_ref, ck, cv):
        n = pl.program_id(1)

        @pl.when(n == 0)
        def _():
            ck[...] = jnp.zeros_like(ck)
            cv[...] = jnp.zeros_like(cv)

        @pl.when(n < nb)
        def _():
            first = n == 0
            for grp in groups:
                args = [[_unit(x, r, h, dil)[...] for x in (q_ref, kp_ref, kc_ref, vp_ref, vc_ref)] for r, h in grp]
                svs = [s_ref[:, h * HEAD:(h + 1) * HEAD] for _, h in grp]
                cots = ([_unit(do_ref, r, h, dil)[...] for r, h in grp], [_unit(dl_ref, r, h, dil)[...] for r, h in grp])
                carry = [(_unit(ck, r, h, dil)[...], _unit(cv, r, h, dil)[...]) for r, h in grp]
                _, vjp = jax.vjp(lambda a, svs=svs: _attn_block(a, svs, first), args)
                grads = vjp(cots)[0]
                for (r, h), (dq, dkp, dkc, dvp, dvc), (c_k, c_v) in zip(grp, grads, carry):
                    _unit(dq_ref, r, h, dil)[...] = dq
                    _unit(dk_ref, r, h, dil)[...] = c_k + dkp
                    _unit(dv_ref, r, h, dil)[...] = c_v + dvp
                    _unit(ck, r, h, dil)[...] = dkc
                    _unit(cv, r, h, dil)[...] = dvc

        @pl.when(n == nb)
        def _():
            dk_ref[...] = ck[...]
            dv_ref[...] = cv[...]

    return pl.pallas_call(
        body, name=name, grid=(hw // wide, nb + 1),
        in_specs=[cur, prev, cur, prev, cur, ssp, cur, cur], out_specs=[cur, prev, prev],
        out_shape=[jax.ShapeDtypeStruct((t, hw), F32)] * 3,
        scratch_shapes=[pltpu.VMEM((rows, wide), F32), pltpu.VMEM((rows, wide), F32)],
        compiler_params=_params(("parallel", "arbitrary")),
    )(q, k, k, v, v, slopes, d_o, d_l)


def _make_attn(slopes, dil, name):
    @jax.custom_vjp
    def op(q, k, v):
        return tuple(_attn_call(q, k, v, slopes, dil, name + "_f"))

    def fwd(q, k, v):
        return tuple(_attn_call(q, k, v, slopes, dil, name + "_f")), (q, k, v)

    def bwd(res, cots):
        q, k, v = res
        return tuple(_attn_bwd_call(q, k, v, slopes, cots[0], cots[1], dil, name + "_b"))

    op.defvjp(fwd, bwd)
    return op


def _dn_pre_fn(g, cq, ck, cv):
    sq, sk, sv = _silu(cq), _silu(ck), _silu(cv)
    qn = sq * lax.rsqrt(jnp.sum(sq * sq, axis=-1, keepdims=True) + 1e-6) * (HEAD ** -0.5)
    kn = sk * lax.rsqrt(jnp.sum(sk * sk, axis=-1, keepdims=True) + 1e-6)
    return qn, kn, sv


def _make_dn_gates_fn(heads):
    def fn(g, ba, a_log, dt_bias):
        lane = lax.broadcasted_iota(jnp.int32, ba.shape, 1)
        lane1 = lax.broadcasted_iota(jnp.int32, a_log.shape, 1)
        betas, gs = [], []
        for h in range(heads):
            b_raw = jnp.sum(jnp.where(lane == h, ba, 0.0), axis=1, keepdims=True)
            a_raw = jnp.sum(jnp.where(lane == heads + h, ba, 0.0), axis=1, keepdims=True)
            al = jnp.sum(jnp.where(lane1 == h, a_log, 0.0), axis=1, keepdims=True)
            dt = jnp.sum(jnp.where(lane1 == h, dt_bias, 0.0), axis=1, keepdims=True)
            beta = _sigmoid(b_raw)
            gl = -jnp.exp(al) * _softplus(a_raw + dt)
            betas.append(jnp.broadcast_to(beta, ba.shape))
            gs.append(jnp.broadcast_to(gl, ba.shape))
        return jnp.concatenate(betas, axis=1), jnp.concatenate(gs, axis=1)
    return fn


def _dn_post_fn(g, o, z, w):
    y = o * lax.rsqrt(jnp.mean(o * o, axis=-1, keepdims=True) + NORM_EPS) * w
    return ((y * _silu(z)).astype(BF16),)


def _lru_pre_fn(g, xc, wa, wx, ba, bx, lam):
    r = _sigmoid(_bdot(xc, wa, _NN) + ba)
    i = _sigmoid(_bdot(xc, wx, _NN) + bx)
    log_a = -LRU_C * r * _softplus(-lam)
    a = jnp.exp(log_a)
    u = jnp.sqrt(1.0 - jnp.exp(2.0 * log_a)) * (i * xc)
    return a, u


def _lru_post_fn(g, hs, yr):
    return ((hs * _gelu(yr)).astype(BF16),)


def _merge_fn(g, o1, o2, o3, l1, l2, l3):
    m = lax.stop_gradient(jnp.maximum(jnp.maximum(l1, l2), l3))
    e1, e2, e3 = jnp.exp(l1 - m), jnp.exp(l2 - m), jnp.exp(l3 - m)
    return (((e1 * o1 + e2 * o2 + e3 * o3) / (e1 + e2 + e3)).astype(BF16),)


def _ret_post_fn(g, o, gate):
    mu = jnp.mean(o, axis=-1, keepdims=True)
    xc = o - mu
    y = xc * lax.rsqrt(jnp.mean(xc * xc, axis=-1, keepdims=True) + GN_EPS)
    return ((y * _silu(gate)).astype(BF16),)


def _pad_lanes(v):
    return jnp.pad(v, (0, LANES - v.shape[0]))[None, :]


def _even_layout(half):
    heads = half // HEAD
    qkv = 3 * half
    segs = [(0, qkv, qkv), (qkv, half, half), (qkv + half, 2 * heads, LANES),
            (qkv + half + 2 * heads, half, half), (qkv + 2 * half + 2 * heads, half, half)]
    return segs


def _pad_ev_w_in(w, half):
    parts = []
    for start, width, padded in _even_layout(half):
        part = w[:, start:start + width]
        if padded != width:
            part = jnp.pad(part, ((0, 0), (0, padded - width)))
        parts.append(part)
    return jnp.concatenate(parts, axis=1)


def _even_mixer(hn, h, lw):
    t, d = h.shape
    half = d // 2
    heads = half // HEAD
    hp = 4 if heads % 4 == 0 else 1
    def assemble(shards):
        return _pad_ev_w_in(shards.transpose(1, 0, 2).reshape(d, -1), half)

    def to_slots(d_padded):
        return jax.vjp(assemble, jnp.zeros_like(lw["w_in"]))[1](d_padded)[0]

    proj = _make_mm("ev_in", F32, to_slots=to_slots)(hn, assemble(lw["w_in"]), lw["w_in_shard"])
    o0 = 0
    segs = []
    for _, _, padded in _even_layout(half):
        segs.append(proj[:, o0:o0 + padded])
        o0 += padded
    qkv, z, ba, xr, yr = segs
    c = _make_conv("dn_conv")(qkv, lw["dn_conv_w"], jnp.zeros((1, 3 * half), F32))
    q, k, v = _make_rowmap(_dn_pre_fn, heads, 3, [], [(half, F32)] * 3, "dn_pre")(
        c[:, :half], c[:, half:2 * half], c[:, 2 * half:])
    beta_b, g_b = _make_rowmap(_make_dn_gates_fn(heads), 1, 1, [False, False], [(half, F32)] * 2, "dn_gates")(
        ba, _pad_lanes(lw["dn_a_log"]), _pad_lanes(lw["dn_dt_bias"]))
    o = _make_scan(_dn_chunk, jnp.zeros((1, heads * LANES), F32), heads, hp, DN_CHUNK, (HEAD, HEAD), HEAD, "dn_core")(
        q, k, v, g_b, beta_b)
    ya = _make_rowmap(_dn_post_fn, heads, 2, [False], [(half, BF16)], "dn_post")(o, z, lw["dn_norm_w"][None, :])[0]
    nblk = lw["lru_wa"].shape[0]
    xc = _make_conv("lru_conv")(xr, lw["lru_conv_w"], lw["lru_conv_b"][None, :])
    wa = lw["lru_wa"].transpose(1, 0, 2).reshape(HEAD, nblk * HEAD)
    wx = lw["lru_wx"].transpose(1, 0, 2).reshape(HEAD, nblk * HEAD)
    a, u = _make_rowmap(_lru_pre_fn, nblk, 1, [True] * 5, [(half, F32)] * 2, "lru_pre")(
        xc, wa, wx, lw["lru_ba"][None, :], lw["lru_bx"][None, :], lw["lru_lambda"][None, :])
    hs = _make_lru("lru_scan")(a.reshape(t, nblk, HEAD), u.reshape(t, nblk, HEAD)).reshape(t, half)
    yb = _make_rowmap(_lru_post_fn, nblk, 2, [], [(half, BF16)], "lru_post")(hs, yr)[0]
    return _make_mm_res("ev_out")(jnp.concatenate([ya, yb], axis=1), lw["w_out"], lw["w_out_shard"], h)


def _odd_mixer(hn, h, lw):
    t, d = h.shape
    half = d // 2
    heads = half // HEAD
    rheads = half // RET_DV
    rqk = rheads * RET_DK
    proj = _make_mm("od_in", F32, slots=True)(hn, lw["w_in"], lw["w_in_shard"])
    cq, ck, cv = proj[:, :half], proj[:, half:2 * half], proj[:, 2 * half:3 * half]
    o1 = 3 * half
    rq, rk = proj[:, o1:o1 + rqk], proj[:, o1 + rqk:o1 + 2 * rqk]
    rv, rg = proj[:, o1 + 2 * rqk:o1 + 2 * rqk + half], proj[:, o1 + 2 * rqk + half:]
    slopes = np.exp2(-8.0 * np.arange(1, heads + 1, dtype=np.float64) / heads)
    outs, lses = [], []
    for window, dil in SWA_BRANCHES:
        assert window // dil == SWA_BLOCK and (t // dil) % SWA_BLOCK == 0
        sl = jnp.asarray(np.repeat(slopes * dil, HEAD)[None, :], F32)
        o_i, l_i = _make_attn(sl, dil, "swa_d%d" % dil)(cq, ck, cv)
        outs.append(o_i)
        lses.append(l_i)
    yc = _make_rowmap(_merge_fn, heads, 6, [], [(half, BF16)], "swa_merge")(*outs, *lses)[0]
    lg = np.log1p(-np.exp2(-5.0 - np.arange(rheads, dtype=np.float64)))
    lg_b = jnp.asarray(np.repeat(lg, LANES)[None, :], F32)
    hp = 4 if rheads % 4 == 0 else 1
    o_r = _make_scan(_ret_chunk, lg_b, rheads, hp, RET_CHUNK, (RET_DK, RET_DV), RET_DV, "ret_core")(rq, rk, rv)
    yd = _make_rowmap(_ret_post_fn, rheads, 2, [], [(half, BF16)], "ret_post")(o_r, rg)[0]
    return _make_mm_res("od_out")(jnp.concatenate([yc, yd], axis=1), lw["w_out"], lw["w_out_shard"], h)


def _local_loss(shards, small, x, big, p, target):
    ffn = shards
    depth = ffn["w_up"].shape[0]
    h = x
    for i in range(depth):
        j = i // 2
        h, hn = _make_norm("ln_mix")(h, small["ln_mix_w"][i][None, :])
        if i % 2 == 0:
            lw = {"w_in": big["ev_w_in"][j], "w_out": big["ev_w_out"][j],
                  "w_in_shard": shards["ev_w_in"][j], "w_out_shard": shards["ev_w_out"][j]}
            for nm in ("dn_conv_w", "dn_a_log", "dn_dt_bias", "dn_norm_w", "lru_conv_w", "lru_conv_b", "lru_wa",
                       "lru_ba", "lru_wx", "lru_bx", "lru_lambda"):
                lw[nm] = small[nm][j]
            h = _even_mixer(hn, h, lw)
        else:
            h = _odd_mixer(hn, h, {"w_in": big["od_w_in"][j], "w_out": big["od_w_out"][j],
                                   "w_in_shard": shards["od_w_in"][j], "w_out_shard": shards["od_w_out"][j]})
        h, hn = _make_norm("ln_mlp")(h, small["ln_mlp_w"][i][None, :])
        h = _make_ffn("ffn")(hn, ffn["w_up"][i], ffn["w_down"][i], h)
        h, hn = _make_norm("ln_ple")(h, small["ln_ple_w"][i][None, :])
        h = _make_ple("ple")(hn, big["w_ple_gate"][i], shards["w_ple_gate"][i], p[i], big["w_ple_proj"][i],
                             shards["w_ple_proj"][i], h)
    return _make_loss("loss_head")(h, small["ln_final_w"][None, :], target)


def _all_gather(x, name):
    r, c = x.shape

    def body(x_ref, out_ref, send_sems, recv_sems, local_sem):
        mx, my, mc = lax.axis_index("x"), lax.axis_index("y"), lax.axis_index("c")
        me, sibling = (mx, my, mc), (mx, my, 1 - mc)
        chips = [(1 - mx, my), (mx, 1 - my), (1 - mx, 1 - my)]

        def slot(px, py, pc):
            return out_ref.at[4 * px + 2 * py + pc]

        def copy(k, block, to, src=None):
            return pltpu.make_async_remote_copy(
                src_ref=slot(*block) if src is None else src, dst_ref=slot(*block),
                send_sem=send_sems.at[k], recv_sem=recv_sems.at[k], device_id=to, device_id_type=MESH)

        mine = pltpu.make_async_copy(x_ref, slot(*me), local_sem)
        mine.start()
        first = [copy(0, me, sibling, src=x_ref)]
        first += [copy(1 + j, me, (*chip, mc), src=x_ref) for j, chip in enumerate(chips)]
        for cp in first:
            cp.start()
        passed = [copy(4 + j, (*chip, mc), sibling) for j, chip in enumerate(chips)]
        for j, chip in enumerate(chips):
            copy(1 + j, (*chip, mc), me).wait_recv()
            passed[j].start()
        copy(0, sibling, me).wait_recv()
        for j, chip in enumerate(chips):
            copy(4 + j, (*chip, 1 - mc), me).wait_recv()
        for cp in first + passed:
            cp.wait_send()
        mine.wait()

    return pl.pallas_call(
        body, name=name,
        out_shape=jax.ShapeDtypeStruct((N_DEV, r, c), x.dtype),
        in_specs=[pl.BlockSpec(memory_space=pl.ANY)],
        out_specs=pl.BlockSpec(memory_space=pl.ANY),
        scratch_shapes=[pltpu.SemaphoreType.DMA((7,)), pltpu.SemaphoreType.DMA((7,)), pltpu.SemaphoreType.DMA],
    )(x)


def _gather_layers(x, name):
    n, r, c = x.shape

    def body(x_ref, *rest):
        outs, (send_sems, recv_sems, local_sems) = rest[:n], rest[n:]
        mx, my, mc = lax.axis_index("x"), lax.axis_index("y"), lax.axis_index("c")
        me, sibling = (mx, my, mc), (mx, my, 1 - mc)
        chips = [(1 - mx, my), (mx, 1 - my), (1 - mx, 1 - my)]

        def slot(l, px, py, pc):
            return outs[l].at[4 * px + 2 * py + pc]

        def copy(k, l, block, to, from_shard=False):
            return pltpu.make_async_remote_copy(
                src_ref=x_ref.at[l] if from_shard else slot(l, *block), dst_ref=slot(l, *block),
                send_sem=send_sems.at[k, l], recv_sem=recv_sems.at[k, l], device_id=to, device_id_type=MESH)

        mine = [pltpu.make_async_copy(x_ref.at[l], slot(l, *me), local_sems.at[l]) for l in range(n)]
        for cp in mine:
            cp.start()
        sent = [copy(0, l, me, sibling, True) for l in range(n)]
        sent += [copy(1 + j, l, me, (*chip, mc), True) for j, chip in enumerate(chips) for l in range(n)]
        for cp in sent:
            cp.start()
        for j, chip in enumerate(chips):
            for l in range(n):
                copy(1 + j, l, (*chip, mc), me).wait_recv()
                passed = copy(4 + j, l, (*chip, mc), sibling)
                passed.start()
                sent.append(passed)
        for l in range(n):
            copy(0, l, sibling, me).wait_recv()
        for j, chip in enumerate(chips):
            for l in range(n):
                copy(4 + j, l, (*chip, 1 - mc), me).wait_recv()
        for cp in sent:
            cp.wait_send()
        for cp in mine:
            cp.wait()

    return pl.pallas_call(
        body, name=name,
        out_shape=[jax.ShapeDtypeStruct((N_DEV, r, c), x.dtype)] * n,
        in_specs=[pl.BlockSpec(memory_space=pl.ANY)],
        out_specs=[pl.BlockSpec(memory_space=pl.ANY)] * n,
        scratch_shapes=[pltpu.SemaphoreType.DMA((7, n)), pltpu.SemaphoreType.DMA((7, n)),
                        pltpu.SemaphoreType.DMA((n,))],
    )(x)


def _swap_pairs(gs, name):
    n = len(gs)
    _, r, c = gs[0].shape

    def body(*refs):
        g_refs, out_ref, send_sems, recv_sems = refs[:n], refs[n], refs[n + 1], refs[n + 2]
        mx, my, mc = lax.axis_index("x"), lax.axis_index("y"), lax.axis_index("c")

        def copy(q, l):
            return pltpu.make_async_remote_copy(
                src_ref=g_refs[l].at[2 * q + (1 - mc)], dst_ref=out_ref.at[q, l],
                send_sem=send_sems.at[q, l], recv_sem=recv_sems.at[q, l],
                device_id=(mx, my, 1 - mc), device_id_type=MESH)

        copies = [copy(q, l) for q in range(4) for l in range(n)]
        for cp in copies:
            cp.start()
        for cp in copies:
            cp.wait_recv()
        for cp in copies:
            cp.wait_send()

    return pl.pallas_call(
        body, name=name,
        out_shape=jax.ShapeDtypeStruct((4, n, r, c), gs[0].dtype),
        in_specs=[pl.BlockSpec(memory_space=pl.ANY)] * n,
        out_specs=pl.BlockSpec(memory_space=pl.ANY),
        scratch_shapes=[pltpu.SemaphoreType.DMA((4, n)), pltpu.SemaphoreType.DMA((4, n))],
    )(*gs)


def _pair_sum(g, recv, layer, side, name):
    _, r, c = g.shape
    tr = _tile(r, max(SUBLANES, (256 * 1024) // c))

    def body(side_ref, g_ref, r_ref, o_ref):
        del side_ref
        o_ref[...] = (g_ref[...].astype(F32) + r_ref[...].astype(F32)).astype(o_ref.dtype)

    return pl.pallas_call(
        body, name=name,
        grid_spec=pltpu.PrefetchScalarGridSpec(
            num_scalar_prefetch=1, grid=(4, r // tr),
            in_specs=[pl.BlockSpec((None, None, tr, c), lambda q, t, side_ref: (q, side_ref[0], t, 0)),
                      pl.BlockSpec((None, None, tr, c), lambda q, t, side_ref: (q, layer, t, 0))],
            out_specs=pl.BlockSpec((None, tr, c), lambda q, t, side_ref: (q, t, 0))),
        out_shape=jax.ShapeDtypeStruct((4, r, c), g.dtype),
        compiler_params=_params(("parallel", "parallel")),
    )(side, g.reshape(4, 2, r, c), recv)


def _pair_sums(g, name):
    side = lax.axis_index("c").astype(jnp.int32).reshape(1)
    return _pair_sum(g, _swap_pairs([g], name + "_swap"), 0, side, name + "_pairsum")


class _GatherRider:
    def __init__(self, x):
        self.inputs = [x]
        self.out_shapes = [jax.ShapeDtypeStruct((N_DEV,) + x.shape, x.dtype)]
        self.scratch = [pltpu.SemaphoreType.DMA((7,)), pltpu.SemaphoreType.DMA((7,)), pltpu.SemaphoreType.DMA]

    @staticmethod
    def _plan(ins, outs, sems):
        x_ref, out_ref = ins[0], outs[0]
        send_sems, recv_sems, local_sem = sems
        mx, my, mc = lax.axis_index("x"), lax.axis_index("y"), lax.axis_index("c")
        me, sibling = (mx, my, mc), (mx, my, 1 - mc)
        chips = [(1 - mx, my), (mx, 1 - my), (1 - mx, 1 - my)]

        def slot(px, py, pc):
            return out_ref.at[4 * px + 2 * py + pc]

        def copy(k, block, to, from_shard=False):
            return pltpu.make_async_remote_copy(
                src_ref=x_ref if from_shard else slot(*block), dst_ref=slot(*block),
                send_sem=send_sems.at[k], recv_sem=recv_sems.at[k], device_id=to, device_id_type=MESH)

        mine = pltpu.make_async_copy(x_ref, slot(*me), local_sem)
        first = [copy(0, me, sibling, True)] + [copy(1 + j, me, (*chip, mc), True) for j, chip in enumerate(chips)]
        return me, sibling, chips, mc, copy, mine, first

    def start(self, ins, outs, sems):
        _, _, _, _, _, mine, first = self._plan(ins, outs, sems)
        mine.start()
        for cp in first:
            cp.start()

    def finish(self, ins, outs, sems):
        me, sibling, chips, mc, copy, mine, first = self._plan(ins, outs, sems)
        passed = []
        for j, chip in enumerate(chips):
            copy(1 + j, (*chip, mc), me).wait_recv()
            passed.append(copy(4 + j, (*chip, mc), sibling))
            passed[-1].start()
        copy(0, sibling, me).wait_recv()
        for j, chip in enumerate(chips):
            copy(4 + j, (*chip, 1 - mc), me).wait_recv()
        for cp in first + passed:
            cp.wait_send()
        mine.wait()


class _DeliverRider:
    def __init__(self, ps):
        self.inputs = [ps]
        self.out_shapes = [jax.ShapeDtypeStruct(ps.shape, ps.dtype)]
        self.scratch = [pltpu.SemaphoreType.DMA((3,)), pltpu.SemaphoreType.DMA((3,)), pltpu.SemaphoreType.DMA]

    @staticmethod
    def _plan(ins, outs, sems):
        p_ref, out_ref = ins[0], outs[0]
        send_sems, recv_sems, local_sem = sems
        mx, my, mc = lax.axis_index("x"), lax.axis_index("y"), lax.axis_index("c")
        q_me = 2 * mx + my
        mine = pltpu.make_async_copy(p_ref.at[q_me], out_ref.at[q_me], local_sem)
        sent, expected = [], []
        for k in range(1, 4):
            fx, fy = (k >> 1) & 1, k & 1
            px = mx + fx - 2 * mx * fx
            py = my + fy - 2 * my * fy
            q_peer = 2 * px + py
            for dst, into in ((q_me, sent), (q_peer, expected)):
                into.append(pltpu.make_async_remote_copy(
                    src_ref=p_ref.at[q_peer], dst_ref=out_ref.at[dst], send_sem=send_sems.at[k - 1],
                    recv_sem=recv_sems.at[k - 1], device_id=(px, py, mc), device_id_type=MESH))
        return mine, sent, expected

    def start(self, ins, outs, sems):
        mine, sent, _ = self._plan(ins, outs, sems)
        mine.start()
        for cp in sent:
            cp.start()

    def finish(self, ins, outs, sems):
        mine, sent, expected = self._plan(ins, outs, sems)
        for cp in expected:
            cp.wait_recv()
        for cp in sent:
            cp.wait_send()
        mine.wait()


def _slot_sum(slots, name):
    ns, r, c = slots.shape
    tr = _tile(r, 256)

    def body(s_ref, o_ref):
        acc = s_ref[0].astype(F32)
        for s in range(1, ns):
            acc = acc + s_ref[s].astype(F32)
        o_ref[...] = acc

    return pl.pallas_call(
        body, name=name, grid=(r // tr,),
        in_specs=[pl.BlockSpec((ns, tr, c), lambda i: (0, i, 0))],
        out_specs=pl.BlockSpec((tr, c), lambda i: (i, 0)),
        out_shape=jax.ShapeDtypeStruct((r, c), F32),
        compiler_params=_params(("parallel",)),
    )(slots)


def _adamw(slots, w, m, v, name):
    ns, r, c = slots.shape
    tr = _tile(r, max(SUBLANES, (128 * 1024) // c))

    def body(s_ref, w_ref, m_ref, v_ref, g_out, d_out, m_out, v_out):
        g = s_ref[0].astype(F32)
        for s in range(1, ns):
            g = g + s_ref[s].astype(F32)
        mn = ADAM_B1 * m_ref[...] + (1.0 - ADAM_B1) * g
        vn = ADAM_B2 * v_ref[...] + (1.0 - ADAM_B2) * (g * g)
        m_hat = mn / (1.0 - ADAM_B1 ** ADAM_STEP)
        v_hat = vn / (1.0 - ADAM_B2 ** ADAM_STEP)
        g_out[...] = g
        d_out[...] = -ADAM_LR * (m_hat / (jnp.sqrt(v_hat) + ADAM_EPS) + ADAM_WD * w_ref[...])
        m_out[...] = mn
        v_out[...] = vn

    blk = pl.BlockSpec((tr, c), lambda i: (i, 0))
    return pl.pallas_call(
        body, name=name, grid=(r // tr,),
        in_specs=[pl.BlockSpec((ns, tr, c), lambda i: (0, i, 0)), blk, blk, blk],
        out_specs=[blk] * 4,
        out_shape=[jax.ShapeDtypeStruct((r, c), F32)] * 4,
        compiler_params=_params(("parallel",)),
    )(slots, w, m, v)


def _pack(arrays, dtype, row_multiple=SUBLANES):
    flat = jnp.concatenate([a.astype(dtype).reshape(-1) for a in arrays])
    unit = row_multiple * PACK_COLS
    total = -(-flat.shape[0] // unit) * unit
    if total != flat.shape[0]:
        flat = jnp.pad(flat, (0, total - flat.shape[0]))
    return flat.reshape(-1, PACK_COLS)


def _unpack(buf, shapes):
    flat = buf.reshape(-1)
    out, o = [], 0
    for s in shapes:
        n = int(np.prod(s))
        out.append(flat[o:o + n].reshape(s))
        o += n
    return out


FFN = ("w_up", "w_down")
BIG = ("w_ple_proj", "w_ple_gate", "ev_w_in", "ev_w_out", "od_w_in", "od_w_out")
BIG_COL_SHARDED = {"w_ple_proj": True, "w_ple_gate": False, "ev_w_in": True, "ev_w_out": False,
                   "od_w_in": True, "od_w_out": False}
SMALL_SHARDED = ("dn_conv_w", "lru_conv_w")
SMALL_REPLICATED = ("ln_mix_w", "ln_mlp_w", "ln_ple_w", "ln_final_w", "dn_a_log", "dn_dt_bias", "dn_norm_w",
                    "lru_conv_b", "lru_wa", "lru_ba", "lru_wx", "lru_bx", "lru_lambda")
WEIGHTS = ("ln_mix_w", "ln_mlp_w", "ln_ple_w", "w_up", "w_down", "w_ple_proj", "w_ple_gate", "ln_final_w",
           "ev_w_in", "ev_w_out", "dn_conv_w", "dn_a_log", "dn_dt_bias", "dn_norm_w", "lru_conv_w", "lru_conv_b",
           "lru_wa", "lru_ba", "lru_wx", "lru_bx", "lru_lambda", "od_w_in", "od_w_out")


def kernel(x, p, ln_mix_w, ln_mlp_w, ln_ple_w, w_up, w_down, w_ple_proj, w_ple_gate, ln_final_w, ev_w_in, ev_w_out, dn_conv_w, dn_a_log, dn_dt_bias, dn_norm_w, lru_conv_w, lru_conv_b, lru_wa, lru_ba, lru_wx, lru_bx, lru_lambda, od_w_in, od_w_out, loss_target, m_ln_mix_w, m_ln_mlp_w, m_ln_ple_w, m_w_up, m_w_down, m_w_ple_proj, m_w_ple_gate, m_ln_final_w, m_ev_w_in, m_ev_w_out, m_dn_conv_w, m_dn_a_log, m_dn_dt_bias, m_dn_norm_w, m_lru_conv_w, m_lru_conv_b, m_lru_wa, m_lru_ba, m_lru_wx, m_lru_bx, m_lru_lambda, m_od_w_in, m_od_w_out, v_ln_mix_w, v_ln_mlp_w, v_ln_ple_w, v_w_up, v_w_down, v_w_ple_proj, v_w_ple_gate, v_ln_final_w, v_ev_w_in, v_ev_w_out, v_dn_conv_w, v_dn_a_log, v_dn_dt_bias, v_dn_norm_w, v_lru_conv_w, v_lru_conv_b, v_lru_wa, v_lru_ba, v_lru_wx, v_lru_bx, v_lru_lambda, v_od_w_in, v_od_w_out):
    w = dict(ln_mix_w=ln_mix_w, ln_mlp_w=ln_mlp_w, ln_ple_w=ln_ple_w, w_up=w_up, w_down=w_down,
             w_ple_proj=w_ple_proj, w_ple_gate=w_ple_gate, ln_final_w=ln_final_w, ev_w_in=ev_w_in,
             ev_w_out=ev_w_out, dn_conv_w=dn_conv_w, dn_a_log=dn_a_log, dn_dt_bias=dn_dt_bias,
             dn_norm_w=dn_norm_w, lru_conv_w=lru_conv_w, lru_conv_b=lru_conv_b, lru_wa=lru_wa, lru_ba=lru_ba,
             lru_wx=lru_wx, lru_bx=lru_bx, lru_lambda=lru_lambda, od_w_in=od_w_in, od_w_out=od_w_out)
    m = dict(ln_mix_w=m_ln_mix_w, ln_mlp_w=m_ln_mlp_w, ln_ple_w=m_ln_ple_w, w_up=m_w_up, w_down=m_w_down,
             w_ple_proj=m_w_ple_proj, w_ple_gate=m_w_ple_gate, ln_final_w=m_ln_final_w, ev_w_in=m_ev_w_in,
             ev_w_out=m_ev_w_out, dn_conv_w=m_dn_conv_w, dn_a_log=m_dn_a_log, dn_dt_bias=m_dn_dt_bias,
             dn_norm_w=m_dn_norm_w, lru_conv_w=m_lru_conv_w, lru_conv_b=m_lru_conv_b, lru_wa=m_lru_wa,
             lru_ba=m_lru_ba, lru_wx=m_lru_wx, lru_bx=m_lru_bx, lru_lambda=m_lru_lambda, od_w_in=m_od_w_in,
             od_w_out=m_od_w_out)
    v = dict(ln_mix_w=v_ln_mix_w, ln_mlp_w=v_ln_mlp_w, ln_ple_w=v_ln_ple_w, w_up=v_w_up, w_down=v_w_down,
             w_ple_proj=v_w_ple_proj, w_ple_gate=v_w_ple_gate, ln_final_w=v_ln_final_w, ev_w_in=v_ev_w_in,
             ev_w_out=v_ev_w_out, dn_conv_w=v_dn_conv_w, dn_a_log=v_dn_a_log, dn_dt_bias=v_dn_dt_bias,
             dn_norm_w=v_dn_norm_w, lru_conv_w=v_lru_conv_w, lru_conv_b=v_lru_conv_b, lru_wa=v_lru_wa,
             lru_ba=v_lru_ba, lru_wx=v_lru_wx, lru_bx=v_lru_bx, lru_lambda=v_lru_lambda, od_w_in=v_od_w_in,
             od_w_out=v_od_w_out)
    me = 4 * lax.axis_index("x") + 2 * lax.axis_index("y") + lax.axis_index("c")

    big = {}
    for n in BIG:
        shards = _gather_layers(w[n].astype(BF16), "gather_" + n)
        big[n] = [s if BIG_COL_SHARDED[n] else s.reshape(-1, s.shape[2]) for s in shards]
    conv_shapes = [w[n].shape for n in SMALL_SHARDED]
    conv_g = _all_gather(_pack([w[n] for n in SMALL_SHARDED], F32), "gather_conv")
    conv_dev = [_unpack(conv_g[s], conv_shapes) for s in range(N_DEV)]
    small = {n: w[n] for n in SMALL_REPLICATED}
    for i, n in enumerate(SMALL_SHARDED):
        small[n] = jnp.concatenate([conv_dev[s][i] for s in range(N_DEV)], axis=-1)

    loss_local, (g_shards, g_small, g_x) = jax.value_and_grad(_local_loss, argnums=(0, 1, 2))(
        {n: w[n] for n in FFN + BIG}, small, x[0], big, p[:, 0], loss_target[0])
    loss = lax.psum(loss_local, ("x", "y", "c"))

    out = {}
    for n in FFN + BIG:
        nl, r, c = w[n].shape
        res = _adamw(g_shards[n].reshape(1, nl * r, c), w[n].reshape(nl * r, c), m[n].reshape(nl * r, c),
                     v[n].reshape(nl * r, c), "adamw_" + n)
        for kind, buf in zip(("grad", "delta", "new_m", "new_v"), res):
            out[kind, n] = buf.reshape(nl, r, c)

    small_names = SMALL_REPLICATED + SMALL_SHARDED
    small_shapes = [small[n].shape for n in small_names]
    all_small = _all_gather(_pack([g_small[n] for n in small_names], F32), "gather_small_grads")
    total = dict(zip(small_names, _unpack(_slot_sum(all_small, "sum_small_grads"), small_shapes)))
    for n in SMALL_SHARDED:
        width = w[n].shape[-1]
        total[n] = lax.dynamic_slice_in_dim(total[n], me * width, width, axis=-1)
    own_shapes = [w[n].shape for n in small_names]
    res_small = _adamw(_pack([total[n] for n in small_names], F32)[None], _pack([w[n] for n in small_names], F32),
                       _pack([m[n] for n in small_names], F32), _pack([v[n] for n in small_names], F32), "adamw_small")
    for kind, buf in zip(("grad", "delta", "new_m", "new_v"), res_small):
        for n, a in zip(small_names, _unpack(buf, own_shapes)):
            out[kind, n] = a

    return (loss, g_x[None], *[out["grad", n] for n in WEIGHTS], *[out["delta", n] for n in WEIGHTS],
            *[out["new_m", n] for n in WEIGHTS], *[out["new_v", n] for n in WEIGHTS])
```

```python
import functools

import numpy as np
import jax
import jax.numpy as jnp
from jax import lax
from jax.experimental import pallas as pl
from jax.experimental.pallas import tpu as pltpu

F32 = jnp.float32
BF16 = jnp.bfloat16
N_DEV = 8
LANES = 128
SUBLANES = 8
VMEM_LIMIT = 56 * 1024 * 1024
PACK_COLS = 1024
NORM_EPS = 1e-6
GN_EPS = 1e-5
DN_CHUNK = 64
RET_CHUNK = 64
HEAD = 128
RET_DK = 128
RET_DV = 256
SWA_BLOCK = 128
SWA_BRANCHES = ((128, 1), (512, 4), (2048, 16))
LRU_C = 8.0
CONV_W = 4
ADAM_LR, ADAM_B1, ADAM_B2, ADAM_EPS, ADAM_WD, ADAM_STEP = 0.001, 0.9, 0.999, 1e-08, 0.01, 10
NEG = -1e30
MESH = pl.DeviceIdType.MESH


def _params(sem):
    return pltpu.CompilerParams(dimension_semantics=sem, vmem_limit_bytes=VMEM_LIMIT)


def _tile(n, cap):
    for t in (2048, 1024, 896, 768, 640, 512, 384, 256, 128, 64, 32, 16, 8):
        if t <= cap and n % t == 0:
            return t
    return n


def _bdot(a, b, dims):
    return lax.dot_general(a.astype(BF16), b.astype(BF16), (dims, ((), ())), preferred_element_type=F32)


_NN = ((1,), (0,))
_NT = ((1,), (1,))
_TN = ((0,), (0,))


def _hdot(a, b):
    return lax.dot_general(a, b, (_NN, ((), ())), precision=lax.Precision.HIGH, preferred_element_type=F32)


def _sigmoid(x):
    return jax.nn.sigmoid(x)


def _silu(x):
    return x * _sigmoid(x)


def _softplus(x):
    return jnp.maximum(x, 0.0) + jnp.log(1.0 + jnp.exp(-jnp.abs(x)))


def _gelu(x):
    return 0.5 * x * (1.0 + jnp.tanh(0.7978845608028654 * (x + 0.044715 * (x * x * x))))


def _mm_call(a, b, *, ta=False, tb=False, extras=(), epilogue=None, out_dtypes=(F32,), b_slots=False,
             out_slots=False, rider=None, name):
    m, k = (a.shape[1], a.shape[0]) if ta else a.shape
    ne, no = len(extras), len(out_dtypes)
    cap_n = 512 if ne + no > 2 else 1024
    shard = b.shape[2] if b_slots else None
    if b_slots:
        n = b.shape[1] if tb else N_DEV * shard
    else:
        n = b.shape[0] if tb else b.shape[1]
    if out_slots:
        shard = n // N_DEV
    tm = _tile(m, 1024)
    tn = _tile(shard if (out_slots or (b_slots and not tb)) else n, cap_n)
    cap_k = 2048 if (a.dtype == BF16 and b.dtype == BF16) else 1024
    tk = _tile(shard if (b_slots and tb) else k, cap_k)
    nk = k // tk
    dims = ((0,) if ta else (1,), (1,) if tb else (0,))

    gm, gn = m // tm, n // tn
    nri = 0 if rider is None else len(rider.inputs)
    nro = 0 if rider is None else len(rider.out_shapes)

    def body(*refs):
        a_ref, b_ref = refs[0], refs[1]
        ex = refs[2:2 + ne]
        r_in = refs[2 + ne:2 + ne + nri]
        outs = refs[2 + ne + nri:2 + ne + nri + no]
        r_out = refs[2 + ne + nri + no:2 + ne + nri + no + nro]
        acc = refs[2 + ne + nri + no + nro]
        r_sems = refs[3 + ne + nri + no + nro:]
        kk = pl.program_id(2)
        if rider is not None:
            at_i, at_j = pl.program_id(0), pl.program_id(1)

            @pl.when(jnp.logical_and(jnp.logical_and(at_i == 0, at_j == 0), kk == 0))
            def _():
                rider.start(r_in, r_out, r_sems)

        def finish(total):
            res = (total,) if epilogue is None else epilogue(total, *[e[...] for e in ex])
            for o, r in zip(outs, res):
                o[...] = r.astype(o.dtype)

        if nk == 1:
            finish(_bdot(a_ref[...], b_ref[...], dims))
        else:
            @pl.when(kk == 0)
            def _():
                acc[...] = _bdot(a_ref[...], b_ref[...], dims)

            @pl.when(kk > 0)
            def _():
                acc[...] += _bdot(a_ref[...], b_ref[...], dims)

            @pl.when(kk == nk - 1)
            def _():
                finish(acc[...])

        if rider is not None:
            @pl.when(jnp.logical_and(jnp.logical_and(at_i == gm - 1, at_j == gn - 1), kk == nk - 1))
            def _():
                rider.finish(r_in, r_out, r_sems)

    a_spec = pl.BlockSpec((tk, tm), lambda i, j, kk: (kk, i)) if ta else pl.BlockSpec((tm, tk), lambda i, j, kk: (i, kk))
    if b_slots and tb:
        per = shard // tk
        b_spec = pl.BlockSpec((None, tn, tk), lambda i, j, kk: (kk // per, j, kk % per))
    elif b_slots:
        per = shard // tn
        b_spec = pl.BlockSpec((None, tk, tn), lambda i, j, kk: (j // per, kk, j % per))
    elif tb:
        b_spec = pl.BlockSpec((tn, tk), lambda i, j, kk: (j, kk))
    else:
        b_spec = pl.BlockSpec((tk, tn), lambda i, j, kk: (kk, j))
    mn_spec = pl.BlockSpec((tm, tn), lambda i, j, kk: (i, j))
    if out_slots:
        per_o = shard // tn
        out_specs = [pl.BlockSpec((None, tm, tn), lambda i, j, kk: (j // per_o, i, j % per_o))]
        out_shape = [jax.ShapeDtypeStruct((N_DEV, m, shard), out_dtypes[0])]
    else:
        out_specs = [mn_spec] * no
        out_shape = [jax.ShapeDtypeStruct((m, n), d) for d in out_dtypes]
    hbm = pl.BlockSpec(memory_space=pl.ANY)
    r_inputs = [] if rider is None else list(rider.inputs)
    return pl.pallas_call(
        body, name=name, grid=(gm, gn, nk),
        in_specs=[a_spec, b_spec] + [mn_spec] * ne + [hbm] * nri,
        out_specs=out_specs + [hbm] * nro,
        out_shape=out_shape + ([] if rider is None else list(rider.out_shapes)),
        scratch_shapes=[pltpu.VMEM((tm, tn), F32)] + ([] if rider is None else list(rider.scratch)),
        compiler_params=_params(("arbitrary", "arbitrary", "arbitrary") if rider is not None
                                else ("parallel", "parallel", "arbitrary")),
    )(a, b, *extras, *r_inputs)


def _reduced(got, name):
    return _slot_sum(got, name + "_sum")


def _make_mm(name, out_dtype, slots=False, to_slots=None):
    def call(a, w, passenger):
        return tuple(_mm_call(a, w, out_dtypes=(out_dtype,), b_slots=slots, rider=_GatherRider(passenger),
                              name=name + "_f"))

    @jax.custom_vjp
    def op(a, w, shard, passenger):
        return call(a, w, passenger)

    def fwd(a, w, shard, passenger):
        return call(a, w, passenger), (a, w, passenger)

    def bwd(res, cots):
        a, w, passenger = res
        dy = cots[0].astype(BF16)
        dw = _mm_call(a, dy, ta=True, out_dtypes=(BF16,), out_slots=slots, name=name + "_dw")[0]
        sums = _pair_sums(dw if to_slots is None else to_slots(dw), name + "_w")
        da, got = _mm_call(dy, w, tb=True, out_dtypes=(a.dtype,), b_slots=slots, rider=_DeliverRider(sums),
                           name=name + "_da")
        return da, jnp.zeros_like(w), _reduced(got, name + "_w"), jnp.zeros_like(passenger)

    op.defvjp(fwd, bwd)
    return op


def _make_mm_res(name):
    def call(a, w, h):
        return _mm_call(a, w, extras=(h,), epilogue=lambda acc, hv: (hv + acc,), out_dtypes=(F32,), name=name + "_f")[0]

    @jax.custom_vjp
    def op(a, w, shard, h):
        return call(a, w, h)

    def fwd(a, w, shard, h):
        return call(a, w, h), (a, w)

    def bwd(res, dy):
        a, w = res
        dyb = dy.astype(BF16)
        dw = _mm_call(a, dyb, ta=True, out_dtypes=(BF16,), name=name + "_dw")[0]
        sums = _pair_sums(dw.reshape(N_DEV, -1, dw.shape[1]), name + "_w")
        da, got = _mm_call(dyb, w, tb=True, out_dtypes=(a.dtype,), rider=_DeliverRider(sums), name=name + "_da")
        return da, jnp.zeros_like(w), _reduced(got, name + "_w"), dy

    op.defvjp(fwd, bwd)
    return op


def _make_ffn(name):
    def forward(hn, wu, w_up, w_down, h):
        def ep(acc):
            r = jnp.maximum(acc, 0.0)
            return acc, r * r
        u, act, wd = _mm_call(hn, wu, epilogue=ep, out_dtypes=(BF16, BF16), b_slots=True,
                              rider=_GatherRider(w_down.astype(BF16)), name=name + "_up")
        wd = wd.reshape(-1, wd.shape[2])
        out = _mm_call(act, wd, extras=(h,), epilogue=lambda acc, hv: (hv + acc,), out_dtypes=(F32,), name=name + "_down")[0]
        return out, (hn, wu, wd, u, act)

    @jax.custom_vjp
    def op(hn, wu, w_up, w_down, h):
        return forward(hn, wu, w_up, w_down, h)[0]

    def bwd(res, dy):
        hn, wu, wd, u, act = res
        dyb = dy.astype(BF16)
        d_wdown = _mm_call(act, dyb, ta=True, out_dtypes=(BF16,), name=name + "_dwdown")[0]
        ps_down = _pair_sums(d_wdown.reshape(N_DEV, -1, d_wdown.shape[1]), name + "_wdown")
        d_u, got_down = _mm_call(dyb, wd, tb=True, extras=(u,),
                                 epilogue=lambda acc, uv: (acc * (2.0 * jnp.maximum(uv.astype(F32), 0.0)),),
                                 out_dtypes=(BF16,), rider=_DeliverRider(ps_down), name=name + "_du")
        d_wup = _mm_call(hn, d_u, ta=True, out_dtypes=(BF16,), out_slots=True, name=name + "_dwup")[0]
        ps_up = _pair_sums(d_wup, name + "_wup")
        d_hn, got_up = _mm_call(d_u, wu, tb=True, out_dtypes=(hn.dtype,), b_slots=True,
                                rider=_DeliverRider(ps_up), name=name + "_dhn")
        return (d_hn, jnp.zeros_like(wu), _slot_sum(got_up, name + "_sum_wup"),
                _slot_sum(got_down, name + "_sum_wdown"), dy)

    op.defvjp(forward, bwd)
    return op


def _make_ple(name):
    def forward(hn, w_gate, gate_shard, p, w_proj, proj_shard, h):
        pp = _mm_call(p, w_proj, out_dtypes=(F32,), b_slots=True, name=name + "_proj")[0]
        out, gp = _mm_call(hn, w_gate, extras=(h, pp),
                           epilogue=lambda acc, hv, ppv: (hv + _sigmoid(acc) * ppv, acc),
                           out_dtypes=(F32, F32), name=name + "_gate")
        return out, (hn, w_gate, p, w_proj, gp, pp)

    @jax.custom_vjp
    def op(hn, w_gate, gate_shard, p, w_proj, proj_shard, h):
        return forward(hn, w_gate, gate_shard, p, w_proj, proj_shard, h)[0]

    def bwd(res, dy):
        hn, w_gate, p, w_proj, gp, pp = res

        def gate_grads(g, dyv, gpv, ppv):
            s = _sigmoid(gpv)
            return (dyv * ppv * s * (1.0 - s)).astype(BF16), (dyv * s).astype(BF16)

        t, d = dy.shape
        d_gp, d_pp = _rowmap_call(gate_grads, t, 1, [dy, gp, pp], [], [(d, BF16), (d, BF16)], name + "_dgate")
        d_wproj = _mm_call(p, d_pp, ta=True, out_dtypes=(BF16,), out_slots=True, name=name + "_dwproj")[0]
        ps_proj = _pair_sums(d_wproj, name + "_wproj")
        d_wgate, got_proj = _mm_call(hn, d_gp, ta=True, out_dtypes=(BF16,), rider=_DeliverRider(ps_proj),
                                     name=name + "_dwgate")
        ps_gate = _pair_sums(d_wgate.reshape(N_DEV, -1, d_wgate.shape[1]), name + "_wgate")
        d_hn, got_gate = _mm_call(d_gp, w_gate, tb=True, out_dtypes=(hn.dtype,), rider=_DeliverRider(ps_gate),
                                  name=name + "_dhn")
        return (d_hn, jnp.zeros_like(w_gate), _reduced(got_gate, name + "_wgate"), jnp.zeros_like(p),
                jnp.zeros_like(w_proj), _reduced(got_proj, name + "_wproj"), dy)

    op.defvjp(forward, bwd)
    return op


def _row_tile(t, widths):
    return _tile(t, max(SUBLANES, (256 * 1024) // max(widths)))


def _rowmap_specs(t, g, rows, bcs, tt):
    row_specs = [pl.BlockSpec((tt, r.shape[1] // g), lambda gg, i: (i, gg)) for r in rows]
    bc_specs = []
    for b, per_group in bcs:
        if per_group:
            bc_specs.append(pl.BlockSpec((b.shape[0], b.shape[1] // g), lambda gg, i: (0, gg)))
        else:
            bc_specs.append(pl.BlockSpec(b.shape, lambda gg, i: (0, 0)))
    return row_specs, bc_specs


def _rowmap_call(fn, t, g, rows, bcs, outs, name):
    widths = [r.shape[1] // g for r in rows] + [c // g for c, _ in outs]
    tt = _row_tile(t, widths)
    nr, nb = len(rows), len(bcs)
    row_specs, bc_specs = _rowmap_specs(t, g, rows, bcs, tt)

    def body(*refs):
        vals = [r[...] for r in refs[:nr + nb]]
        res = fn(pl.program_id(0), *vals)
        for o, r in zip(refs[nr + nb:], res):
            o[...] = r.astype(o.dtype)

    return pl.pallas_call(
        body, name=name, grid=(g, t // tt),
        in_specs=row_specs + bc_specs,
        out_specs=[pl.BlockSpec((tt, c // g), lambda gg, i: (i, gg)) for c, _ in outs],
        out_shape=[jax.ShapeDtypeStruct((t, c), d) for c, d in outs],
        compiler_params=_params(("parallel", "parallel")),
    )(*rows, *[b for b, _ in bcs])


def _rowmap_bwd_call(fn, t, g, rows, bcs, cots, name, add0=None):
    widths = [r.shape[1] // g for r in rows] + [c.shape[1] // g for c in cots]
    tt = _row_tile(t, widths)
    nr, nb, nc = len(rows), len(bcs), len(cots)
    na = 0 if add0 is None else 1
    row_specs, bc_specs = _rowmap_specs(t, g, rows, bcs, tt)
    cot_specs = [pl.BlockSpec((tt, c.shape[1] // g), lambda gg, i: (i, gg)) for c in cots]
    add_specs = [] if add0 is None else [row_specs[0]]
    shared = [not per_group for _, per_group in bcs]

    def body(*refs):
        ins = refs[:nr + nb]
        cot_refs = refs[nr + nb:nr + nb + nc]
        add_refs = refs[nr + nb + nc:nr + nb + nc + na]
        d_rows = refs[nr + nb + nc + na:nr + nb + nc + na + nr]
        d_bcs = refs[nr + nb + nc + na + nr:]
        gg, i = pl.program_id(0), pl.program_id(1)
        vals = [r[...] for r in ins]
        _, vjp = jax.vjp(lambda *v: tuple(fn(gg, *v)), *vals)
        grads = vjp(tuple(c[...] for c in cot_refs))
        for j, (o, gr) in enumerate(zip(d_rows, grads[:nr])):
            if j == 0 and na:
                gr = gr + add_refs[0][...]
            o[...] = gr.astype(o.dtype)
        for o, gr, sh in zip(d_bcs, grads[nr:], shared):
            first = jnp.logical_and(i == 0, gg == 0) if sh else i == 0

            @pl.when(first)
            def _():
                o[...] = jnp.zeros_like(o)

            o[...] += gr.astype(o.dtype)

    res = pl.pallas_call(
        body, name=name, grid=(g, t // tt),
        in_specs=row_specs + bc_specs + cot_specs + add_specs,
        out_specs=row_specs + bc_specs,
        out_shape=[jax.ShapeDtypeStruct(r.shape, r.dtype) for r in rows]
        + [jax.ShapeDtypeStruct(b.shape, F32) for b, _ in bcs],
        compiler_params=_params(("arbitrary", "arbitrary")),
    )(*rows, *[b for b, _ in bcs], *cots, *([] if add0 is None else [add0]))
    return res[:nr], res[nr:]


def _make_rowmap(fn, g, n_rows, per_group, outs, name):
    def call(*args):
        rows, bcs = list(args[:n_rows]), list(zip(args[n_rows:], per_group))
        return tuple(_rowmap_call(fn, rows[0].shape[0], g, rows, bcs, outs, name + "_f"))

    @jax.custom_vjp
    def op(*args):
        return call(*args)

    def fwd(*args):
        return call(*args), args

    def bwd(args, cots):
        rows, bcs = list(args[:n_rows]), list(zip(args[n_rows:], per_group))
        d_rows, d_bcs = _rowmap_bwd_call(fn, rows[0].shape[0], g, rows, bcs, list(cots), name + "_b")
        return tuple(d_rows) + tuple(d.astype(b.dtype) for d, (b, _) in zip(d_bcs, bcs))

    op.defvjp(fwd, bwd)
    return op


def _rms_fn(g, h, w):
    y = h * lax.rsqrt(jnp.mean(h * h, axis=-1, keepdims=True) + NORM_EPS)
    return ((y * w).astype(BF16),)


def _make_norm(name):
    def call(h, w):
        return _rowmap_call(_rms_fn, h.shape[0], 1, [h], [(w, False)], [(h.shape[1], BF16)], name + "_f")[0]

    @jax.custom_vjp
    def op(h, w):
        return h, call(h, w)

    def fwd(h, w):
        return (h, call(h, w)), (h, w)

    def bwd(res, cots):
        h, w = res
        dh_pass, dhn = cots
        d_rows, d_bcs = _rowmap_bwd_call(_rms_fn, h.shape[0], 1, [h], [(w, False)], [dhn], name + "_b", add0=dh_pass)
        return d_rows[0], d_bcs[0]

    op.defvjp(fwd, bwd)
    return op


def _loss_call(h, w, target, name):
    t, d = h.shape
    tt = _row_tile(t, [d])

    def body(h_ref, w_ref, t_ref, dh_ref, dw_ref, loss_ref):
        i = pl.program_id(0)
        tgt = t_ref[...]

        def lf(hv, wv):
            y = hv * lax.rsqrt(jnp.mean(hv * hv, axis=-1, keepdims=True) + NORM_EPS) * wv
            err = y - tgt
            return 0.5 * jnp.sum(jnp.mean(err * err, axis=-1, keepdims=True))

        lv, (dh, dw) = jax.value_and_grad(lf, argnums=(0, 1))(h_ref[...], w_ref[...])
        dh_ref[...] = dh

        @pl.when(i == 0)
        def _():
            dw_ref[...] = jnp.zeros_like(dw_ref)
            loss_ref[...] = jnp.zeros_like(loss_ref)

        dw_ref[...] += dw
        loss_ref[...] += jnp.full(loss_ref.shape, lv, F32)

    row = pl.BlockSpec((tt, d), lambda i: (i, 0))
    return pl.pallas_call(
        body, name=name, grid=(t // tt,),
        in_specs=[row, pl.BlockSpec((1, d), lambda i: (0, 0)), row],
        out_specs=[row, pl.BlockSpec((1, d), lambda i: (0, 0)), pl.BlockSpec((SUBLANES, LANES), lambda i: (0, 0))],
        out_shape=[jax.ShapeDtypeStruct((t, d), F32), jax.ShapeDtypeStruct((1, d), F32),
                   jax.ShapeDtypeStruct((SUBLANES, LANES), F32)],
        compiler_params=_params(("arbitrary",)),
    )(h, w, target)


def _make_loss(name):
    @jax.custom_vjp
    def op(h, w, target):
        return _loss_call(h, w, target, name)[2][0, 0]

    def fwd(h, w, target):
        dh, dw, lv = _loss_call(h, w, target, name)
        return lv[0, 0], (dh, dw, target)

    def bwd(res, ct):
        dh, dw, target = res
        return dh * ct, dw * ct, jnp.zeros_like(target)

    op.defvjp(fwd, bwd)
    return op


def _shift_down(cur, halo, s, first):
    if s == 0:
        return cur
    r = pltpu.roll(cur, s, 0)
    p = jnp.where(first, 0.0, pltpu.roll(halo, s, 0))
    rows = lax.broadcasted_iota(jnp.int32, p.shape, 0)
    head = jnp.where(rows < s, p, r[:SUBLANES])
    return jnp.concatenate([head, r[SUBLANES:]], axis=0)


def _shift_up(cur, halo, s, last):
    if s == 0:
        return cur
    n = cur.shape[0]
    r = pltpu.roll(cur, n - s, 0)
    p = jnp.where(last, 0.0, pltpu.roll(halo, SUBLANES - s, 0))
    rows = lax.broadcasted_iota(jnp.int32, p.shape, 0)
    tail = jnp.where(rows >= SUBLANES - s, p, r[n - SUBLANES:])
    return jnp.concatenate([r[:n - SUBLANES], tail], axis=0)


def _conv_specs(t, c):
    tt, cw = _tile(t, 512), _tile(c, 512)
    per = tt // SUBLANES
    nblk = t // SUBLANES
    cur = pl.BlockSpec((tt, cw), lambda j, i: (i, j))
    prev = pl.BlockSpec((SUBLANES, cw), lambda j, i: (jnp.maximum(i * per - 1, 0), j))
    nxt = pl.BlockSpec((SUBLANES, cw), lambda j, i: (jnp.minimum((i + 1) * per, nblk - 1), j))
    wsp = pl.BlockSpec((CONV_W, cw), lambda j, i: (0, j))
    bsp = pl.BlockSpec((1, cw), lambda j, i: (0, j))
    return tt, cw, cur, prev, nxt, wsp, bsp


def _conv_call(x, w, b, name):
    t, c = x.shape
    tt, cw, cur, prev, nxt, wsp, bsp = _conv_specs(t, c)

    def body(x_ref, p_ref, w_ref, b_ref, y_ref):
        first = pl.program_id(1) == 0
        xv, pv = x_ref[...], p_ref[...]
        y = jnp.zeros_like(xv) + b_ref[...]
        for j in range(CONV_W):
            y = y + w_ref[j:j + 1, :] * _shift_down(xv, pv, CONV_W - 1 - j, first)
        y_ref[...] = y

    return pl.pallas_call(
        body, name=name, grid=(c // cw, t // tt),
        in_specs=[cur, prev, wsp, bsp], out_specs=cur,
        out_shape=jax.ShapeDtypeStruct((t, c), F32),
        compiler_params=_params(("parallel", "parallel")),
    )(x, x, w, b)


def _conv_bwd_call(x, w, dy, name):
    t, c = x.shape
    tt, cw, cur, prev, nxt, wsp, bsp = _conv_specs(t, c)
    nt = t // tt

    def body(x_ref, p_ref, w_ref, dy_ref, n_ref, dx_ref, dw_ref, db_ref):
        i = pl.program_id(1)
        first, last = i == 0, i == nt - 1
        xv, pv, dyv, nv = x_ref[...], p_ref[...], dy_ref[...], n_ref[...]

        @pl.when(first)
        def _():
            dw_ref[...] = jnp.zeros_like(dw_ref)
            db_ref[...] = jnp.zeros_like(db_ref)

        dx = jnp.zeros_like(xv)
        for j in range(CONV_W):
            s = CONV_W - 1 - j
            dx = dx + w_ref[j:j + 1, :] * _shift_up(dyv, nv, s, last)
            dw_ref[j:j + 1, :] += jnp.sum(dyv * _shift_down(xv, pv, s, first), axis=0, keepdims=True)
        dx_ref[...] = dx
        db_ref[...] += jnp.sum(dyv, axis=0, keepdims=True)

    return pl.pallas_call(
        body, name=name, grid=(c // cw, nt),
        in_specs=[cur, prev, wsp, cur, nxt], out_specs=[cur, wsp, bsp],
        out_shape=[jax.ShapeDtypeStruct((t, c), F32), jax.ShapeDtypeStruct((CONV_W, c), F32),
                   jax.ShapeDtypeStruct((1, c), F32)],
        compiler_params=_params(("arbitrary", "arbitrary")),
    )(x, x, w, dy, dy)


def _make_conv(name):
    @jax.custom_vjp
    def op(x, w, b):
        return _conv_call(x, w, b, name + "_f")

    def fwd(x, w, b):
        return _conv_call(x, w, b, name + "_f"), (x, w)

    def bwd(res, dy):
        x, w = res
        return tuple(_conv_bwd_call(x, w, dy, name + "_b"))

    op.defvjp(fwd, bwd)
    return op


def _lru_call(a, u, name):
    t, nb, ln = a.shape
    tt = _tile(t, 1024)
    blk = pl.BlockSpec((tt, nb, ln), lambda i: (i, 0, 0))

    def body(a_ref, u_ref, h_ref, carry):
        @pl.when(pl.program_id(0) == 0)
        def _():
            carry[...] = jnp.zeros_like(carry)

        def step(k, h):
            h = a_ref[k] * h + u_ref[k]
            h_ref[k] = h
            return h

        carry[...] = lax.fori_loop(0, tt, step, carry[...], unroll=8)

    return pl.pallas_call(
        body, name=name, grid=(t // tt,), in_specs=[blk, blk], out_specs=blk,
        out_shape=jax.ShapeDtypeStruct(a.shape, F32), scratch_shapes=[pltpu.VMEM((nb, ln), F32)],
        compiler_params=_params(("arbitrary",)),
    )(a, u)


def _lru_bwd_call(a, hs, dy, name):
    t, nb, ln = a.shape
    tt = _tile(t, 1024)
    nt = t // tt
    blk = pl.BlockSpec((tt, nb, ln), lambda i: (nt - 1 - i, 0, 0))
    prev = pl.BlockSpec((1, nb, ln), lambda i: (jnp.maximum((nt - 1 - i) * tt - 1, 0), 0, 0))

    def body(a_ref, h_ref, hp_ref, dy_ref, da_ref, du_ref, carry):
        i = pl.program_id(0)

        @pl.when(i == 0)
        def _():
            carry[...] = jnp.zeros_like(carry)

        h_before = jnp.where(i == nt - 1, 0.0, hp_ref[0])

        def step(k, c):
            r = tt - 1 - k
            dh = dy_ref[r] + c
            du_ref[r] = dh
            da_ref[r] = dh * h_ref[jnp.maximum(r - 1, 0)]
            return a_ref[r] * dh

        carry[...] = lax.fori_loop(0, tt, step, carry[...], unroll=8)
        da_ref[0] = du_ref[0] * h_before

    return pl.pallas_call(
        body, name=name, grid=(nt,), in_specs=[blk, blk, prev, blk], out_specs=[blk, blk],
        out_shape=[jax.ShapeDtypeStruct(a.shape, F32), jax.ShapeDtypeStruct(a.shape, F32)],
        scratch_shapes=[pltpu.VMEM((nb, ln), F32)],
        compiler_params=_params(("arbitrary",)),
    )(a, hs, hs, dy)


def _make_lru(name):
    @jax.custom_vjp
    def op(a, u):
        return _lru_call(a, u, name + "_f")

    def fwd(a, u):
        hs = _lru_call(a, u, name + "_f")
        return hs, (a, hs)

    def bwd(res, dy):
        a, hs = res
        return tuple(_lru_bwd_call(a, hs, dy, name + "_b"))

    op.defvjp(fwd, bwd)
    return op


def _scan_specs(ins, const, heads, hp, chunk, rev_n):
    def tmap(n_of):
        return lambda hg, n: (n_of(n), hg)
    n_of = (lambda n: rev_n - 1 - n) if rev_n else (lambda n: n)
    in_specs = [pl.BlockSpec((chunk, hp * (x.shape[1] // heads)), tmap(n_of)) for x in ins]
    c_spec = pl.BlockSpec((1, hp * (const.shape[1] // heads)), lambda hg, n: (0, hg))
    return in_specs, c_spec, n_of


def _scan_call(chunk_fn, ins, const, heads, hp, chunk, state_shape, out_width, name):
    t = ins[0].shape[0]
    nc = t // chunk
    ni = len(ins)
    in_specs, c_spec, _ = _scan_specs(ins, const, heads, hp, chunk, 0)
    ws = [x.shape[1] // heads for x in ins]
    cw = const.shape[1] // heads
    dk, dv = state_shape

    def body(*refs):
        in_refs, c_ref, o_ref, s_ref, state = refs[:ni], refs[ni], refs[ni + 1], refs[ni + 2], refs[ni + 3]

        @pl.when(pl.program_id(1) == 0)
        def _():
            state[...] = jnp.zeros_like(state)

        vals = [[r[:, k * w:(k + 1) * w] for r, w in zip(in_refs, ws)] for k in range(hp)]
        consts = [c_ref[:, k * cw:(k + 1) * cw] for k in range(hp)]
        s0 = [state[k] for k in range(hp)]
        outs, s1 = chunk_fn(vals, consts, s0)
        for k in range(hp):
            s_ref[0, k] = s0[k]
            o_ref[:, k * out_width:(k + 1) * out_width] = outs[k]
            state[k] = s1[k]

    return pl.pallas_call(
        body, name=name, grid=(heads // hp, nc),
        in_specs=in_specs + [c_spec],
        out_specs=[pl.BlockSpec((chunk, hp * out_width), lambda hg, n: (n, hg)),
                   pl.BlockSpec((1, hp, dk, dv), lambda hg, n: (n, hg, 0, 0))],
        out_shape=[jax.ShapeDtypeStruct((t, heads * out_width), F32),
                   jax.ShapeDtypeStruct((nc, heads, dk, dv), F32)],
        scratch_shapes=[pltpu.VMEM((hp, dk, dv), F32)],
        compiler_params=_params(("parallel", "arbitrary")),
    )(*ins, const)


def _scan_bwd_call(chunk_fn, ins, const, states, d_out, heads, hp, chunk, state_shape, out_width, name):
    t = ins[0].shape[0]
    nc = t // chunk
    ni = len(ins)
    in_specs, c_spec, n_of = _scan_specs(ins, const, heads, hp, chunk, nc)
    ws = [x.shape[1] // heads for x in ins]
    cw = const.shape[1] // heads
    dk, dv = state_shape

    def body(*refs):
        in_refs, c_ref, s_ref, do_ref = refs[:ni], refs[ni], refs[ni + 1], refs[ni + 2]
        d_refs, dstate = refs[ni + 3:ni + 3 + ni], refs[-1]

        @pl.when(pl.program_id(1) == 0)
        def _():
            dstate[...] = jnp.zeros_like(dstate)

        vals = [[r[:, k * w:(k + 1) * w] for r, w in zip(in_refs, ws)] for k in range(hp)]
        consts = [c_ref[:, k * cw:(k + 1) * cw] for k in range(hp)]
        s0 = [s_ref[0, k] for k in range(hp)]
        d_o = [do_ref[:, k * out_width:(k + 1) * out_width] for k in range(hp)]
        d_s1 = [dstate[k] for k in range(hp)]
        _, vjp = jax.vjp(lambda vv, ss: chunk_fn(vv, consts, ss), vals, s0)
        d_vals, d_s0 = vjp((d_o, d_s1))
        for k in range(hp):
            for r, w, gr in zip(d_refs, ws, d_vals[k]):
                r[:, k * w:(k + 1) * w] = gr
            dstate[k] = d_s0[k]

    return pl.pallas_call(
        body, name=name, grid=(heads // hp, nc),
        in_specs=in_specs + [c_spec,
                             pl.BlockSpec((1, hp, dk, dv), lambda hg, n: (n_of(n), hg, 0, 0)),
                             pl.BlockSpec((chunk, hp * out_width), lambda hg, n: (n_of(n), hg))],
        out_specs=in_specs,
        out_shape=[jax.ShapeDtypeStruct(x.shape, F32) for x in ins],
        scratch_shapes=[pltpu.VMEM((hp, dk, dv), F32)],
        compiler_params=_params(("parallel", "arbitrary")),
    )(*ins, const, states, d_out)


def _make_scan(chunk_fn, const, heads, hp, chunk, state_shape, out_width, name):
    def call(*ins):
        return _scan_call(chunk_fn, list(ins), const, heads, hp, chunk, state_shape, out_width, name + "_f")

    @jax.custom_vjp
    def op(*ins):
        return call(*ins)[0]

    def fwd(*ins):
        o, states = call(*ins)
        return o, (ins, states)

    def bwd(res, d_out):
        ins, states = res
        return tuple(_scan_bwd_call(chunk_fn, list(ins), const, states, d_out, heads, hp, chunk, state_shape,
                                    out_width, name + "_b"))

    op.defvjp(fwd, bwd)
    return op


def _tri(c):
    ri = lax.broadcasted_iota(jnp.int32, (c, c), 0)
    ci = lax.broadcasted_iota(jnp.int32, (c, c), 1)
    return ri, ci


def _each(fn, *lists):
    return [fn(*a) for a in zip(*lists)]


@jax.custom_vjp
def _neumann_inverses(xs):
    c = xs[0].shape[0]
    ri, ci = _tri(c)
    eye = jnp.where(ri == ci, 1.0, 0.0)
    invs = [eye + x for x in xs]
    xps = list(xs)
    for _ in range(max(1, int(np.ceil(np.log2(c))) - 1)):
        xps = [_hdot(xp, xp) for xp in xps]
        invs = [inv + _hdot(inv, xp) for inv, xp in zip(invs, xps)]
    return invs


def _neumann_inverses_fwd(xs):
    invs = _neumann_inverses(xs)
    return invs, invs


def _neumann_inverses_bwd(invs, ds):
    hi = lax.Precision.HIGH
    ts = [lax.dot_general(d, inv, (_NT, ((), ())), precision=hi, preferred_element_type=F32)
          for d, inv in zip(ds, invs)]
    return ([lax.dot_general(inv, t, (_TN, ((), ())), precision=hi, preferred_element_type=F32)
             for inv, t in zip(invs, ts)],)


_neumann_inverses.defvjp(_neumann_inverses_fwd, _neumann_inverses_bwd)


def _dn_chunk(vals, consts, ss):
    del consts
    qs, ks, vs, gbs, bbs = (list(x) for x in zip(*vals))
    c = qs[0].shape[0]
    ri, ci = _tri(c)
    causal, strict = ri >= ci, ri > ci
    tri_f = causal.astype(F32)
    gc_b = [_hdot(tri_f, gb) for gb in gbs]
    gcol = [jnp.mean(x, axis=1, keepdims=True) for x in gc_b]
    grow = [jnp.mean(x.T, axis=0, keepdims=True) for x in gc_b]
    bcol = [jnp.mean(bb, axis=1, keepdims=True) for bb in bbs]
    decay = _each(lambda a, b: jnp.where(causal, jnp.exp(jnp.where(causal, a - b, 0.0)), 0.0), gcol, grow)
    kb = _each(lambda k, b: k * b, ks, bcol)
    m = _each(lambda a, k: _bdot(a, k, _NT), kb, ks)
    invs = _neumann_inverses(_each(lambda mm, d: -jnp.where(strict, mm * d, 0.0), m, decay))
    eg = [jnp.exp(a) for a in gcol]
    u = _each(lambda inv, v, b: _hdot(inv, v * b), invs, vs, bcol)
    w = _each(lambda inv, a, e: _hdot(inv, a * e), invs, kb, eg)
    qk = _each(lambda q, k, d: _bdot(q, k, _NT) * d, qs, ks, decay)
    g_last = [jnp.sum(jnp.mean(gb, axis=1, keepdims=True), axis=0, keepdims=True) for gb in gbs]
    k_dec = _each(lambda k, gl, a: k * jnp.exp(gl - a), ks, g_last, gcol)
    ws = _each(lambda w_, s: _bdot(w_, s, _NN), w, ss)
    v_new = _each(lambda u_, x: u_ - x, u, ws)
    o_state = _each(lambda q, e, s: _bdot(q * e, s, _NN), qs, eg, ss)
    o_intra = _each(lambda a, vn: _bdot(a, vn, _NN), qk, v_new)
    s_add = _each(lambda kd, vn: _bdot(kd, vn, _TN), k_dec, v_new)
    outs = _each(lambda a, b: a + b, o_state, o_intra)
    s_new = _each(lambda s, gl, a: s * jnp.exp(gl) + a, ss, g_last, s_add)
    return outs, s_new


def _ret_chunk(vals, consts, ss):
    qs, ks, vs = (list(x) for x in zip(*vals))
    c = qs[0].shape[0]
    ri, ci = _tri(c)
    rel = (ri - ci).astype(F32)
    idx = lax.broadcasted_iota(jnp.int32, (c, 1), 0).astype(F32)
    lg = [jnp.mean(x, axis=1, keepdims=True) for x in consts]
    dmask = [jnp.where(rel >= 0, jnp.exp(jnp.maximum(rel, 0.0) * a), 0.0) for a in lg]
    ksc = [k * (RET_DK ** -0.5) for k in ks]
    qk = _each(lambda q, k, d: _bdot(q, k, _NT) * d, qs, ksc, dmask)
    intra = _each(lambda a, v: _bdot(a, v, _NN), qk, vs)
    inter = _each(lambda q, a, s: _bdot(q * jnp.exp((idx + 1.0) * a), s, _NN), qs, lg, ss)
    s_add = _each(lambda k, a, v: _bdot(k * jnp.exp((c - 1.0 - idx) * a), v, _TN), ksc, lg, vs)
    outs = _each(lambda a, b: a + b, intra, inter)
    s_new = _each(lambda s, a, x: s * jnp.exp(c * a) + x, ss, lg, s_add)
    return outs, s_new


def _attn_block(args, slopes, first):
    qs, kps, kcs, vps, vcs = (list(x) for x in zip(*args))
    b = qs[0].shape[0]
    ri, ci = _tri(b)
    rel_c = ri - ci
    rel_p = rel_c + b
    ok_c = rel_c >= 0
    ok_p = jnp.logical_and(rel_p <= b, jnp.logical_not(first))
    rel_cf, rel_pf = rel_c.astype(F32), rel_p.astype(F32)
    qsc = [q * (HEAD ** -0.5) for q in qs]
    s_c = _each(lambda q, k, sl: jnp.where(ok_c, _bdot(q, k, _NT) - sl * rel_cf, NEG), qsc, kcs, slopes)
    s_p = _each(lambda q, k, sl: jnp.where(ok_p, _bdot(q, k, _NT) - sl * rel_pf, NEG), qsc, kps, slopes)
    mx = _each(lambda a, c: lax.stop_gradient(jnp.maximum(jnp.max(a, axis=1, keepdims=True),
                                                          jnp.max(c, axis=1, keepdims=True))), s_c, s_p)
    p_c = _each(lambda a, m: jnp.exp(a - m), s_c, mx)
    p_p = _each(lambda a, m: jnp.exp(a - m), s_p, mx)
    den = _each(lambda a, c: jnp.sum(a, axis=1, keepdims=True) + jnp.sum(c, axis=1, keepdims=True), p_c, p_p)
    o_c = _each(lambda a, v: _bdot(a, v, _NN), p_c, vcs)
    o_p = _each(lambda a, v: _bdot(a, v, _NN), p_p, vps)
    outs = _each(lambda a, c, d: (a + c) / d, o_c, o_p, den)
    lses = _each(lambda m, d, o: jnp.broadcast_to(m + jnp.log(d), o.shape), mx, den, outs)
    return outs, lses


ATTN_GROUP = 4


def _attn_plan(t, hw, dil):
    nh = hw // HEAD
    hb = 1 if dil > 1 else max(h for h in (4, 2, 1) if nh % h == 0)
    units = [(r, h) for r in range(dil) for h in range(hb)]
    groups = [units[i:i + ATTN_GROUP] for i in range(0, len(units), ATTN_GROUP)]
    return hb, SWA_BLOCK * dil, groups


class _unit:
    def __init__(self, ref, r, h, dil):
        rows = pl.ds(0, SWA_BLOCK) if dil == 1 else pl.ds(r, SWA_BLOCK, stride=dil)
        self.ref, self.idx = ref, (rows, pl.ds(h * HEAD, HEAD))

    def __getitem__(self, _):
        return self.ref[self.idx]

    def __setitem__(self, _, value):
        self.ref[self.idx] = value


def _attn_call(q, k, v, slopes, dil, name):
    t, hw = q.shape
    hb, rows, groups = _attn_plan(t, hw, dil)
    wide = hb * HEAD
    cur = pl.BlockSpec((rows, wide), lambda j, n: (n, j))
    prev = pl.BlockSpec((rows, wide), lambda j, n: (jnp.maximum(n - 1, 0), j))
    ssp = pl.BlockSpec((1, wide), lambda j, n: (0, j))

    def body(q_ref, kp_ref, kc_ref, vp_ref, vc_ref, s_ref, o_ref, l_ref):
        first = pl.program_id(1) == 0
        for grp in groups:
            args = [[_unit(x, r, h, dil)[...] for x in (q_ref, kp_ref, kc_ref, vp_ref, vc_ref)] for r, h in grp]
            outs, lses = _attn_block(args, [s_ref[:, h * HEAD:(h + 1) * HEAD] for _, h in grp], first)
            for (r, h), o, lse in zip(grp, outs, lses):
                _unit(o_ref, r, h, dil)[...] = o
                _unit(l_ref, r, h, dil)[...] = lse

    return pl.pallas_call(
        body, name=name, grid=(hw // wide, t // rows),
        in_specs=[cur, prev, cur, prev, cur, ssp], out_specs=[cur, cur],
        out_shape=[jax.ShapeDtypeStruct((t, hw), F32), jax.ShapeDtypeStruct((t, hw), F32)],
        compiler_params=_params(("parallel", "parallel")),
    )(q, k, k, v, v, slopes)


def _attn_bwd_call(q, k, v, slopes, d_o, d_l, dil, name):
    t, hw = q.shape
    hb, rows, groups = _attn_plan(t, hw, dil)
    wide = hb * HEAD
    nb = t // rows
    cur = pl.BlockSpec((rows, wide), lambda j, n: (jnp.minimum(n, nb - 1), j))
    prev = pl.BlockSpec((rows, wide), lambda j, n: (jnp.clip(n - 1, 0, nb - 1), j))
    ssp = pl.BlockSpec((1, wide), lambda j, n: (0, j))

    def body(q_ref, kp_ref, kc_ref, vp_ref, vc_ref, s_ref, do_ref, dl_ref, dq_ref, dk_ref, dv_ref, ck, cv):
        n = pl.program_id(1)

        @pl.when(n == 0)
        def _():
            ck[...] = jnp.zeros_like(ck)
            cv[...] = jnp.zeros_like(cv)

        @pl.when(n < nb)
        def _():
            first = n == 0
            for grp in groups:
                args = [[_unit(x, r, h, dil)[...] for x in (q_ref, kp_ref, kc_ref, vp_ref, vc_ref)] for r, h in grp]
                svs = [s_ref[:, h * HEAD:(h + 1) * HEAD] for _, h in grp]
                cots = ([_unit(do_ref, r, h, dil)[...] for r, h in grp], [_unit(dl_ref, r, h, dil)[...] for r, h in grp])
                carry = [(_unit(ck, r, h, dil)[...], _unit(cv, r, h, dil)[...]) for r, h in grp]
                _, vjp = jax.vjp(lambda a, svs=svs: _attn_block(a, svs, first), args)
                grads = vjp(cots)[0]
                for (r, h), (dq, dkp, dkc, dvp, dvc), (c_k, c_v) in zip(grp, grads, carry):
                    _unit(dq_ref, r, h, dil)[...] = dq
                    _unit(dk_ref, r, h, dil)[...] = c_k + dkp
                    _unit(dv_ref, r, h, dil)[...] = c_v + dvp
                    _unit(ck, r, h, dil)[...] = dkc
                    _unit(cv, r, h, dil)[...] = dvc

        @pl.when(n == nb)
        def _():
            dk_ref[...] = ck[...]
            dv_ref[...] = cv[...]

    return pl.pallas_call(
        body, name=name, grid=(hw // wide, nb + 1),
        in_specs=[cur, prev, cur, prev, cur, ssp, cur, cur], out_specs=[cur, prev, prev],
        out_shape=[jax.ShapeDtypeStruct((t, hw), F32)] * 3,
        scratch_shapes=[pltpu.VMEM((rows, wide), F32), pltpu.VMEM((rows, wide), F32)],
        compiler_params=_params(("parallel", "arbitrary")),
    )(q, k, k, v, v, slopes, d_o, d_l)


def _make_attn(slopes, dil, name):
    @jax.custom_vjp
    def op(q, k, v):
        return tuple(_attn_call(q, k, v, slopes, dil, name + "_f"))

    def fwd(q, k, v):
        return tuple(_attn_call(q, k, v, slopes, dil, name + "_f")), (q, k, v)

    def bwd(res, cots):
        q, k, v = res
        return tuple(_attn_bwd_call(q, k, v, slopes, cots[0], cots[1], dil, name + "_b"))

    op.defvjp(fwd, bwd)
    return op


def _dn_pre_fn(g, cq, ck, cv):
    sq, sk, sv = _silu(cq), _silu(ck), _silu(cv)
    qn = sq * lax.rsqrt(jnp.sum(sq * sq, axis=-1, keepdims=True) + 1e-6) * (HEAD ** -0.5)
    kn = sk * lax.rsqrt(jnp.sum(sk * sk, axis=-1, keepdims=True) + 1e-6)
    return qn, kn, sv


def _make_dn_gates_fn(heads):
    def fn(g, ba, a_log, dt_bias):
        lane = lax.broadcasted_iota(jnp.int32, ba.shape, 1)
        lane1 = lax.broadcasted_iota(jnp.int32, a_log.shape, 1)
        betas, gs = [], []
        for h in range(heads):
            b_raw = jnp.sum(jnp.where(lane == h, ba, 0.0), axis=1, keepdims=True)
            a_raw = jnp.sum(jnp.where(lane == heads + h, ba, 0.0), axis=1, keepdims=True)
            al = jnp.sum(jnp.where(lane1 == h, a_log, 0.0), axis=1, keepdims=True)
            dt = jnp.sum(jnp.where(lane1 == h, dt_bias, 0.0), axis=1, keepdims=True)
            beta = _sigmoid(b_raw)
            gl = -jnp.exp(al) * _softplus(a_raw + dt)
            betas.append(jnp.broadcast_to(beta, ba.shape))
            gs.append(jnp.broadcast_to(gl, ba.shape))
        return jnp.concatenate(betas, axis=1), jnp.concatenate(gs, axis=1)
    return fn


def _dn_post_fn(g, o, z, w):
    y = o * lax.rsqrt(jnp.mean(o * o, axis=-1, keepdims=True) + NORM_EPS) * w
    return ((y * _silu(z)).astype(BF16),)


def _lru_pre_fn(g, xc, wa, wx, ba, bx, lam):
    r = _sigmoid(_bdot(xc, wa, _NN) + ba)
    i = _sigmoid(_bdot(xc, wx, _NN) + bx)
    log_a = -LRU_C * r * _softplus(-lam)
    a = jnp.exp(log_a)
    u = jnp.sqrt(1.0 - jnp.exp(2.0 * log_a)) * (i * xc)
    return a, u


def _lru_post_fn(g, hs, yr):
    return ((hs * _gelu(yr)).astype(BF16),)


def _merge_fn(g, o1, o2, o3, l1, l2, l3):
    m = lax.stop_gradient(jnp.maximum(jnp.maximum(l1, l2), l3))
    e1, e2, e3 = jnp.exp(l1 - m), jnp.exp(l2 - m), jnp.exp(l3 - m)
    return (((e1 * o1 + e2 * o2 + e3 * o3) / (e1 + e2 + e3)).astype(BF16),)


def _ret_post_fn(g, o, gate):
    mu = jnp.mean(o, axis=-1, keepdims=True)
    xc = o - mu
    y = xc * lax.rsqrt(jnp.mean(xc * xc, axis=-1, keepdims=True) + GN_EPS)
    return ((y * _silu(gate)).astype(BF16),)


def _pad_lanes(v):
    return jnp.pad(v, (0, LANES - v.shape[0]))[None, :]


def _even_layout(half):
    heads = half // HEAD
    qkv = 3 * half
    segs = [(0, qkv, qkv), (qkv, half, half), (qkv + half, 2 * heads, LANES),
            (qkv + half + 2 * heads, half, half), (qkv + 2 * half + 2 * heads, half, half)]
    return segs


def _pad_ev_w_in(w, half):
    parts = []
    for start, width, padded in _even_layout(half):
        part = w[:, start:start + width]
        if padded != width:
            part = jnp.pad(part, ((0, 0), (0, padded - width)))
        parts.append(part)
    return jnp.concatenate(parts, axis=1)


def _even_mixer(hn, h, lw):
    t, d = h.shape
    half = d // 2
    heads = half // HEAD
    hp = 4 if heads % 4 == 0 else 1
    def assemble(shards):
        return _pad_ev_w_in(shards.transpose(1, 0, 2).reshape(d, -1), half)

    def to_slots(d_padded):
        return jax.vjp(assemble, jnp.zeros_like(lw["w_in"]))[1](d_padded)[0]

    proj, passenger = _make_mm("ev_in", F32, to_slots=to_slots)(hn, assemble(lw["w_in"]), lw["w_in_shard"],
                                                                lw["passenger"])
    o0 = 0
    segs = []
    for _, _, padded in _even_layout(half):
        segs.append(proj[:, o0:o0 + padded])
        o0 += padded
    qkv, z, ba, xr, yr = segs
    c = _make_conv("dn_conv")(qkv, lw["dn_conv_w"], jnp.zeros((1, 3 * half), F32))
    q, k, v = _make_rowmap(_dn_pre_fn, heads, 3, [], [(half, F32)] * 3, "dn_pre")(
        c[:, :half], c[:, half:2 * half], c[:, 2 * half:])
    beta_b, g_b = _make_rowmap(_make_dn_gates_fn(heads), 1, 1, [False, False], [(half, F32)] * 2, "dn_gates")(
        ba, _pad_lanes(lw["dn_a_log"]), _pad_lanes(lw["dn_dt_bias"]))
    o = _make_scan(_dn_chunk, jnp.zeros((1, heads * LANES), F32), heads, hp, DN_CHUNK, (HEAD, HEAD), HEAD, "dn_core")(
        q, k, v, g_b, beta_b)
    ya = _make_rowmap(_dn_post_fn, heads, 2, [False], [(half, BF16)], "dn_post")(o, z, lw["dn_norm_w"][None, :])[0]
    nblk = lw["lru_wa"].shape[0]
    xc = _make_conv("lru_conv")(xr, lw["lru_conv_w"], lw["lru_conv_b"][None, :])
    wa = lw["lru_wa"].transpose(1, 0, 2).reshape(HEAD, nblk * HEAD)
    wx = lw["lru_wx"].transpose(1, 0, 2).reshape(HEAD, nblk * HEAD)
    a, u = _make_rowmap(_lru_pre_fn, nblk, 1, [True] * 5, [(half, F32)] * 2, "lru_pre")(
        xc, wa, wx, lw["lru_ba"][None, :], lw["lru_bx"][None, :], lw["lru_lambda"][None, :])
    hs = _make_lru("lru_scan")(a.reshape(t, nblk, HEAD), u.reshape(t, nblk, HEAD)).reshape(t, half)
    yb = _make_rowmap(_lru_post_fn, nblk, 2, [], [(half, BF16)], "lru_post")(hs, yr)[0]
    return _make_mm_res("ev_out")(jnp.concatenate([ya, yb], axis=1), lw["w_out"], lw["w_out_shard"], h), passenger


def _odd_mixer(hn, h, lw):
    t, d = h.shape
    half = d // 2
    heads = half // HEAD
    rheads = half // RET_DV
    rqk = rheads * RET_DK
    proj, passenger = _make_mm("od_in", F32, slots=True)(hn, lw["w_in"], lw["w_in_shard"], lw["passenger"])
    cq, ck, cv = proj[:, :half], proj[:, half:2 * half], proj[:, 2 * half:3 * half]
    o1 = 3 * half
    rq, rk = proj[:, o1:o1 + rqk], proj[:, o1 + rqk:o1 + 2 * rqk]
    rv, rg = proj[:, o1 + 2 * rqk:o1 + 2 * rqk + half], proj[:, o1 + 2 * rqk + half:]
    slopes = np.exp2(-8.0 * np.arange(1, heads + 1, dtype=np.float64) / heads)
    outs, lses = [], []
    for window, dil in SWA_BRANCHES:
        assert window // dil == SWA_BLOCK and (t // dil) % SWA_BLOCK == 0
        sl = jnp.asarray(np.repeat(slopes * dil, HEAD)[None, :], F32)
        o_i, l_i = _make_attn(sl, dil, "swa_d%d" % dil)(cq, ck, cv)
        outs.append(o_i)
        lses.append(l_i)
    yc = _make_rowmap(_merge_fn, heads, 6, [], [(half, BF16)], "swa_merge")(*outs, *lses)[0]
    lg = np.log1p(-np.exp2(-5.0 - np.arange(rheads, dtype=np.float64)))
    lg_b = jnp.asarray(np.repeat(lg, LANES)[None, :], F32)
    hp = 4 if rheads % 4 == 0 else 1
    o_r = _make_scan(_ret_chunk, lg_b, rheads, hp, RET_CHUNK, (RET_DK, RET_DV), RET_DV, "ret_core")(rq, rk, rv)
    yd = _make_rowmap(_ret_post_fn, rheads, 2, [], [(half, BF16)], "ret_post")(o_r, rg)[0]
    return _make_mm_res("od_out")(jnp.concatenate([yc, yd], axis=1), lw["w_out"], lw["w_out_shard"], h), passenger


def _local_loss(shards, small, x, big, p, target):
    ffn = shards
    depth = ffn["w_up"].shape[0]
    h = x
    for i in range(depth):
        j = i // 2
        h, hn = _make_norm("ln_mix")(h, small["ln_mix_w"][i][None, :])
        w_up_b = lax.stop_gradient(ffn["w_up"][i]).astype(BF16)
        if i % 2 == 0:
            lw = {"w_in": big["ev_w_in"][j], "w_out": big["ev_w_out"][j], "passenger": w_up_b,
                  "w_in_shard": shards["ev_w_in"][j], "w_out_shard": shards["ev_w_out"][j]}
            for nm in ("dn_conv_w", "dn_a_log", "dn_dt_bias", "dn_norm_w", "lru_conv_w", "lru_conv_b", "lru_wa",
                       "lru_ba", "lru_wx", "lru_bx", "lru_lambda"):
                lw[nm] = small[nm][j]
            h, wu = _even_mixer(hn, h, lw)
        else:
            h, wu = _odd_mixer(hn, h, {"w_in": big["od_w_in"][j], "w_out": big["od_w_out"][j], "passenger": w_up_b,
                                       "w_in_shard": shards["od_w_in"][j], "w_out_shard": shards["od_w_out"][j]})
        h, hn = _make_norm("ln_mlp")(h, small["ln_mlp_w"][i][None, :])
        h = _make_ffn("ffn")(hn, wu, ffn["w_up"][i], ffn["w_down"][i], h)
        h, hn = _make_norm("ln_ple")(h, small["ln_ple_w"][i][None, :])
        h = _make_ple("ple")(hn, big["w_ple_gate"][i], shards["w_ple_gate"][i], p[i], big["w_ple_proj"][i],
                             shards["w_ple_proj"][i], h)
    return _make_loss("loss_head")(h, small["ln_final_w"][None, :], target)


def _all_gather(x, name):
    r, c = x.shape

    def body(x_ref, out_ref, send_sems, recv_sems, local_sem):
        mx, my, mc = lax.axis_index("x"), lax.axis_index("y"), lax.axis_index("c")
        me, sibling = (mx, my, mc), (mx, my, 1 - mc)
        chips = [(1 - mx, my), (mx, 1 - my), (1 - mx, 1 - my)]

        def slot(px, py, pc):
            return out_ref.at[4 * px + 2 * py + pc]

        def copy(k, block, to, src=None):
            return pltpu.make_async_remote_copy(
                src_ref=slot(*block) if src is None else src, dst_ref=slot(*block),
                send_sem=send_sems.at[k], recv_sem=recv_sems.at[k], device_id=to, device_id_type=MESH)

        mine = pltpu.make_async_copy(x_ref, slot(*me), local_sem)
        mine.start()
        first = [copy(0, me, sibling, src=x_ref)]
        first += [copy(1 + j, me, (*chip, mc), src=x_ref) for j, chip in enumerate(chips)]
        for cp in first:
            cp.start()
        passed = [copy(4 + j, (*chip, mc), sibling) for j, chip in enumerate(chips)]
        for j, chip in enumerate(chips):
            copy(1 + j, (*chip, mc), me).wait_recv()
            passed[j].start()
        copy(0, sibling, me).wait_recv()
        for j, chip in enumerate(chips):
            copy(4 + j, (*chip, 1 - mc), me).wait_recv()
        for cp in first + passed:
            cp.wait_send()
        mine.wait()

    return pl.pallas_call(
        body, name=name,
        out_shape=jax.ShapeDtypeStruct((N_DEV, r, c), x.dtype),
        in_specs=[pl.BlockSpec(memory_space=pl.ANY)],
        out_specs=pl.BlockSpec(memory_space=pl.ANY),
        scratch_shapes=[pltpu.SemaphoreType.DMA((7,)), pltpu.SemaphoreType.DMA((7,)), pltpu.SemaphoreType.DMA],
    )(x)


def _gather_layers(x, name):
    n, r, c = x.shape

    def body(x_ref, *rest):
        outs, (send_sems, recv_sems, local_sems) = rest[:n], rest[n:]
        mx, my, mc = lax.axis_index("x"), lax.axis_index("y"), lax.axis_index("c")
        me, sibling = (mx, my, mc), (mx, my, 1 - mc)
        chips = [(1 - mx, my), (mx, 1 - my), (1 - mx, 1 - my)]

        def slot(l, px, py, pc):
            return outs[l].at[4 * px + 2 * py + pc]

        def copy(k, l, block, to, from_shard=False):
            return pltpu.make_async_remote_copy(
                src_ref=x_ref.at[l] if from_shard else slot(l, *block), dst_ref=slot(l, *block),
                send_sem=send_sems.at[k, l], recv_sem=recv_sems.at[k, l], device_id=to, device_id_type=MESH)

        mine = [pltpu.make_async_copy(x_ref.at[l], slot(l, *me), local_sems.at[l]) for l in range(n)]
        for cp in mine:
            cp.start()
        sent = [copy(0, l, me, sibling, True) for l in range(n)]
        sent += [copy(1 + j, l, me, (*chip, mc), True) for j, chip in enumerate(chips) for l in range(n)]
        for cp in sent:
            cp.start()
        for j, chip in enumerate(chips):
            for l in range(n):
                copy(1 + j, l, (*chip, mc), me).wait_recv()
                passed = copy(4 + j, l, (*chip, mc), sibling)
                passed.start()
                sent.append(passed)
        for l in range(n):
            copy(0, l, sibling, me).wait_recv()
        for j, chip in enumerate(chips):
            for l in range(n):
                copy(4 + j, l, (*chip, 1 - mc), me).wait_recv()
        for cp in sent:
            cp.wait_send()
        for cp in mine:
            cp.wait()

    return pl.pallas_call(
        body, name=name,
        out_shape=[jax.ShapeDtypeStruct((N_DEV, r, c), x.dtype)] * n,
        in_specs=[pl.BlockSpec(memory_space=pl.ANY)],
        out_specs=[pl.BlockSpec(memory_space=pl.ANY)] * n,
        scratch_shapes=[pltpu.SemaphoreType.DMA((7, n)), pltpu.SemaphoreType.DMA((7, n)),
                        pltpu.SemaphoreType.DMA((n,))],
    )(x)


def _swap_pairs(gs, name):
    n = len(gs)
    _, r, c = gs[0].shape

    def body(*refs):
        g_refs, out_ref, send_sems, recv_sems = refs[:n], refs[n], refs[n + 1], refs[n + 2]
        mx, my, mc = lax.axis_index("x"), lax.axis_index("y"), lax.axis_index("c")

        def copy(q, l):
            return pltpu.make_async_remote_copy(
                src_ref=g_refs[l].at[2 * q + (1 - mc)], dst_ref=out_ref.at[q, l],
                send_sem=send_sems.at[q, l], recv_sem=recv_sems.at[q, l],
                device_id=(mx, my, 1 - mc), device_id_type=MESH)

        copies = [copy(q, l) for q in range(4) for l in range(n)]
        for cp in copies:
            cp.start()
        for cp in copies:
            cp.wait_recv()
        for cp in copies:
            cp.wait_send()

    return pl.pallas_call(
        body, name=name,
        out_shape=jax.ShapeDtypeStruct((4, n, r, c), gs[0].dtype),
        in_specs=[pl.BlockSpec(memory_space=pl.ANY)] * n,
        out_specs=pl.BlockSpec(memory_space=pl.ANY),
        scratch_shapes=[pltpu.SemaphoreType.DMA((4, n)), pltpu.SemaphoreType.DMA((4, n))],
    )(*gs)


def _pair_sum(g, recv, layer, side, name):
    _, r, c = g.shape
    tr = _tile(r, max(SUBLANES, (256 * 1024) // c))

    def body(side_ref, g_ref, r_ref, o_ref):
        del side_ref
        o_ref[...] = (g_ref[...].astype(F32) + r_ref[...].astype(F32)).astype(o_ref.dtype)

    return pl.pallas_call(
        body, name=name,
        grid_spec=pltpu.PrefetchScalarGridSpec(
            num_scalar_prefetch=1, grid=(4, r // tr),
            in_specs=[pl.BlockSpec((None, None, tr, c), lambda q, t, side_ref: (q, side_ref[0], t, 0)),
                      pl.BlockSpec((None, None, tr, c), lambda q, t, side_ref: (q, layer, t, 0))],
            out_specs=pl.BlockSpec((None, tr, c), lambda q, t, side_ref: (q, t, 0))),
        out_shape=jax.ShapeDtypeStruct((4, r, c), g.dtype),
        compiler_params=_params(("parallel", "parallel")),
    )(side, g.reshape(4, 2, r, c), recv)


def _pair_sums(g, name):
    side = lax.axis_index("c").astype(jnp.int32).reshape(1)
    return _pair_sum(g, _swap_pairs([g], name + "_swap"), 0, side, name + "_pairsum")


class _GatherRider:
    def __init__(self, x):
        self.inputs = [x]
        self.out_shapes = [jax.ShapeDtypeStruct((N_DEV,) + x.shape, x.dtype)]
        self.scratch = [pltpu.SemaphoreType.DMA((7,)), pltpu.SemaphoreType.DMA((7,)), pltpu.SemaphoreType.DMA]

    @staticmethod
    def _plan(ins, outs, sems):
        x_ref, out_ref = ins[0], outs[0]
        send_sems, recv_sems, local_sem = sems
        mx, my, mc = lax.axis_index("x"), lax.axis_index("y"), lax.axis_index("c")
        me, sibling = (mx, my, mc), (mx, my, 1 - mc)
        chips = [(1 - mx, my), (mx, 1 - my), (1 - mx, 1 - my)]

        def slot(px, py, pc):
            return out_ref.at[4 * px + 2 * py + pc]

        def copy(k, block, to, from_shard=False):
            return pltpu.make_async_remote_copy(
                src_ref=x_ref if from_shard else slot(*block), dst_ref=slot(*block),
                send_sem=send_sems.at[k], recv_sem=recv_sems.at[k], device_id=to, device_id_type=MESH)

        mine = pltpu.make_async_copy(x_ref, slot(*me), local_sem)
        first = [copy(0, me, sibling, True)] + [copy(1 + j, me, (*chip, mc), True) for j, chip in enumerate(chips)]
        return me, sibling, chips, mc, copy, mine, first

    def start(self, ins, outs, sems):
        _, _, _, _, _, mine, first = self._plan(ins, outs, sems)
        mine.start()
        for cp in first:
            cp.start()

    def finish(self, ins, outs, sems):
        me, sibling, chips, mc, copy, mine, first = self._plan(ins, outs, sems)
        passed = []
        for j, chip in enumerate(chips):
            copy(1 + j, (*chip, mc), me).wait_recv()
            passed.append(copy(4 + j, (*chip, mc), sibling))
            passed[-1].start()
        copy(0, sibling, me).wait_recv()
        for j, chip in enumerate(chips):
            copy(4 + j, (*chip, 1 - mc), me).wait_recv()
        for cp in first + passed:
            cp.wait_send()
        mine.wait()


class _DeliverRider:
    def __init__(self, ps):
        self.inputs = [ps]
        self.out_shapes = [jax.ShapeDtypeStruct(ps.shape, ps.dtype)]
        self.scratch = [pltpu.SemaphoreType.DMA((3,)), pltpu.SemaphoreType.DMA((3,)), pltpu.SemaphoreType.DMA]

    @staticmethod
    def _plan(ins, outs, sems):
        p_ref, out_ref = ins[0], outs[0]
        send_sems, recv_sems, local_sem = sems
        mx, my, mc = lax.axis_index("x"), lax.axis_index("y"), lax.axis_index("c")
        q_me = 2 * mx + my
        mine = pltpu.make_async_copy(p_ref.at[q_me], out_ref.at[q_me], local_sem)
        sent, expected = [], []
        for k in range(1, 4):
            fx, fy = (k >> 1) & 1, k & 1
            px = mx + fx - 2 * mx * fx
            py = my + fy - 2 * my * fy
            q_peer = 2 * px + py
            for dst, into in ((q_me, sent), (q_peer, expected)):
                into.append(pltpu.make_async_remote_copy(
                    src_ref=p_ref.at[q_peer], dst_ref=out_ref.at[dst], send_sem=send_sems.at[k - 1],
                    recv_sem=recv_sems.at[k - 1], device_id=(px, py, mc), device_id_type=MESH))
        return mine, sent, expected

    def start(self, ins, outs, sems):
        mine, sent, _ = self._plan(ins, outs, sems)
        mine.start()
        for cp in sent:
            cp.start()

    def finish(self, ins, outs, sems):
        mine, sent, expected = self._plan(ins, outs, sems)
        for cp in expected:
            cp.wait_recv()
        for cp in sent:
            cp.wait_send()
        mine.wait()


def _slot_sum(slots, name):
    ns, r, c = slots.shape
    tr = _tile(r, 256)

    def body(s_ref, o_ref):
        acc = s_ref[0].astype(F32)
        for s in range(1, ns):
            acc = acc + s_ref[s].astype(F32)
        o_ref[...] = acc

    return pl.pallas_call(
        body, name=name, grid=(r // tr,),
        in_specs=[pl.BlockSpec((ns, tr, c), lambda i: (0, i, 0))],
        out_specs=pl.BlockSpec((tr, c), lambda i: (i, 0)),
        out_shape=jax.ShapeDtypeStruct((r, c), F32),
        compiler_params=_params(("parallel",)),
    )(slots)


def _adamw(slots, w, m, v, name):
    ns, r, c = slots.shape
    tr = _tile(r, max(SUBLANES, (128 * 1024) // c))

    def body(s_ref, w_ref, m_ref, v_ref, g_out, d_out, m_out, v_out):
        g = s_ref[0].astype(F32)
        for s in range(1, ns):
            g = g + s_ref[s].astype(F32)
        mn = ADAM_B1 * m_ref[...] + (1.0 - ADAM_B1) * g
        vn = ADAM_B2 * v_ref[...] + (1.0 - ADAM_B2) * (g * g)
        m_hat = mn / (1.0 - ADAM_B1 ** ADAM_STEP)
        v_hat = vn / (1.0 - ADAM_B2 ** ADAM_STEP)
        g_out[...] = g
        d_out[...] = -ADAM_LR * (m_hat / (jnp.sqrt(v_hat) + ADAM_EPS) + ADAM_WD * w_ref[...])
        m_out[...] = mn
        v_out[...] = vn

    blk = pl.BlockSpec((tr, c), lambda i: (i, 0))
    return pl.pallas_call(
        body, name=name, grid=(r // tr,),
        in_specs=[pl.BlockSpec((ns, tr, c), lambda i: (0, i, 0)), blk, blk, blk],
        out_specs=[blk] * 4,
        out_shape=[jax.ShapeDtypeStruct((r, c), F32)] * 4,
        compiler_params=_params(("parallel",)),
    )(slots, w, m, v)


def _pack(arrays, dtype, row_multiple=SUBLANES):
    flat = jnp.concatenate([a.astype(dtype).reshape(-1) for a in arrays])
    unit = row_multiple * PACK_COLS
    total = -(-flat.shape[0] // unit) * unit
    if total != flat.shape[0]:
        flat = jnp.pad(flat, (0, total - flat.shape[0]))
    return flat.reshape(-1, PACK_COLS)


def _unpack(buf, shapes):
    flat = buf.reshape(-1)
    out, o = [], 0
    for s in shapes:
        n = int(np.prod(s))
        out.append(flat[o:o + n].reshape(s))
        o += n
    return out


FFN = ("w_up", "w_down")
BIG = ("w_ple_proj", "w_ple_gate", "ev_w_in", "ev_w_out", "od_w_in", "od_w_out")
BIG_COL_SHARDED = {"w_ple_proj": True, "w_ple_gate": False, "ev_w_in": True, "ev_w_out": False,
                   "od_w_in": True, "od_w_out": False}
SMALL_SHARDED = ("dn_conv_w", "lru_conv_w")
SMALL_REPLICATED = ("ln_mix_w", "ln_mlp_w", "ln_ple_w", "ln_final_w", "dn_a_log", "dn_dt_bias", "dn_norm_w",
                    "lru_conv_b", "lru_wa", "lru_ba", "lru_wx", "lru_bx", "lru_lambda")
WEIGHTS = ("ln_mix_w", "ln_mlp_w", "ln_ple_w", "w_up", "w_down", "w_ple_proj", "w_ple_gate", "ln_final_w",
           "ev_w_in", "ev_w_out", "dn_conv_w", "dn_a_log", "dn_dt_bias", "dn_norm_w", "lru_conv_w", "lru_conv_b",
           "lru_wa", "lru_ba", "lru_wx", "lru_bx", "lru_lambda", "od_w_in", "od_w_out")


def kernel(x, p, ln_mix_w, ln_mlp_w, ln_ple_w, w_up, w_down, w_ple_proj, w_ple_gate, ln_final_w, ev_w_in, ev_w_out, dn_conv_w, dn_a_log, dn_dt_bias, dn_norm_w, lru_conv_w, lru_conv_b, lru_wa, lru_ba, lru_wx, lru_bx, lru_lambda, od_w_in, od_w_out, loss_target, m_ln_mix_w, m_ln_mlp_w, m_ln_ple_w, m_w_up, m_w_down, m_w_ple_proj, m_w_ple_gate, m_ln_final_w, m_ev_w_in, m_ev_w_out, m_dn_conv_w, m_dn_a_log, m_dn_dt_bias, m_dn_norm_w, m_lru_conv_w, m_lru_conv_b, m_lru_wa, m_lru_ba, m_lru_wx, m_lru_bx, m_lru_lambda, m_od_w_in, m_od_w_out, v_ln_mix_w, v_ln_mlp_w, v_ln_ple_w, v_w_up, v_w_down, v_w_ple_proj, v_w_ple_gate, v_ln_final_w, v_ev_w_in, v_ev_w_out, v_dn_conv_w, v_dn_a_log, v_dn_dt_bias, v_dn_norm_w, v_lru_conv_w, v_lru_conv_b, v_lru_wa, v_lru_ba, v_lru_wx, v_lru_bx, v_lru_lambda, v_od_w_in, v_od_w_out):
    w = dict(ln_mix_w=ln_mix_w, ln_mlp_w=ln_mlp_w, ln_ple_w=ln_ple_w, w_up=w_up, w_down=w_down,
             w_ple_proj=w_ple_proj, w_ple_gate=w_ple_gate, ln_final_w=ln_final_w, ev_w_in=ev_w_in,
             ev_w_out=ev_w_out, dn_conv_w=dn_conv_w, dn_a_log=dn_a_log, dn_dt_bias=dn_dt_bias,
             dn_norm_w=dn_norm_w, lru_conv_w=lru_conv_w, lru_conv_b=lru_conv_b, lru_wa=lru_wa, lru_ba=lru_ba,
             lru_wx=lru_wx, lru_bx=lru_bx, lru_lambda=lru_lambda, od_w_in=od_w_in, od_w_out=od_w_out)
    m = dict(ln_mix_w=m_ln_mix_w, ln_mlp_w=m_ln_mlp_w, ln_ple_w=m_ln_ple_w, w_up=m_w_up, w_down=m_w_down,
             w_ple_proj=m_w_ple_proj, w_ple_gate=m_w_ple_gate, ln_final_w=m_ln_final_w, ev_w_in=m_ev_w_in,
             ev_w_out=m_ev_w_out, dn_conv_w=m_dn_conv_w, dn_a_log=m_dn_a_log, dn_dt_bias=m_dn_dt_bias,
             dn_norm_w=m_dn_norm_w, lru_conv_w=m_lru_conv_w, lru_conv_b=m_lru_conv_b, lru_wa=m_lru_wa,
             lru_ba=m_lru_ba, lru_wx=m_lru_wx, lru_bx=m_lru_bx, lru_lambda=m_lru_lambda, od_w_in=m_od_w_in,
             od_w_out=m_od_w_out)
    v = dict(ln_mix_w=v_ln_mix_w, ln_mlp_w=v_ln_mlp_w, ln_ple_w=v_ln_ple_w, w_up=v_w_up, w_down=v_w_down,
             w_ple_proj=v_w_ple_proj, w_ple_gate=v_w_ple_gate, ln_final_w=v_ln_final_w, ev_w_in=v_ev_w_in,
             ev_w_out=v_ev_w_out, dn_conv_w=v_dn_conv_w, dn_a_log=v_dn_a_log, dn_dt_bias=v_dn_dt_bias,
             dn_norm_w=v_dn_norm_w, lru_conv_w=v_lru_conv_w, lru_conv_b=v_lru_conv_b, lru_wa=v_lru_wa,
             lru_ba=v_lru_ba, lru_wx=v_lru_wx, lru_bx=v_lru_bx, lru_lambda=v_lru_lambda, od_w_in=v_od_w_in,
             od_w_out=v_od_w_out)
    me = 4 * lax.axis_index("x") + 2 * lax.axis_index("y") + lax.axis_index("c")

    big = {}
    for n in BIG:
        shards = _gather_layers(w[n].astype(BF16), "gather_" + n)
        big[n] = [s if BIG_COL_SHARDED[n] else s.reshape(-1, s.shape[2]) for s in shards]
    conv_shapes = [w[n].shape for n in SMALL_SHARDED]
    conv_g = _all_gather(_pack([w[n] for n in SMALL_SHARDED], F32), "gather_conv")
    conv_dev = [_unpack(conv_g[s], conv_shapes) for s in range(N_DEV)]
    small = {n: w[n] for n in SMALL_REPLICATED}
    for i, n in enumerate(SMALL_SHARDED):
        small[n] = jnp.concatenate([conv_dev[s][i] for s in range(N_DEV)], axis=-1)

    loss_local, (g_shards, g_small, g_x) = jax.value_and_grad(_local_loss, argnums=(0, 1, 2))(
        {n: w[n] for n in FFN + BIG}, small, x[0], big, p[:, 0], loss_target[0])
    loss = lax.psum(loss_local, ("x", "y", "c"))

    out = {}
    for n in FFN + BIG:
        nl, r, c = w[n].shape
        res = _adamw(g_shards[n].reshape(1, nl * r, c), w[n].reshape(nl * r, c), m[n].reshape(nl * r, c),
                     v[n].reshape(nl * r, c), "adamw_" + n)
        for kind, buf in zip(("grad", "delta", "new_m", "new_v"), res):
            out[kind, n] = buf.reshape(nl, r, c)

    small_names = SMALL_REPLICATED + SMALL_SHARDED
    small_shapes = [small[n].shape for n in small_names]
    all_small = _all_gather(_pack([g_small[n] for n in small_names], F32), "gather_small_grads")
    total = dict(zip(small_names, _unpack(_slot_sum(all_small, "sum_small_grads"), small_shapes)))
    for n in SMALL_SHARDED:
        width = w[n].shape[-1]
        total[n] = lax.dynamic_slice_in_dim(total[n], me * width, width, axis=-1)
    own_shapes = [w[n].shape for n in small_names]
    res_small = _adamw(_pack([total[n] for n in small_names], F32)[None], _pack([w[n] for n in small_names], F32),
                       _pack([m[n] for n in small_names], F32), _pack([v[n] for n in small_names], F32), "adamw_small")
    for kind, buf in zip(("grad", "delta", "new_m", "new_v"), res_small):
        for n, a in zip(small_names, _unpack(buf, own_shapes)):
            out[kind, n] = a

    return (loss, g_x[None], *[out["grad", n] for n in WEIGHTS], *[out["delta", n] for n in WEIGHTS],
            *[out["new_m", n] for n in WEIGHTS], *[out["new_v", n] for n in WEIGHTS])
```
